```python
import math
import jax, jax.numpy as jnp
from jax import lax
import numpy as np

D_MODEL = 2048
BATCH = 1
SEQ = 8192
DEPTH = 4

CHUNK = 64
Q_BLOCK = 128
N_HEADS = 4
HEAD_DIM = D_MODEL // 16
BRANCH_W = N_HEADS * HEAD_DIM
GLA_DK = HEAD_DIM // 2
GLA_KW = N_HEADS * GLA_DK
GLA_GATE_RANK = 16
GLA_GATE_NORM = 16.0
N_BRANCH = 4
N_GROUPS = 4
EXPERTS_PER_GROUP = 8
N_EXPERTS = N_GROUPS * EXPERTS_PER_GROUP
TOP_K = 2
D_EXPERT = D_MODEL // 4
MOE_BLOCK = 128
ROPE_BASE = 10000.0
LN_EPS = 1e-5
FOX_FORGET_BIAS_MEAN = 3.0
DEEPNORM_ALPHA = (2.0 * DEPTH) ** 0.25
DEEPNORM_BETA = (8.0 * DEPTH) ** -0.25

IN_SPLITS = (
    BRANCH_W, BRANCH_W, BRANCH_W, N_HEADS,
    BRANCH_W, BRANCH_W, BRANCH_W, BRANCH_W,
    BRANCH_W, BRANCH_W, BRANCH_W,
    GLA_KW, GLA_KW, BRANCH_W, BRANCH_W, GLA_GATE_RANK,
)
V_SEGMENTS = (2, 6, 10, 13)
IN_WIDTH = sum(IN_SPLITS)

kernel_name = 'hybrid_fox_ret_sb_gla_hmoe_deepnorm'

F32 = jnp.float32


def layer_norm(x, g, b):
    xf = x.astype(F32)
    mu = jnp.mean(xf, axis=-1, keepdims=True)
    var = jnp.mean(jnp.square(xf - mu), axis=-1, keepdims=True)
    return ((xf - mu) * lax.rsqrt(var + LN_EPS) * g + b).astype(x.dtype)


def split_heads(t, n):
    B, S, W = t.shape
    return t.reshape(B, S, n, W // n).transpose(0, 2, 1, 3)


def merge_heads(t):
    B, H, S, d = t.shape
    return t.transpose(0, 2, 1, 3).reshape(B, S, H * d)


def head_group_norm(t):
    mu = jnp.mean(t, axis=-1, keepdims=True)
    var = jnp.mean(jnp.square(t - mu), axis=-1, keepdims=True)
    return (t - mu) * lax.rsqrt(var + LN_EPS)


def head_rms_norm(t):
    return t * lax.rsqrt(jnp.mean(jnp.square(t), axis=-1, keepdims=True) + LN_EPS)


def rotary(t, pos):
    half = t.shape[-1] // 2
    inv = ROPE_BASE ** (-jnp.arange(half, dtype=F32) / half)
    ang = pos.astype(F32)[:, None] * inv[None, :]
    cos, sin = jnp.cos(ang), jnp.sin(ang)
    t1, t2 = t[..., :half], t[..., half:]
    return jnp.concatenate([t1 * cos - t2 * sin, t1 * sin + t2 * cos], axis=-1)


def forgetting_attention(q, k, v, logf):
    B, H, S, d = q.shape
    nb = S // Q_BLOCK
    F = jnp.cumsum(logf, axis=-1)
    qb = q.reshape(B, H, nb, Q_BLOCK, d).transpose(2, 0, 1, 3, 4)
    Fb = F.reshape(B, H, nb, Q_BLOCK).transpose(2, 0, 1, 3)
    starts = jnp.arange(nb) * Q_BLOCK
    kpos = jnp.arange(S)
    scale = d ** -0.5

    def block(args):
        qi, Fi, s0 = args
        logits = jnp.einsum('bhqd,bhkd->bhqk', qi, k).astype(F32) * scale
        logits = logits + Fi[..., :, None] - F[..., None, :]
        tpos = s0 + jnp.arange(Q_BLOCK)
        mask = kpos[None, :] <= tpos[:, None]
        p = jax.nn.softmax(jnp.where(mask, logits, -jnp.inf), axis=-1)
        return jnp.einsum('bhqk,bhkd->bhqd', p.astype(v.dtype), v)

    out = lax.map(block, (qb, Fb, starts))
    return out.transpose(1, 2, 0, 3, 4).reshape(B, H, S, d)


def stick_breaking_attention(q, k, v):
    B, H, S, d = q.shape
    nb = S // Q_BLOCK
    qb = q.reshape(B, H, nb, Q_BLOCK, d).transpose(2, 0, 1, 3, 4)
    starts = jnp.arange(nb) * Q_BLOCK
    kpos = jnp.arange(S)
    scale = d ** -0.5

    def block(args):
        qi, s0 = args
        z = jnp.einsum('bhqd,bhkd->bhqk', qi, k).astype(F32) * scale
        tpos = s0 + jnp.arange(Q_BLOCK)
        mask = kpos[None, :] < tpos[:, None]
        log_one_minus = jnp.where(mask, -jax.nn.softplus(z), 0.0)
        rev = lax.cumsum(log_one_minus, axis=log_one_minus.ndim - 1, reverse=True)
        after = jnp.concatenate([rev[..., 1:], jnp.zeros_like(rev[..., :1])], axis=-1)
        A = jnp.where(mask, jnp.exp(jax.nn.log_sigmoid(z) + after), 0.0)
        return jnp.einsum('bhqk,bhkd->bhqd', A.astype(v.dtype), v)

    out = lax.map(block, (qb, starts))
    return out.transpose(1, 2, 0, 3, 4).reshape(B, H, S, d)


def retention_chunkwise(q, k, v):
    B, H, S, d = q.shape
    nc = S // CHUNK
    log_g = jnp.log(1.0 - jnp.exp2(-5.0 - jnp.arange(H, dtype=F32)))
    k = k * d ** -0.5
    qc = q.reshape(B, H, nc, CHUNK, d)
    kc = k.reshape(B, H, nc, CHUNK, d)
    vc = v.reshape(B, H, nc, CHUNK, d)
    idx = jnp.arange(CHUNK, dtype=F32)
    intra_decay = jnp.exp(log_g[:, None, None] * jnp.abs(idx[:, None] - idx[None, :]))
    scores = jnp.einsum('bhnid,bhnjd->bhnij', qc, kc) * intra_decay[None, :, None]
    intra = jnp.einsum('bhnij,bhnje->bhnie', scores, vc)
    k_dec = jnp.exp(log_g[:, None] * (CHUNK - 1.0 - idx))[None, :, None, :, None]
    chunk_kv = jnp.einsum('bhnjd,bhnje->bhnde', kc * k_dec, vc)
    g_chunk = jnp.exp(log_g * CHUNK)[None, :, None, None]

    def step(state, kv):
        return state * g_chunk + kv, state

    _, r_prev = lax.scan(step, jnp.zeros((B, H, d, d), F32), jnp.moveaxis(chunk_kv, 2, 0))
    r_prev = jnp.moveaxis(r_prev, 0, 2)
    q_dec = jnp.exp(log_g[:, None] * (idx + 1.0))[None, :, None, :, None]
    inter = jnp.einsum('bhnid,bhnde->bhnie', qc * q_dec, r_prev)
    return (intra + inter).reshape(B, H, S, d)


def gla_chunked(q, k, v, log_a):
    B, H, S, dk = q.shape
    dv = v.shape[-1]
    nc = S // CHUNK
    q = q.reshape(B, H, nc, CHUNK, dk)
    k = k.reshape(B, H, nc, CHUNK, dk)
    v = v.reshape(B, H, nc, CHUNK, dv)
    b = jnp.cumsum(log_a.reshape(B, H, nc, CHUNK, dk), axis=3)
    b_last = b[:, :, :, -1:, :]
    eb, inv_eb = jnp.exp(b), jnp.exp(-b)
    i = jnp.arange(CHUNK)
    causal = i[:, None] >= i[None, :]
    s_past = jnp.einsum('bhnid,bhnjd->bhnij', q * eb, k * inv_eb)
    s_future = jnp.einsum('bhnid,bhnjd->bhnij', q * inv_eb, k * eb)
    scores = jnp.where(causal, s_past, s_future)
    intra = jnp.einsum('bhnij,bhnje->bhnie', scores, v)
    chunk_kv = jnp.einsum('bhnjd,bhnje->bhnde', k * jnp.exp(b_last - b), v)
    chunk_decay = jnp.exp(b_last[:, :, :, 0, :])

    def step(state, inp):
        kv, dec = inp
        return state * dec[..., :, None] + kv, state

    _, s_prev = lax.scan(step, jnp.zeros((B, H, dk, dv), F32),
                         (jnp.moveaxis(chunk_kv, 2, 0), jnp.moveaxis(chunk_decay, 2, 0)))
    s_prev = jnp.moveaxis(s_prev, 0, 2)
    inter = jnp.einsum('bhnid,bhnde->bhnie', q * eb, s_prev)
    return (intra + inter).reshape(B, H, S, dv)


def gated_parallel_mixer(x, w_in, fox_forget_bias, gla_gate_up, gla_gate_bias,
                         w_branch, w_gate, b_gate, w_out):
    B, S, D = x.shape
    proj = x @ w_in
    cuts = [int(c) for c in np.cumsum(IN_SPLITS)[:-1]]
    (fq, fk, fv, ff, rq, rk, rv, rg, sq, sk, sv,
     gq, gk, gv, gr, ga) = jnp.split(proj, cuts, axis=-1)

    logf = jax.nn.log_sigmoid((ff + fox_forget_bias).astype(F32)).transpose(0, 2, 1)
    y_fox = merge_heads(forgetting_attention(split_heads(fq, N_HEADS), split_heads(fk, N_HEADS),
                                             split_heads(fv, N_HEADS), logf))

    pos = jnp.arange(S)
    r_q = rotary(split_heads(rq, N_HEADS).astype(F32), pos)
    r_k = rotary(split_heads(rk, N_HEADS).astype(F32), pos)
    r_o = head_group_norm(retention_chunkwise(r_q, r_k, split_heads(rv, N_HEADS).astype(F32)))
    y_ret = (merge_heads(r_o) * jax.nn.silu(rg.astype(F32))).astype(x.dtype)

    y_sb = merge_heads(stick_breaking_attention(split_heads(sq, N_HEADS), split_heads(sk, N_HEADS),
                                                split_heads(sv, N_HEADS)))

    log_a = jax.nn.log_sigmoid((ga @ gla_gate_up + gla_gate_bias).astype(F32)) / GLA_GATE_NORM
    g_o = gla_chunked(split_heads(gq, N_HEADS).astype(F32) * GLA_DK ** -0.5,
                      split_heads(gk, N_HEADS).astype(F32),
                      split_heads(gv, N_HEADS).astype(F32),
                      split_heads(log_a, N_HEADS))
    y_gla = (merge_heads(head_rms_norm(g_o)) * jax.nn.silu(gr.astype(F32))).astype(x.dtype)

    merged = jnp.zeros_like(x)
    for n, y in enumerate((y_fox, y_ret, y_sb, y_gla)):
        gate = jax.nn.sigmoid(x @ w_gate[n] + b_gate[n])
        merged = merged + gate * (y @ w_branch[n])
    return merged @ w_out


def grouped_swiglu_experts(xt, expert_id, weight, w1, w3, w2):
    T, D = xt.shape
    flat_e = expert_id.reshape(-1)
    flat_tok = jnp.repeat(jnp.arange(T, dtype=jnp.int32), TOP_K)
    flat_w = weight.reshape(-1)
    n_assign = flat_e.shape[0]
    order = jnp.argsort(flat_e)
    se = flat_e[order]
    counts = jnp.bincount(flat_e, length=N_EXPERTS)
    padded = (counts + MOE_BLOCK - 1) // MOE_BLOCK * MOE_BLOCK
    start = jnp.cumsum(counts) - counts
    pend = jnp.cumsum(padded)
    pstart = pend - padded
    dest = pstart[se] + (jnp.arange(n_assign) - start[se])
    n_rows = n_assign + N_EXPERTS * MOE_BLOCK
    n_blocks = n_rows // MOE_BLOCK
    row_tok = jnp.zeros((n_rows,), jnp.int32).at[dest].set(flat_tok[order])
    row_w = jnp.zeros((n_rows,), weight.dtype).at[dest].set(flat_w[order])
    blk_e = jnp.minimum(jnp.searchsorted(pend, jnp.arange(n_blocks) * MOE_BLOCK, side='right'),
                        N_EXPERTS - 1)
    xb = xt[row_tok].reshape(n_blocks, MOE_BLOCK, D)

    def run(args):
        xi, e = args
        h = jax.nn.silu(xi @ w1[e]) * (xi @ w3[e])
        return h @ w2[e]

    yb = lax.map(run, (xb, blk_e)).reshape(n_rows, D)
    y = yb * row_w[:, None].astype(yb.dtype)
    return jnp.zeros_like(xt).at[row_tok].add(y.astype(xt.dtype))


def hierarchical_moe(x, w_group, b_group, w_expert_router, b_expert_router, w1, w3, w2):
    B, S, D = x.shape
    xt = x.reshape(B * S, D)
    T = xt.shape[0]
    tok = jnp.arange(T)
    g_logits = (xt @ w_group).astype(F32) + b_group
    g_prob = jax.nn.softmax(g_logits, axis=-1)
    g_sel = jnp.argmax(g_logits, axis=-1)
    g_w = g_prob[tok, g_sel][:, None]
    e_logits = ((xt @ w_expert_router).astype(F32) + b_expert_router).reshape(T, N_GROUPS, EXPERTS_PER_GROUP)
    e_in = e_logits[tok, g_sel]
    top_v, top_i = lax.top_k(e_in, TOP_K)
    weight = (jax.nn.softmax(top_v, axis=-1) * g_w).astype(x.dtype)
    expert_id = (g_sel[:, None] * EXPERTS_PER_GROUP + top_i).astype(jnp.int32)
    return grouped_swiglu_experts(xt, expert_id, weight, w1, w3, w2).reshape(B, S, D)


def setup_inputs(seed: int = 0) -> dict:
    key = jax.random.key(seed)
    ks = jax.random.split(key, 20)

    def nrm(k, shape, scale):
        return jax.random.normal(k, shape, F32) * scale

    offs = np.concatenate([[0], np.cumsum(IN_SPLITS)])
    col_scale = np.ones((IN_WIDTH,), np.float32)
    for s in V_SEGMENTS:
        col_scale[offs[s]:offs[s + 1]] = DEEPNORM_BETA
    col_scale = jnp.asarray(col_scale)
    L, D = DEPTH, D_MODEL
    return {
        'x': nrm(ks[0], (BATCH, SEQ, D), 1.0),
        'w_in': nrm(ks[1], (L, D, IN_WIDTH), D ** -0.5) * col_scale,
        'fox_forget_bias': FOX_FORGET_BIAS_MEAN + nrm(ks[2], (L, N_HEADS), 0.5),
        'gla_gate_up': nrm(ks[3], (L, GLA_GATE_RANK, GLA_KW), GLA_GATE_RANK ** -0.5),
        'gla_gate_bias': nrm(ks[4], (L, GLA_KW), 0.1),
        'w_branch': nrm(ks[5], (L, N_BRANCH, BRANCH_W, D), BRANCH_W ** -0.5 * DEEPNORM_BETA),
        'w_gate': nrm(ks[6], (L, N_BRANCH, D, D), D ** -0.5),
        'b_gate': nrm(ks[7], (L, N_BRANCH, D), 0.01),
        'w_out': nrm(ks[8], (L, D, D), D ** -0.5 * DEEPNORM_BETA),
        'ln1_g': 1.0 + nrm(ks[9], (L, D), 0.01),
        'ln1_b': nrm(ks[10], (L, D), 0.01),
        'w_group': nrm(ks[11], (L, D, N_GROUPS), D ** -0.5),
        'b_group': nrm(ks[12], (L, N_GROUPS), 0.01),
        'w_expert_router': nrm(ks[13], (L, D, N_EXPERTS), D ** -0.5),
        'b_expert_router': nrm(ks[14], (L, N_EXPERTS), 0.01),
        'w1': nrm(ks[15], (L, N_EXPERTS, D, D_EXPERT), D ** -0.5),
        'w3': nrm(ks[16], (L, N_EXPERTS, D, D_EXPERT), D ** -0.5),
        'w2': nrm(ks[17], (L, N_EXPERTS, D_EXPERT, D), D_EXPERT ** -0.5 * DEEPNORM_BETA),
        'ln2_g': 1.0 + nrm(ks[18], (L, D), 0.01),
        'ln2_b': nrm(ks[19], (L, D), 0.01),
    }


def reference(x, w_in, fox_forget_bias, gla_gate_up, gla_gate_bias, w_branch, w_gate, b_gate,
              w_out, ln1_g, ln1_b, w_group, b_group, w_expert_router, b_expert_router,
              w1, w3, w2, ln2_g, ln2_b):
    for l in range(DEPTH):
        h = gated_parallel_mixer(x, w_in[l], fox_forget_bias[l], gla_gate_up[l], gla_gate_bias[l],
                                 w_branch[l], w_gate[l], b_gate[l], w_out[l])
        x = layer_norm(DEEPNORM_ALPHA * x + h, ln1_g[l], ln1_b[l])
        h = hierarchical_moe(x, w_group[l], b_group[l], w_expert_router[l], b_expert_router[l],
                             w1[l], w3[l], w2[l])
        x = layer_norm(DEEPNORM_ALPHA * x + h, ln2_g[l], ln2_b[l])
    return x
```

```python
import functools
import math

import jax
import jax.numpy as jnp
from jax import lax
from jax.experimental import pallas as pl
from jax.experimental.pallas import tpu as pltpu

F32 = jnp.float32
BF16 = jnp.bfloat16

D_MODEL = 2048
CHUNK = 64
N_HEADS = 4
HEAD_DIM = 128
BRANCH_W = N_HEADS * HEAD_DIM
GLA_DK = 64
GLA_KW = N_HEADS * GLA_DK
GLA_GATE_RANK = 16
GLA_GATE_NORM = 16.0
N_BRANCH = 4
N_GROUPS = 4
EXPERTS_PER_GROUP = 8
N_EXPERTS = N_GROUPS * EXPERTS_PER_GROUP
TOP_K = 2
D_EXPERT = D_MODEL // 4
ROPE_BASE = 10000.0
LN_EPS = 1e-5
DEPTH = 4
DEEPNORM_ALPHA = (2.0 * DEPTH) ** 0.25

LANES = 128
NEG_BIG = -1e30
SB_EXIT = -104.0

C_FQ, C_FK, C_FV = 0, 512, 1024
C_RQ, C_RK, C_RV, C_RG = 1536, 2048, 2560, 3072
C_SQ, C_SK, C_SV = 3584, 4096, 4608
C_GQ, C_GK, C_GV, C_GR = 5120, 5376, 5632, 6144
MAIN_W = 6656
O_FF, O_GA = 1536, 6660
IN_WIDTH = 6676
S_FF, S_GA = 0, 4


def _cparams(sem, vmem_mb=None):
    kw = dict(dimension_semantics=sem)
    if vmem_mb is not None:
        kw["vmem_limit_bytes"] = vmem_mb * 1024 * 1024
    return pltpu.CompilerParams(**kw)


def _log_sigmoid_parts(z):
    t = jnp.log1p(jnp.exp(-jnp.abs(z)))
    return jnp.minimum(z, 0.0) - t, -(jnp.maximum(z, 0.0) + t)


def _split3(x):
    h1 = x.astype(BF16)
    r1 = x - h1.astype(F32)
    h2 = r1.astype(BF16)
    h3 = (r1 - h2.astype(F32)).astype(BF16)
    return h1, h2, h3


def _dot(a, b):
    return jnp.dot(a, b, preferred_element_type=F32)


def _dot_nt(a, b):
    return lax.dot_general(a, b, (((1,), (1,)), ((), ())), preferred_element_type=F32)


def _dot_tn(a, b):
    return lax.dot_general(a, b, (((0,), (0,)), ((), ())), preferred_element_type=F32)


def _mm_kernel(a_ref, w_ref, o_ref):
    o_ref[...] = _dot(a_ref[...], w_ref[...]).astype(o_ref.dtype)


def _matmul(a, w, out_dtype, bm, bn):
    m, k = a.shape
    n = w.shape[1]
    bm, bn = min(bm, m), min(bn, n)
    return pl.pallas_call(
        _mm_kernel,
        out_shape=jax.ShapeDtypeStruct((m, n), out_dtype),
        grid=(n // bn, m // bm),
        in_specs=[pl.BlockSpec((bm, k), lambda j, i: (i, 0)),
                  pl.BlockSpec((k, bn), lambda j, i: (0, j))],
        out_specs=pl.BlockSpec((bm, bn), lambda j, i: (i, j)),
        compiler_params=_cparams(("arbitrary", "arbitrary"), 48),
        name="matmul",
    )(a, w)


def _fcum_kernel(ps_ref, bias_ref, tri_ref, o_ref, carry_ref):
    @pl.when(pl.program_id(0) == 0)
    def _():
        carry_ref[...] = jnp.zeros_like(carry_ref)

    lf, _ = _log_sigmoid_parts(ps_ref[...] + bias_ref[...])
    tri = tri_ref[...]
    h1, h2, h3 = _split3(lf)
    c = _dot(tri, h1) + _dot(tri, h2) + _dot(tri, h3) + carry_ref[0:1, :]
    o_ref[...] = c
    carry_ref[...] = jnp.broadcast_to(c[-1:, :], carry_ref.shape)


def _forget_cumsum(ps, bias_row, tb):
    s = ps.shape[0]
    tb = min(tb, s)
    r = lax.broadcasted_iota(jnp.int32, (tb, tb), 0)
    c = lax.broadcasted_iota(jnp.int32, (tb, tb), 1)
    tri = (c <= r).astype(BF16)
    return pl.pallas_call(
        _fcum_kernel,
        out_shape=jax.ShapeDtypeStruct((s, LANES), F32),
        grid=(s // tb,),
        in_specs=[pl.BlockSpec((tb, LANES), lambda i: (i, 0)),
                  pl.BlockSpec((1, LANES), lambda i: (0, 0)),
                  pl.BlockSpec((tb, tb), lambda i: (0, 0))],
        out_specs=pl.BlockSpec((tb, LANES), lambda i: (i, 0)),
        scratch_shapes=[pltpu.VMEM((8, LANES), F32)],
        compiler_params=_cparams(("arbitrary",)),
        name="forget_cumsum",
    )(ps, bias_row, tri)


def _fox_kernel(qi_ref, ki_ref, q_ref, k_ref, v_ref, fq_ref, fk_ref, o_ref, m_ref, l_ref, acc_ref, *, t):
    p_id = pl.program_id(0)
    qi = qi_ref[p_id]
    ki = ki_ref[p_id]

    @pl.when(ki == qi)
    def _():
        m_ref[...] = jnp.full_like(m_ref, NEG_BIG)
        l_ref[...] = jnp.zeros_like(l_ref)
        acc_ref[...] = jnp.zeros_like(acc_ref)

    row = qi * t + lax.broadcasted_iota(jnp.int32, (t, t), 0)
    col = ki * t + lax.broadcasted_iota(jnp.int32, (t, t), 1)
    keep = col <= row
    scale = HEAD_DIM ** -0.5
    for h in range(N_HEADS):
        sl = slice(h * HEAD_DIM, (h + 1) * HEAD_DIM)
        s = _dot_nt(q_ref[:, sl], k_ref[:, sl]) * scale
        s = s + (fq_ref[:, h:h + 1] - fk_ref[h:h + 1, :])
        s = jnp.where(keep, s, NEG_BIG)
        m_prev = m_ref[h]
        m_new = jnp.maximum(m_prev, jnp.max(s, axis=-1, keepdims=True))
        alpha = jnp.exp(m_prev - m_new)
        p = jnp.exp(s - m_new)
        l_ref[h] = alpha * l_ref[h] + jnp.sum(p, axis=-1, keepdims=True)
        acc_ref[h] = alpha * acc_ref[h] + _dot(p.astype(BF16), v_ref[:, sl])
        m_ref[h] = m_new

    @pl.when(ki == 0)
    def _():
        for h in range(N_HEADS):
            sl = slice(h * HEAD_DIM, (h + 1) * HEAD_DIM)
            o_ref[:, sl] = (acc_ref[h] / l_ref[h]).astype(o_ref.dtype)


def _fox_attention(pm, fcol, frow, t):
    s = pm.shape[0]
    t = min(t, s)
    nb = s // t
    pairs = [(i, j) for i in range(nb) for j in range(i, -1, -1)]
    qi_tab = jnp.asarray([p[0] for p in pairs], jnp.int32)
    ki_tab = jnp.asarray([p[1] for p in pairs], jnp.int32)
    cq, ck, cv = C_FQ // BRANCH_W, C_FK // BRANCH_W, C_FV // BRANCH_W
    grid_spec = pltpu.PrefetchScalarGridSpec(
        num_scalar_prefetch=2,
        grid=(len(pairs),),
        in_specs=[pl.BlockSpec((t, BRANCH_W), lambda p, qi, ki: (qi[p], cq)),
                  pl.BlockSpec((t, BRANCH_W), lambda p, qi, ki: (ki[p], ck)),
                  pl.BlockSpec((t, BRANCH_W), lambda p, qi, ki: (ki[p], cv)),
                  pl.BlockSpec((t, LANES), lambda p, qi, ki: (qi[p], 0)),
                  pl.BlockSpec((8, t), lambda p, qi, ki: (0, ki[p]))],
        out_specs=pl.BlockSpec((t, BRANCH_W), lambda p, qi, ki: (qi[p], 0)),
        scratch_shapes=[pltpu.VMEM((N_HEADS, t, 1), F32),
                        pltpu.VMEM((N_HEADS, t, 1), F32),
                        pltpu.VMEM((N_HEADS, t, HEAD_DIM), F32)],
    )
    return pl.pallas_call(
        functools.partial(_fox_kernel, t=t),
        out_shape=jax.ShapeDtypeStruct((s, BRANCH_W), BF16),
        grid_spec=grid_spec,
        compiler_params=_cparams(("arbitrary",), 48),
        name="fox_attention",
    )(qi_tab, ki_tab, pm, pm, pm, fcol, frow)


def _sb_kernel(q_ref, k_ref, v_ref, gt_ref, o_ref, r_ref, acc_ref, *, t):
    i = pl.program_id(0)
    r_ref[...] = jnp.zeros_like(r_ref)
    acc_ref[...] = jnp.zeros_like(acc_ref)
    scale = HEAD_DIM ** -0.5
    gt = gt_ref[...]

    def cond(c):
        jj, rmax = c
        return jnp.logical_and(jj <= i, rmax > SB_EXIT)

    def body(c):
        jj, _ = c
        kb = i - jj
        k0 = pl.multiple_of(kb * t, t)
        row = i * t + lax.broadcasted_iota(jnp.int32, (t, t), 0)
        col = kb * t + lax.broadcasted_iota(jnp.int32, (t, t), 1)
        keep = col < row
        rmax = jnp.full((1, 1), -jnp.inf, F32)
        for h in range(N_HEADS):
            sl = slice(h * HEAD_DIM, (h + 1) * HEAD_DIM)
            z = _dot_nt(q_ref[:, sl], k_ref[pl.ds(k0, t), sl]) * scale
            lsg, lom = _log_sigmoid_parts(z)
            lom = jnp.where(keep, lom, 0.0)
            hi = lom.astype(BF16)
            lo = (lom - hi.astype(F32)).astype(BF16)
            r_prev = r_ref[h]
            after = _dot(hi, gt) + _dot(lo, gt) + r_prev
            a = jnp.where(keep, jnp.exp(lsg + after), 0.0)
            acc_ref[h] += _dot(a.astype(BF16), v_ref[pl.ds(k0, t), sl])
            r_new = r_prev + jnp.sum(lom, axis=-1, keepdims=True)
            r_ref[h] = r_new
            rmax = jnp.maximum(rmax, jnp.max(r_new, axis=0, keepdims=True))
        return jj + 1, rmax[0, 0]

    lax.while_loop(cond, body, (jnp.int32(0), jnp.float32(0.0)))
    for h in range(N_HEADS):
        sl = slice(h * HEAD_DIM, (h + 1) * HEAD_DIM)
        o_ref[:, sl] = acc_ref[h].astype(o_ref.dtype)


def _sb_attention(pm, t):
    s = pm.shape[0]
    t = min(t, s)
    r = lax.broadcasted_iota(jnp.int32, (t, t), 0)
    c = lax.broadcasted_iota(jnp.int32, (t, t), 1)
    gt = (r > c).astype(BF16)
    return pl.pallas_call(
        functools.partial(_sb_kernel, t=t),
        out_shape=jax.ShapeDtypeStruct((s, BRANCH_W), BF16),
        grid=(s // t,),
        in_specs=[pl.BlockSpec((t, BRANCH_W), lambda i: (i, C_SQ // BRANCH_W)),
                  pl.BlockSpec((s, BRANCH_W), lambda i: (0, C_SK // BRANCH_W)),
                  pl.BlockSpec((s, BRANCH_W), lambda i: (0, C_SV // BRANCH_W)),
                  pl.BlockSpec((t, t), lambda i: (0, 0))],
        out_specs=pl.BlockSpec((t, BRANCH_W), lambda i: (i, 0)),
        scratch_shapes=[pltpu.VMEM((N_HEADS, t, 1), F32),
                        pltpu.VMEM((N_HEADS, t, HEAD_DIM), F32)],
        compiler_params=_cparams(("arbitrary",), 48),
        name="sb_attention",
    )(pm, pm, pm, gt)


def _ret_kernel(q_ref, k_ref, v_ref, g_ref, cos_ref, sin_ref, o_ref, st_ref, *, tb):
    @pl.when(pl.program_id(0) == 0)
    def _():
        st_ref[...] = jnp.zeros_like(st_ref)

    cos = cos_ref[...]
    sin = sin_ref[...]
    ri = lax.broadcasted_iota(jnp.int32, (tb, tb), 0)
    ci = lax.broadcasted_iota(jnp.int32, (tb, tb), 1)
    chunk_ok = (ci // CHUNK) <= (ri // CHUNK)
    dist = jnp.abs(ri - ci).astype(F32)
    idx = lax.broadcasted_iota(jnp.int32, (tb, 1), 0).astype(F32)
    scale = HEAD_DIM ** -0.5
    for h in range(N_HEADS):
        sl = slice(h * HEAD_DIM, (h + 1) * HEAD_DIM)
        lg = math.log(1.0 - 2.0 ** (-5.0 - h))
        q = q_ref[:, sl].astype(F32)
        k = k_ref[:, sl].astype(F32)
        v = v_ref[:, sl]
        qr = q * cos + pltpu.roll(q, HEAD_DIM // 2, 1) * sin
        kr = (k * cos + pltpu.roll(k, HEAD_DIM // 2, 1) * sin) * scale
        decay = jnp.where(chunk_ok, jnp.exp(lg * dist), 0.0)
        scores = _dot_nt(qr.astype(BF16), kr.astype(BF16)) * decay
        intra = _dot(scores.astype(BF16), v)
        q_dec = jnp.exp(lg * (idx + 1.0))
        k_dec = jnp.exp(lg * (tb - 1.0 - idx))
        state = st_ref[h]
        inter = _dot((qr * q_dec).astype(BF16), state.astype(BF16))
        kv = _dot_tn((kr * k_dec).astype(BF16), v)
        st_ref[h] = state * math.exp(lg * tb) + kv
        o = intra + inter
        mu = jnp.mean(o, axis=-1, keepdims=True)
        d = o - mu
        var = jnp.mean(d * d, axis=-1, keepdims=True)
        on = d * lax.rsqrt(var + LN_EPS)
        g = g_ref[:, sl].astype(F32)
        o_ref[:, sl] = (on * (g * jax.nn.sigmoid(g))).astype(o_ref.dtype)


def _retention(pm, cos_t, sin_t, tb):
    s = pm.shape[0]
    tb = min(tb, s)
    blk = lambda c: pl.BlockSpec((tb, BRANCH_W), lambda i, c=c: (i, c // BRANCH_W))
    return pl.pallas_call(
        functools.partial(_ret_kernel, tb=tb),
        out_shape=jax.ShapeDtypeStruct((s, BRANCH_W), BF16),
        grid=(s // tb,),
        in_specs=[blk(C_RQ), blk(C_RK), blk(C_RV), blk(C_RG),
                  pl.BlockSpec((tb, HEAD_DIM), lambda i: (i, 0)),
                  pl.BlockSpec((tb, HEAD_DIM), lambda i: (i, 0))],
        out_specs=pl.BlockSpec((tb, BRANCH_W), lambda i: (i, 0)),
        scratch_shapes=[pltpu.VMEM((N_HEADS, HEAD_DIM, HEAD_DIM), F32)],
        compiler_params=_cparams(("arbitrary",)),
        name="retention",
    )(pm, pm, pm, pm, cos_t, sin_t)


def _gla_kernel(q_ref, k_ref, v_ref, g_ref, ps_ref, up_ref, gb_ref, bd_ref, o_ref, st_ref, *, tb):
    @pl.when(pl.program_id(0) == 0)
    def _():
        st_ref[...] = jnp.zeros_like(st_ref)

    pre = _dot(ps_ref[...].astype(BF16), up_ref[...].astype(BF16)) + gb_ref[...]
    la, _ = _log_sigmoid_parts(pre)
    la = la / GLA_GATE_NORM
    bd = bd_ref[...]
    h1, h2, h3 = _split3(la)
    b = _dot(bd, h1) + _dot(bd, h2) + _dot(bd, h3)
    eb = jnp.exp(b)
    ieb = jnp.exp(-b)
    scale = GLA_DK ** -0.5
    q = q_ref[...].astype(F32) * scale
    k = k_ref[...].astype(F32)
    qe = q * eb
    qi = q * ieb
    ke = (k * ieb).astype(BF16)
    kf = (k * eb).astype(BF16)
    lane = lax.broadcasted_iota(jnp.int32, (1, GLA_KW), 1)
    ri = lax.broadcasted_iota(jnp.int32, (CHUNK, CHUNK), 0)
    ci = lax.broadcasted_iota(jnp.int32, (CHUNK, CHUNK), 1)
    causal = ri >= ci
    for c in range(tb // CHUNK):
        rs = slice(c * CHUNK, (c + 1) * CHUNK)
        b_c = b[rs]
        b_last = b_c[CHUNK - 1:CHUNK, :]
        kd = k[rs] * jnp.exp(b_last - b_c)
        dec = jnp.exp(b_last)
        for h in range(N_HEADS):
            hm = jnp.logical_and(lane >= h * GLA_DK, lane < (h + 1) * GLA_DK)
            vs = slice(h * HEAD_DIM, (h + 1) * HEAD_DIM)
            qe_h = jnp.where(hm, qe[rs], 0.0).astype(BF16)
            qi_h = jnp.where(hm, qi[rs], 0.0).astype(BF16)
            s_past = _dot_nt(qe_h, ke[rs])
            s_future = _dot_nt(qi_h, kf[rs])
            scores = jnp.where(causal, s_past, s_future)
            v_c = v_ref[rs, vs]
            st = st_ref[h]
            o = _dot(scores.astype(BF16), v_c) + _dot_nt(qe_h, st.astype(BF16))
            kd_h = jnp.where(hm, kd, 0.0).astype(BF16)
            st_ref[h] = st * dec + _dot_tn(v_c, kd_h)
            on = o * lax.rsqrt(jnp.mean(o * o, axis=-1, keepdims=True) + LN_EPS)
            g = g_ref[rs, vs].astype(F32)
            o_ref[rs, vs] = (on * (g * jax.nn.sigmoid(g))).astype(o_ref.dtype)


def _gla(pm, ps, up_pad, gbias, tb):
    s = pm.shape[0]
    tb = min(tb, s)
    r = lax.broadcasted_iota(jnp.int32, (tb, tb), 0)
    c = lax.broadcasted_iota(jnp.int32, (tb, tb), 1)
    bd = jnp.logical_and(c <= r, (c // CHUNK) == (r // CHUNK)).astype(BF16)
    return pl.pallas_call(
        functools.partial(_gla_kernel, tb=tb),
        out_shape=jax.ShapeDtypeStruct((s, BRANCH_W), BF16),
        grid=(s // tb,),
        in_specs=[pl.BlockSpec((tb, GLA_KW), lambda i: (i, C_GQ // GLA_KW)),
                  pl.BlockSpec((tb, GLA_KW), lambda i: (i, C_GK // GLA_KW)),
                  pl.BlockSpec((tb, BRANCH_W), lambda i: (i, C_GV // BRANCH_W)),
                  pl.BlockSpec((tb, BRANCH_W), lambda i: (i, C_GR // BRANCH_W)),
                  pl.BlockSpec((tb, LANES), lambda i: (i, 0)),
                  pl.BlockSpec((LANES, GLA_KW), lambda i: (0, 0)),
                  pl.BlockSpec((1, GLA_KW), lambda i: (0, 0)),
                  pl.BlockSpec((tb, tb), lambda i: (0, 0))],
        out_specs=pl.BlockSpec((tb, BRANCH_W), lambda i: (i, 0)),
        scratch_shapes=[pltpu.VMEM((N_HEADS, HEAD_DIM, GLA_KW), F32)],
        compiler_params=_cparams(("arbitrary",)),
        name="gla",
    )(pm, pm, pm, pm, ps, up_pad, gbias, bd)


def _merge_kernel(x_ref, y0_ref, y1_ref, y2_ref, y3_ref, wg_ref, bg_ref, wb_ref, o_ref):
    x = x_ref[...]
    acc = None
    for n, y_ref in enumerate((y0_ref, y1_ref, y2_ref, y3_ref)):
        gate = jax.nn.sigmoid(_dot(x, wg_ref[n]) + bg_ref[n])
        term = gate * _dot(y_ref[...], wb_ref[n])
        acc = term if acc is None else acc + term
    o_ref[...] = acc.astype(o_ref.dtype)


def _merge(xb, ys, wg, bg, wb, bm, bn):
    m = xb.shape[0]
    bm = min(bm, m)
    yspec = pl.BlockSpec((bm, BRANCH_W), lambda j, i: (i, 0))
    return pl.pallas_call(
        _merge_kernel,
        out_shape=jax.ShapeDtypeStruct((m, D_MODEL), BF16),
        grid=(D_MODEL // bn, m // bm),
        in_specs=[pl.BlockSpec((bm, D_MODEL), lambda j, i: (i, 0)),
                  yspec, yspec, yspec, yspec,
                  pl.BlockSpec((N_BRANCH, D_MODEL, bn), lambda j, i: (0, 0, j)),
                  pl.BlockSpec((N_BRANCH, 1, bn), lambda j, i: (0, 0, j)),
                  pl.BlockSpec((N_BRANCH, BRANCH_W, bn), lambda j, i: (0, 0, j))],
        out_specs=pl.BlockSpec((bm, bn), lambda j, i: (i, j)),
        compiler_params=_cparams(("arbitrary", "arbitrary"), 52),
        name="merge",
    )(xb, *ys, wg, bg, wb)


def _layer_norm_rows(z, g, b):
    mu = jnp.mean(z, axis=-1, keepdims=True)
    d = z - mu
    var = jnp.mean(d * d, axis=-1, keepdims=True)
    return d * lax.rsqrt(var + LN_EPS) * g + b


def _outln_kernel(m_ref, w_ref, x_ref, g_ref, b_ref, o_ref, ob_ref, *, alpha):
    h = _dot(m_ref[...], w_ref[...])
    y = _layer_norm_rows(alpha * x_ref[...] + h, g_ref[...], b_ref[...])
    o_ref[...] = y
    ob_ref[...] = y.astype(BF16)


def _out_ln(merged, w_out, x, g, b, alpha, bm):
    m = x.shape[0]
    bm = min(bm, m)
    row = pl.BlockSpec((bm, D_MODEL), lambda i: (i, 0))
    vec = pl.BlockSpec((1, D_MODEL), lambda i: (0, 0))
    return pl.pallas_call(
        functools.partial(_outln_kernel, alpha=alpha),
        out_shape=(jax.ShapeDtypeStruct((m, D_MODEL), F32), jax.ShapeDtypeStruct((m, D_MODEL), BF16)),
        grid=(m // bm,),
        in_specs=[row, pl.BlockSpec((D_MODEL, D_MODEL), lambda i: (0, 0)), row, vec, vec],
        out_specs=(row, row),
        compiler_params=_cparams(("arbitrary",), 48),
        name="out_ln",
    )(merged, w_out, x, g, b)


def _router_kernel(x_ref, w_ref, b_ref, id_ref, wt_ref):
    x = x_ref[...]
    xh = x.astype(BF16)
    xl = (x - xh.astype(F32)).astype(BF16)
    w = w_ref[...]
    wh = w.astype(BF16)
    wl = (w - wh.astype(F32)).astype(BF16)
    logits = _dot(xh, wh) + _dot(xh, wl) + _dot(xl, wh) + b_ref[...]
    lane = lax.broadcasted_iota(jnp.int32, logits.shape, 1)
    neg = -jnp.inf
    gl = jnp.where(lane < N_GROUPS, logits, neg)
    gmax = jnp.max(gl, axis=-1, keepdims=True)
    g_sel = jnp.min(jnp.where(gl == gmax, lane, LANES), axis=-1, keepdims=True)
    g_w = 1.0 / jnp.sum(jnp.where(lane < N_GROUPS, jnp.exp(logits - gmax), 0.0), axis=-1, keepdims=True)
    lo = N_GROUPS + g_sel * EXPERTS_PER_GROUP
    el = jnp.where(jnp.logical_and(lane >= lo, lane < lo + EXPERTS_PER_GROUP), logits, neg)
    v1 = jnp.max(el, axis=-1, keepdims=True)
    i1 = jnp.min(jnp.where(el == v1, lane, LANES), axis=-1, keepdims=True)
    el2 = jnp.where(lane == i1, neg, el)
    v2 = jnp.max(el2, axis=-1, keepdims=True)
    i2 = jnp.min(jnp.where(el2 == v2, lane, LANES), axis=-1, keepdims=True)
    e2 = jnp.exp(v2 - v1)
    p1 = 1.0 / (1.0 + e2)
    p2 = e2 / (1.0 + e2)
    id_ref[...] = jnp.where(lane == 0, i1 - N_GROUPS, jnp.where(lane == 1, i2 - N_GROUPS, 0))
    wt_ref[...] = jnp.where(lane == 0, p1 * g_w, jnp.where(lane == 1, p2 * g_w, 0.0))


def _router(x1, w_r, b_r, bm):
    m = x1.shape[0]
    bm = min(bm, m)
    return pl.pallas_call(
        _router_kernel,
        out_shape=(jax.ShapeDtypeStruct((m, LANES), jnp.int32), jax.ShapeDtypeStruct((m, LANES), F32)),
        grid=(m // bm,),
        in_specs=[pl.BlockSpec((bm, D_MODEL), lambda i: (i, 0)),
                  pl.BlockSpec((D_MODEL, LANES), lambda i: (0, 0)),
                  pl.BlockSpec((1, LANES), lambda i: (0, 0))],
        out_specs=(pl.BlockSpec((bm, LANES), lambda i: (i, 0)), pl.BlockSpec((bm, LANES), lambda i: (i, 0))),
        compiler_params=_cparams(("arbitrary",)),
        name="router",
    )(x1, w_r, b_r)


def _gather_kernel(idx_ref, src_ref, o_ref, sem, *, tm):
    base = pl.program_id(0) * tm

    def row_copy(r, src_row):
        return pltpu.make_async_copy(src_ref.at[pl.ds(src_row, 1), :], o_ref.at[pl.ds(r, 1), :], sem)

    def start(r, c):
        row_copy(r, idx_ref[base + r]).start()
        return c

    def wait(r, c):
        row_copy(r, 0).wait()
        return c

    lax.fori_loop(0, tm, start, 0)
    lax.fori_loop(0, tm, wait, 0)


def _gather_rows(idx, src, tm):
    n = idx.shape[0]
    d = src.shape[1]
    grid_spec = pltpu.PrefetchScalarGridSpec(
        num_scalar_prefetch=1,
        grid=(n // tm,),
        in_specs=[pl.BlockSpec(memory_space=pl.ANY)],
        out_specs=pl.BlockSpec((tm, d), lambda i, idx: (i, 0)),
        scratch_shapes=[pltpu.SemaphoreType.DMA],
    )
    return pl.pallas_call(
        functools.partial(_gather_kernel, tm=tm),
        out_shape=jax.ShapeDtypeStruct((n, d), src.dtype),
        grid_spec=grid_spec,
        compiler_params=_cparams(("arbitrary",)),
        name="gather_rows",
    )(idx, src)


def _expert_kernel(be_ref, nv_ref, x_ref, w1_ref, w3_ref, w2_ref, o_ref, w1b, w3b, w2b):
    i = pl.program_id(0)
    e = be_ref[i]
    prev = be_ref[jnp.maximum(i - 1, 0)]

    @pl.when(jnp.logical_or(i == 0, e != prev))
    def _():
        w1b[...] = w1_ref[0].astype(BF16)
        w3b[...] = w3_ref[0].astype(BF16)
        w2b[...] = w2_ref[0].astype(BF16)

    @pl.when(i < nv_ref[0])
    def _():
        x = x_ref[...].astype(BF16)
        a = _dot(x, w1b[...])
        h = (a * jax.nn.sigmoid(a)) * _dot(x, w3b[...])
        o_ref[...] = _dot(h.astype(BF16), w2b[...])

    @pl.when(i >= nv_ref[0])
    def _():
        o_ref[...] = jnp.zeros_like(o_ref)


def _experts(blk_e, n_valid, xs, w1, w3, w2, tm):
    n = xs.shape[0]
    grid_spec = pltpu.PrefetchScalarGridSpec(
        num_scalar_prefetch=2,
        grid=(n // tm,),
        in_specs=[pl.BlockSpec((tm, D_MODEL), lambda i, be, nv: (i, 0)),
                  pl.BlockSpec((1, D_MODEL, D_EXPERT), lambda i, be, nv: (be[i], 0, 0)),
                  pl.BlockSpec((1, D_MODEL, D_EXPERT), lambda i, be, nv: (be[i], 0, 0)),
                  pl.BlockSpec((1, D_EXPERT, D_MODEL), lambda i, be, nv: (be[i], 0, 0))],
        out_specs=pl.BlockSpec((tm, D_MODEL), lambda i, be, nv: (i, 0)),
        scratch_shapes=[pltpu.VMEM((D_MODEL, D_EXPERT), BF16),
                        pltpu.VMEM((D_MODEL, D_EXPERT), BF16),
                        pltpu.VMEM((D_EXPERT, D_MODEL), BF16)],
    )
    return pl.pallas_call(
        _expert_kernel,
        out_shape=jax.ShapeDtypeStruct((n, D_MODEL), F32),
        grid_spec=grid_spec,
        compiler_params=_cparams(("arbitrary",), 52),
        name="experts",
    )(blk_e, n_valid, xs, w1, w3, w2)


def _combine_kernel(pos_ref, yb_ref, x_ref, wt_ref, g_ref, b_ref, o_ref, ob_ref, buf, sem, *, tm, alpha):
    base = pl.program_id(0) * tm

    def row_copy(k, r, src_row):
        return pltpu.make_async_copy(yb_ref.at[pl.ds(src_row, 1), :], buf.at[k, pl.ds(r, 1), :], sem)

    def start(r, c):
        row_copy(0, r, pos_ref[2 * (base + r)]).start()
        row_copy(1, r, pos_ref[2 * (base + r) + 1]).start()
        return c

    def wait(r, c):
        row_copy(0, r, 0).wait()
        row_copy(1, r, 0).wait()
        return c

    lax.fori_loop(0, tm, start, 0)
    lax.fori_loop(0, tm, wait, 0)
    wt = wt_ref[...]
    h = buf[0] * wt[:, 0:1] + buf[1] * wt[:, 1:2]
    y = _layer_norm_rows(alpha * x_ref[...] + h, g_ref[...], b_ref[...])
    o_ref[...] = y
    ob_ref[...] = y.astype(BF16)


def _combine_ln(pos, yb, x1, wts, g, b, alpha, tm):
    m = x1.shape[0]
    tm = min(tm, m)
    row = lambda i, pos: (i, 0)
    grid_spec = pltpu.PrefetchScalarGridSpec(
        num_scalar_prefetch=1,
        grid=(m // tm,),
        in_specs=[pl.BlockSpec(memory_space=pl.ANY),
                  pl.BlockSpec((tm, D_MODEL), row),
                  pl.BlockSpec((tm, LANES), row),
                  pl.BlockSpec((1, D_MODEL), lambda i, pos: (0, 0)),
                  pl.BlockSpec((1, D_MODEL), lambda i, pos: (0, 0))],
        out_specs=(pl.BlockSpec((tm, D_MODEL), row), pl.BlockSpec((tm, D_MODEL), row)),
        scratch_shapes=[pltpu.VMEM((TOP_K, tm, D_MODEL), F32), pltpu.SemaphoreType.DMA],
    )
    return pl.pallas_call(
        functools.partial(_combine_kernel, tm=tm, alpha=alpha),
        out_shape=(jax.ShapeDtypeStruct((m, D_MODEL), F32), jax.ShapeDtypeStruct((m, D_MODEL), BF16)),
        grid_spec=grid_spec,
        compiler_params=_cparams(("arbitrary",)),
        name="combine_ln",
    )(pos, yb, x1, wts, g, b)


def _dispatch_plan(ids, tm):
    t = ids.shape[0]
    flat_e = ids.reshape(-1)
    n_assign = flat_e.shape[0]
    onehot = (flat_e[:, None] == jnp.arange(N_EXPERTS, dtype=jnp.int32)[None, :]).astype(jnp.int32)
    csum = jnp.cumsum(onehot, axis=0)
    rank = jnp.sum((csum - onehot) * onehot, axis=1)
    counts = csum[-1]
    padded = (counts + tm - 1) // tm * tm
    pend = jnp.cumsum(padded)
    pstart = pend - padded
    dest = jnp.sum(onehot * pstart[None, :], axis=1) + rank
    n_rows = n_assign + N_EXPERTS * tm
    n_blocks = n_rows // tm
    flat_tok = jnp.arange(n_assign, dtype=jnp.int32) // TOP_K
    row_tok = jnp.zeros((n_rows,), jnp.int32).at[dest].set(flat_tok)
    blk_start = jnp.arange(n_blocks, dtype=jnp.int32) * tm
    blk_e = jnp.minimum(jnp.sum((pend[None, :] <= blk_start[:, None]).astype(jnp.int32), axis=1), N_EXPERTS - 1)
    n_valid = (pend[-1] // tm).astype(jnp.int32).reshape(1)
    return row_tok, blk_e.astype(jnp.int32), n_valid, dest.astype(jnp.int32)


def _rope_tables(s):
    half = HEAD_DIM // 2
    inv = ROPE_BASE ** (-jnp.arange(half, dtype=F32) / half)
    ang = jnp.arange(s, dtype=F32)[:, None] * inv[None, :]
    cos, sin = jnp.cos(ang), jnp.sin(ang)
    return jnp.concatenate([cos, cos], axis=1), jnp.concatenate([-sin, sin], axis=1)


def kernel(x, w_in, fox_forget_bias, gla_gate_up, gla_gate_bias, w_branch, w_gate, b_gate, w_out, ln1_g, ln1_b, w_group, b_group, w_expert_router, b_expert_router, w1, w3, w2, ln2_g, ln2_b):
    bsz, s, d = x.shape
    depth = w_in.shape[0]
    alpha = DEEPNORM_ALPHA
    assert bsz == 1 and d == D_MODEL and w_in.shape[2] == IN_WIDTH
    t = bsz * s
    tm_moe = 256

    w_main = jnp.concatenate([w_in[:, :, :O_FF], w_in[:, :, O_FF + N_HEADS:O_GA]], axis=2).astype(BF16)
    w_small = jnp.concatenate(
        [w_in[:, :, O_FF:O_FF + N_HEADS], w_in[:, :, O_GA:],
         jnp.zeros((depth, d, LANES - N_HEADS - GLA_GATE_RANK), F32)], axis=2).astype(BF16)
    wg_b = w_gate.astype(BF16)
    wb_b = w_branch.astype(BF16)
    wo_b = w_out.astype(BF16)
    fbias = jnp.pad(fox_forget_bias, ((0, 0), (0, LANES - N_HEADS)))[:, None, :]
    up_pad = jnp.pad(gla_gate_up, ((0, 0), (S_GA, LANES - S_GA - GLA_GATE_RANK), (0, 0)))
    w_r = jnp.pad(jnp.concatenate([w_group, w_expert_router], axis=2),
                  ((0, 0), (0, 0), (0, LANES - N_GROUPS - N_EXPERTS)))
    b_r = jnp.pad(jnp.concatenate([b_group, b_expert_router], axis=1),
                  ((0, 0), (0, LANES - N_GROUPS - N_EXPERTS)))[:, None, :]
    cos_t, sin_t = _rope_tables(s)

    xf = x.reshape(t, d)
    xb = xf.astype(BF16)
    for l in range(depth):
        pm = _matmul(xb, w_main[l], BF16, 1024, 512)
        ps = _matmul(xb, w_small[l], F32, 1024, LANES)
        fcol = _forget_cumsum(ps, fbias[l], 512)
        frow = jnp.pad(fcol[:, :N_HEADS].T, ((0, 8 - N_HEADS), (0, 0)))
        y_fox = _fox_attention(pm, fcol, frow, 512)
        y_ret = _retention(pm, cos_t, sin_t, 256)
        y_sb = _sb_attention(pm, 256)
        y_gla = _gla(pm, ps, up_pad[l], gla_gate_bias[l][None, :], 256)
        merged = _merge(xb, (y_fox, y_ret, y_sb, y_gla), wg_b[l], b_gate[l][:, None, :], wb_b[l], 512, 512)
        x1, x1b = _out_ln(merged, wo_b[l], xf, ln1_g[l][None, :], ln1_b[l][None, :], alpha, 512)

        ids, wts = _router(x1, w_r[l], b_r[l], 512)
        row_tok, blk_e, n_valid, dest = _dispatch_plan(ids[:, :TOP_K], tm_moe)
        xs = _gather_rows(row_tok, x1, tm_moe)
        yb = _experts(blk_e, n_valid, xs, w1[l], w3[l], w2[l], tm_moe)
        xf, xb = _combine_ln(dest, yb, x1, wts, ln2_g[l][None, :], ln2_b[l][None, :], alpha, 256)
    return xf.reshape(bsz, s, d)
```

```python
import functools
import math

import jax
import jax.numpy as jnp
from jax import lax
from jax.experimental import pallas as pl
from jax.experimental.pallas import tpu as pltpu

F32 = jnp.float32
BF16 = jnp.bfloat16

D_MODEL = 2048
CHUNK = 64
N_HEADS = 4
HEAD_DIM = 128
BRANCH_W = N_HEADS * HEAD_DIM
GLA_DK = 64
GLA_KW = N_HEADS * GLA_DK
GLA_GATE_RANK = 16
GLA_GATE_NORM = 16.0
N_BRANCH = 4
N_GROUPS = 4
EXPERTS_PER_GROUP = 8
N_EXPERTS = N_GROUPS * EXPERTS_PER_GROUP
TOP_K = 2
D_EXPERT = D_MODEL // 4
ROPE_BASE = 10000.0
LN_EPS = 1e-5
DEPTH = 4
DEEPNORM_ALPHA = (2.0 * DEPTH) ** 0.25

LANES = 128
ROW_TILES = D_MODEL // LANES
NEG_BIG = -1e30
SB_EXIT = -104.0

C_FQ, C_FK, C_FV = 0, 512, 1024
C_RQ, C_RK, C_RV, C_RG = 1536, 2048, 2560, 3072
C_SQ, C_SK, C_SV = 3584, 4096, 4608
C_GQ, C_GK, C_GV, C_GR = 5120, 5376, 5632, 6144
MAIN_W = 6656
O_FF, O_GA = 1536, 6660
IN_WIDTH = 6676
S_FF, S_GA = 0, 4


def _cparams(sem, vmem_mb=None):
    kw = dict(dimension_semantics=sem)
    if vmem_mb is not None:
        kw["vmem_limit_bytes"] = vmem_mb * 1024 * 1024
    return pltpu.CompilerParams(**kw)


def _log_sigmoid_parts(z):
    t = jnp.log1p(jnp.exp(-jnp.abs(z)))
    return jnp.minimum(z, 0.0) - t, -(jnp.maximum(z, 0.0) + t)


def _split3(x):
    h1 = x.astype(BF16)
    r1 = x - h1.astype(F32)
    h2 = r1.astype(BF16)
    h3 = (r1 - h2.astype(F32)).astype(BF16)
    return h1, h2, h3


def _dot(a, b):
    return jnp.dot(a, b, preferred_element_type=F32)


def _dot_nt(a, b):
    return lax.dot_general(a, b, (((1,), (1,)), ((), ())), preferred_element_type=F32)


def _dot_tn(a, b):
    return lax.dot_general(a, b, (((0,), (0,)), ((), ())), preferred_element_type=F32)


PROJ_BN = 512
N_ALIGNED_TILES = O_FF // PROJ_BN


def _proj_kernel(x_ref, wa_ref, wb_ref, o_ref, w_scr):
    j = pl.program_id(0)
    first_row_tile = pl.program_id(1) == 0

    @pl.when(jnp.logical_and(first_row_tile, j < N_ALIGNED_TILES))
    def _():
        w_scr[...] = wa_ref[0].astype(BF16)

    @pl.when(jnp.logical_and(first_row_tile, j >= N_ALIGNED_TILES))
    def _():
        ab = jnp.concatenate([wa_ref[0], wb_ref[0]], axis=1)
        width = PROJ_BN + LANES
        w_scr[...] = pltpu.roll(ab, width - N_HEADS, 1)[:, :PROJ_BN].astype(BF16)

    o_ref[...] = _dot(x_ref[...], w_scr[...]).astype(o_ref.dtype)


def _proj_main(xb, w_in, l, bm):
    m = xb.shape[0]
    bm = min(bm, m)
    lanes_per_tile = PROJ_BN // LANES
    return pl.pallas_call(
        _proj_kernel,
        out_shape=jax.ShapeDtypeStruct((m, MAIN_W), BF16),
        grid=(MAIN_W // PROJ_BN, m // bm),
        in_specs=[pl.BlockSpec((bm, D_MODEL), lambda j, i: (i, 0)),
                  pl.BlockSpec((1, D_MODEL, PROJ_BN), lambda j, i: (l, 0, j)),
                  pl.BlockSpec((1, D_MODEL, LANES), lambda j, i: (l, 0, lanes_per_tile * (j + 1)))],
        out_specs=pl.BlockSpec((bm, PROJ_BN), lambda j, i: (i, j)),
        scratch_shapes=[pltpu.VMEM((D_MODEL, PROJ_BN), BF16)],
        compiler_params=_cparams(("arbitrary", "arbitrary"), 48),
        name="proj_main",
    )(xb, w_in, w_in)


def _proj_small_kernel(x_ref, wf_ref, wg_ref, o_ref):
    lane = lax.broadcasted_iota(jnp.int32, (1, LANES), 1)
    w = jnp.where(lane < S_GA, wf_ref[0], jnp.where(lane < S_GA + GLA_GATE_RANK, wg_ref[0], 0.0))
    o_ref[...] = _dot(x_ref[...], w.astype(BF16))


def _proj_small(xb, w_in, l, bm):
    m = xb.shape[0]
    bm = min(bm, m)
    assert O_FF % LANES == 0 and O_GA % LANES == S_GA
    return pl.pallas_call(
        _proj_small_kernel,
        out_shape=jax.ShapeDtypeStruct((m, LANES), F32),
        grid=(m // bm,),
        in_specs=[pl.BlockSpec((bm, D_MODEL), lambda i: (i, 0)),
                  pl.BlockSpec((1, D_MODEL, LANES), lambda i: (l, 0, O_FF // LANES)),
                  pl.BlockSpec((1, D_MODEL, LANES), lambda i: (l, 0, O_GA // LANES))],
        out_specs=pl.BlockSpec((bm, LANES), lambda i: (i, 0)),
        compiler_params=_cparams(("arbitrary",)),
        name="proj_small",
    )(xb, w_in, w_in)


def _fcum_kernel(ps_ref, bias_ref, tri_ref, o_ref, carry_ref):
    @pl.when(pl.program_id(0) == 0)
    def _():
        carry_ref[...] = jnp.zeros_like(carry_ref)

    lf, _ = _log_sigmoid_parts(ps_ref[...] + bias_ref[...])
    tri = tri_ref[...]
    h1, h2, h3 = _split3(lf)
    c = _dot(tri, h1) + _dot(tri, h2) + _dot(tri, h3) + carry_ref[0:1, :]
    o_ref[...] = c
    carry_ref[...] = jnp.broadcast_to(c[-1:, :], carry_ref.shape)


def _forget_cumsum(ps, bias_row, tb):
    s = ps.shape[0]
    tb = min(tb, s)
    r = lax.broadcasted_iota(jnp.int32, (tb, tb), 0)
    c = lax.broadcasted_iota(jnp.int32, (tb, tb), 1)
    tri = (c <= r).astype(BF16)
    return pl.pallas_call(
        _fcum_kernel,
        out_shape=jax.ShapeDtypeStruct((s, LANES), F32),
        grid=(s // tb,),
        in_specs=[pl.BlockSpec((tb, LANES), lambda i: (i, 0)),
                  pl.BlockSpec((1, LANES), lambda i: (0, 0)),
                  pl.BlockSpec((tb, tb), lambda i: (0, 0))],
        out_specs=pl.BlockSpec((tb, LANES), lambda i: (i, 0)),
        scratch_shapes=[pltpu.VMEM((8, LANES), F32)],
        compiler_params=_cparams(("arbitrary",)),
        name="forget_cumsum",
    )(ps, bias_row, tri)


def _fox_kernel(qi_ref, ki_ref, q_ref, k_ref, v_ref, fq_ref, fk_ref, o_ref, m_ref, l_ref, acc_ref, *, t):
    p_id = pl.program_id(0)
    qi = qi_ref[p_id]
    ki = ki_ref[p_id]

    @pl.when(ki == qi)
    def _():
        m_ref[...] = jnp.full_like(m_ref, NEG_BIG)
        l_ref[...] = jnp.zeros_like(l_ref)
        acc_ref[...] = jnp.zeros_like(acc_ref)

    row = qi * t + lax.broadcasted_iota(jnp.int32, (t, t), 0)
    col = ki * t + lax.broadcasted_iota(jnp.int32, (t, t), 1)
    keep = col <= row
    scale = HEAD_DIM ** -0.5
    for h in range(N_HEADS):
        sl = slice(h * HEAD_DIM, (h + 1) * HEAD_DIM)
        s = _dot_nt(q_ref[:, sl], k_ref[:, sl]) * scale
        s = s + (fq_ref[:, h:h + 1] - fk_ref[h:h + 1, :])
        s = jnp.where(keep, s, NEG_BIG)
        m_prev = m_ref[h]
        m_new = jnp.maximum(m_prev, jnp.max(s, axis=-1, keepdims=True))
        alpha = jnp.exp(m_prev - m_new)
        p = jnp.exp(s - m_new)
        l_ref[h] = alpha * l_ref[h] + jnp.sum(p, axis=-1, keepdims=True)
        acc_ref[h] = alpha * acc_ref[h] + _dot(p.astype(BF16), v_ref[:, sl])
        m_ref[h] = m_new

    @pl.when(ki == 0)
    def _():
        for h in range(N_HEADS):
            sl = slice(h * HEAD_DIM, (h + 1) * HEAD_DIM)
            o_ref[:, sl] = (acc_ref[h] / l_ref[h]).astype(o_ref.dtype)


def _fox_attention(pm, fcol, frow, t):
    s = pm.shape[0]
    t = min(t, s)
    nb = s // t
    pairs = [(i, j) for i in range(nb) for j in range(i, -1, -1)]
    qi_tab = jnp.asarray([p[0] for p in pairs], jnp.int32)
    ki_tab = jnp.asarray([p[1] for p in pairs], jnp.int32)
    cq, ck, cv = C_FQ // BRANCH_W, C_FK // BRANCH_W, C_FV // BRANCH_W
    grid_spec = pltpu.PrefetchScalarGridSpec(
        num_scalar_prefetch=2,
        grid=(len(pairs),),
        in_specs=[pl.BlockSpec((t, BRANCH_W), lambda p, qi, ki: (qi[p], cq)),
                  pl.BlockSpec((t, BRANCH_W), lambda p, qi, ki: (ki[p], ck)),
                  pl.BlockSpec((t, BRANCH_W), lambda p, qi, ki: (ki[p], cv)),
                  pl.BlockSpec((t, LANES), lambda p, qi, ki: (qi[p], 0)),
                  pl.BlockSpec((8, t), lambda p, qi, ki: (0, ki[p]))],
        out_specs=pl.BlockSpec((t, BRANCH_W), lambda p, qi, ki: (qi[p], 0)),
        scratch_shapes=[pltpu.VMEM((N_HEADS, t, 1), F32),
                        pltpu.VMEM((N_HEADS, t, 1), F32),
                        pltpu.VMEM((N_HEADS, t, HEAD_DIM), F32)],
    )
    return pl.pallas_call(
        functools.partial(_fox_kernel, t=t),
        out_shape=jax.ShapeDtypeStruct((s, BRANCH_W), BF16),
        grid_spec=grid_spec,
        compiler_params=_cparams(("arbitrary",), 48),
        name="fox_attention",
    )(qi_tab, ki_tab, pm, pm, pm, fcol, frow)


def _sb_kernel(q_ref, k_ref, v_ref, gt_ref, o_ref, r_ref, acc_ref, *, t):
    i = pl.program_id(0)
    r_ref[...] = jnp.zeros_like(r_ref)
    acc_ref[...] = jnp.zeros_like(acc_ref)
    scale = HEAD_DIM ** -0.5
    gt = gt_ref[...]

    def cond(c):
        jj, rmax = c
        return jnp.logical_and(jj <= i, rmax > SB_EXIT)

    def body(c):
        jj, _ = c
        kb = i - jj
        k0 = pl.multiple_of(kb * t, t)
        row = i * t + lax.broadcasted_iota(jnp.int32, (t, t), 0)
        col = kb * t + lax.broadcasted_iota(jnp.int32, (t, t), 1)
        keep = col < row
        rmax = jnp.full((1, 1), -jnp.inf, F32)
        for h in range(N_HEADS):
            sl = slice(h * HEAD_DIM, (h + 1) * HEAD_DIM)
            z = _dot_nt(q_ref[:, sl], k_ref[pl.ds(k0, t), sl]) * scale
            lsg, lom = _log_sigmoid_parts(z)
            lom = jnp.where(keep, lom, 0.0)
            hi = lom.astype(BF16)
            lo = (lom - hi.astype(F32)).astype(BF16)
            r_prev = r_ref[h]
            after = _dot(hi, gt) + _dot(lo, gt) + r_prev
            a = jnp.where(keep, jnp.exp(lsg + after), 0.0)
            acc_ref[h] += _dot(a.astype(BF16), v_ref[pl.ds(k0, t), sl])
            r_new = r_prev + jnp.sum(lom, axis=-1, keepdims=True)
            r_ref[h] = r_new
            rmax = jnp.maximum(rmax, jnp.max(r_new, axis=0, keepdims=True))
        return jj + 1, rmax[0, 0]

    lax.while_loop(cond, body, (jnp.int32(0), jnp.float32(0.0)))
    for h in range(N_HEADS):
        sl = slice(h * HEAD_DIM, (h + 1) * HEAD_DIM)
        o_ref[:, sl] = acc_ref[h].astype(o_ref.dtype)


def _sb_attention(pm, t):
    s = pm.shape[0]
    t = min(t, s)
    r = lax.broadcasted_iota(jnp.int32, (t, t), 0)
    c = lax.broadcasted_iota(jnp.int32, (t, t), 1)
    gt = (r > c).astype(BF16)
    return pl.pallas_call(
        functools.partial(_sb_kernel, t=t),
        out_shape=jax.ShapeDtypeStruct((s, BRANCH_W), BF16),
        grid=(s // t,),
        in_specs=[pl.BlockSpec((t, BRANCH_W), lambda i: (i, C_SQ // BRANCH_W)),
                  pl.BlockSpec((s, BRANCH_W), lambda i: (0, C_SK // BRANCH_W)),
                  pl.BlockSpec((s, BRANCH_W), lambda i: (0, C_SV // BRANCH_W)),
                  pl.BlockSpec((t, t), lambda i: (0, 0))],
        out_specs=pl.BlockSpec((t, BRANCH_W), lambda i: (i, 0)),
        scratch_shapes=[pltpu.VMEM((N_HEADS, t, 1), F32),
                        pltpu.VMEM((N_HEADS, t, HEAD_DIM), F32)],
        compiler_params=_cparams(("arbitrary",), 48),
        name="sb_attention",
    )(pm, pm, pm, gt)


def _ret_kernel(q_ref, k_ref, v_ref, g_ref, cos_ref, sin_ref, o_ref, st_ref, *, tb):
    @pl.when(pl.program_id(0) == 0)
    def _():
        st_ref[...] = jnp.zeros_like(st_ref)

    cos = cos_ref[...]
    sin = sin_ref[...]
    ri = lax.broadcasted_iota(jnp.int32, (tb, tb), 0)
    ci = lax.broadcasted_iota(jnp.int32, (tb, tb), 1)
    chunk_ok = (ci // CHUNK) <= (ri // CHUNK)
    dist = jnp.abs(ri - ci).astype(F32)
    idx = lax.broadcasted_iota(jnp.int32, (tb, 1), 0).astype(F32)
    scale = HEAD_DIM ** -0.5
    for h in range(N_HEADS):
        sl = slice(h * HEAD_DIM, (h + 1) * HEAD_DIM)
        lg = math.log(1.0 - 2.0 ** (-5.0 - h))
        q = q_ref[:, sl].astype(F32)
        k = k_ref[:, sl].astype(F32)
        v = v_ref[:, sl]
        qr = q * cos + pltpu.roll(q, HEAD_DIM // 2, 1) * sin
        kr = (k * cos + pltpu.roll(k, HEAD_DIM // 2, 1) * sin) * scale
        decay = jnp.where(chunk_ok, jnp.exp(lg * dist), 0.0)
        scores = _dot_nt(qr.astype(BF16), kr.astype(BF16)) * decay
        intra = _dot(scores.astype(BF16), v)
        q_dec = jnp.exp(lg * (idx + 1.0))
        k_dec = jnp.exp(lg * (tb - 1.0 - idx))
        state = st_ref[h]
        inter = _dot((qr * q_dec).astype(BF16), state.astype(BF16))
        kv = _dot_tn((kr * k_dec).astype(BF16), v)
        st_ref[h] = state * math.exp(lg * tb) + kv
        o = intra + inter
        mu = jnp.mean(o, axis=-1, keepdims=True)
        d = o - mu
        var = jnp.mean(d * d, axis=-1, keepdims=True)
        on = d * lax.rsqrt(var + LN_EPS)
        g = g_ref[:, sl].astype(F32)
        o_ref[:, sl] = (on * (g * jax.nn.sigmoid(g))).astype(o_ref.dtype)


def _retention(pm, cos_t, sin_t, tb):
    s = pm.shape[0]
    tb = min(tb, s)
    blk = lambda c: pl.BlockSpec((tb, BRANCH_W), lambda i, c=c: (i, c // BRANCH_W))
    return pl.pallas_call(
        functools.partial(_ret_kernel, tb=tb),
        out_shape=jax.ShapeDtypeStruct((s, BRANCH_W), BF16),
        grid=(s // tb,),
        in_specs=[blk(C_RQ), blk(C_RK), blk(C_RV), blk(C_RG),
                  pl.BlockSpec((tb, HEAD_DIM), lambda i: (i, 0)),
                  pl.BlockSpec((tb, HEAD_DIM), lambda i: (i, 0))],
        out_specs=pl.BlockSpec((tb, BRANCH_W), lambda i: (i, 0)),
        scratch_shapes=[pltpu.VMEM((N_HEADS, HEAD_DIM, HEAD_DIM), F32)],
        compiler_params=_cparams(("arbitrary",)),
        name="retention",
    )(pm, pm, pm, pm, cos_t, sin_t)


def _gla_kernel(q_ref, k_ref, v_ref, g_ref, ps_ref, up_ref, gb_ref, bd_ref, o_ref, st_ref, *, tb):
    @pl.when(pl.program_id(0) == 0)
    def _():
        st_ref[...] = jnp.zeros_like(st_ref)

    pre = _dot(ps_ref[...].astype(BF16), up_ref[...].astype(BF16)) + gb_ref[...]
    la, _ = _log_sigmoid_parts(pre)
    la = la / GLA_GATE_NORM
    bd = bd_ref[...]
    h1, h2, h3 = _split3(la)
    b = _dot(bd, h1) + _dot(bd, h2) + _dot(bd, h3)
    eb = jnp.exp(b)
    ieb = jnp.exp(-b)
    scale = GLA_DK ** -0.5
    q = q_ref[...].astype(F32) * scale
    k = k_ref[...].astype(F32)
    qe = q * eb
    qi = q * ieb
    ke = (k * ieb).astype(BF16)
    kf = (k * eb).astype(BF16)
    lane = lax.broadcasted_iota(jnp.int32, (1, GLA_KW), 1)
    ri = lax.broadcasted_iota(jnp.int32, (CHUNK, CHUNK), 0)
    ci = lax.broadcasted_iota(jnp.int32, (CHUNK, CHUNK), 1)
    causal = ri >= ci
    for c in range(tb // CHUNK):
        rs = slice(c * CHUNK, (c + 1) * CHUNK)
        b_c = b[rs]
        b_last = b_c[CHUNK - 1:CHUNK, :]
        kd = k[rs] * jnp.exp(b_last - b_c)
        dec = jnp.exp(b_last)
        for h in range(N_HEADS):
            hm = jnp.logical_and(lane >= h * GLA_DK, lane < (h + 1) * GLA_DK)
            vs = slice(h * HEAD_DIM, (h + 1) * HEAD_DIM)
            qe_h = jnp.where(hm, qe[rs], 0.0).astype(BF16)
            qi_h = jnp.where(hm, qi[rs], 0.0).astype(BF16)
            s_past = _dot_nt(qe_h, ke[rs])
            s_future = _dot_nt(qi_h, kf[rs])
            scores = jnp.where(causal, s_past, s_future)
            v_c = v_ref[rs, vs]
            st = st_ref[h]
            o = _dot(scores.astype(BF16), v_c) + _dot_nt(qe_h, st.astype(BF16))
            kd_h = jnp.where(hm, kd, 0.0).astype(BF16)
            st_ref[h] = st * dec + _dot_tn(v_c, kd_h)
            on = o * lax.rsqrt(jnp.mean(o * o, axis=-1, keepdims=True) + LN_EPS)
            g = g_ref[rs, vs].astype(F32)
            o_ref[rs, vs] = (on * (g * jax.nn.sigmoid(g))).astype(o_ref.dtype)


def _gla(pm, ps, up_pad, gbias, tb):
    s = pm.shape[0]
    tb = min(tb, s)
    r = lax.broadcasted_iota(jnp.int32, (tb, tb), 0)
    c = lax.broadcasted_iota(jnp.int32, (tb, tb), 1)
    bd = jnp.logical_and(c <= r, (c // CHUNK) == (r // CHUNK)).astype(BF16)
    return pl.pallas_call(
        functools.partial(_gla_kernel, tb=tb),
        out_shape=jax.ShapeDtypeStruct((s, BRANCH_W), BF16),
        grid=(s // tb,),
        in_specs=[pl.BlockSpec((tb, GLA_KW), lambda i: (i, C_GQ // GLA_KW)),
                  pl.BlockSpec((tb, GLA_KW), lambda i: (i, C_GK // GLA_KW)),
                  pl.BlockSpec((tb, BRANCH_W), lambda i: (i, C_GV // BRANCH_W)),
                  pl.BlockSpec((tb, BRANCH_W), lambda i: (i, C_GR // BRANCH_W)),
                  pl.BlockSpec((tb, LANES), lambda i: (i, 0)),
                  pl.BlockSpec((LANES, GLA_KW), lambda i: (0, 0)),
                  pl.BlockSpec((1, GLA_KW), lambda i: (0, 0)),
                  pl.BlockSpec((tb, tb), lambda i: (0, 0))],
        out_specs=pl.BlockSpec((tb, BRANCH_W), lambda i: (i, 0)),
        scratch_shapes=[pltpu.VMEM((N_HEADS, HEAD_DIM, GLA_KW), F32)],
        compiler_params=_cparams(("arbitrary",)),
        name="gla",
    )(pm, pm, pm, pm, ps, up_pad, gbias, bd)


def _merge_kernel(x_ref, y0_ref, y1_ref, y2_ref, y3_ref, wg_ref, bg_ref, wb_ref, o_ref):
    x = x_ref[...]
    acc = None
    for n, y_ref in enumerate((y0_ref, y1_ref, y2_ref, y3_ref)):
        gate = jax.nn.sigmoid(_dot(x, wg_ref[n]) + bg_ref[n])
        term = gate * _dot(y_ref[...], wb_ref[n])
        acc = term if acc is None else acc + term
    o_ref[...] = acc.astype(o_ref.dtype)


def _merge(xb, ys, wg, bg, wb, bm, bn):
    m = xb.shape[0]
    bm = min(bm, m)
    yspec = pl.BlockSpec((bm, BRANCH_W), lambda j, i: (i, 0))
    return pl.pallas_call(
        _merge_kernel,
        out_shape=jax.ShapeDtypeStruct((m, D_MODEL), BF16),
        grid=(D_MODEL // bn, m // bm),
        in_specs=[pl.BlockSpec((bm, D_MODEL), lambda j, i: (i, 0)),
                  yspec, yspec, yspec, yspec,
                  pl.BlockSpec((N_BRANCH, D_MODEL, bn), lambda j, i: (0, 0, j)),
                  pl.BlockSpec((N_BRANCH, 1, bn), lambda j, i: (0, 0, j)),
                  pl.BlockSpec((N_BRANCH, BRANCH_W, bn), lambda j, i: (0, 0, j))],
        out_specs=pl.BlockSpec((bm, bn), lambda j, i: (i, j)),
        compiler_params=_cparams(("arbitrary", "arbitrary"), 52),
        name="merge",
    )(xb, *ys, wg, bg, wb)


def _layer_norm_rows(z, g, b):
    mu = jnp.mean(z, axis=-1, keepdims=True)
    d = z - mu
    var = jnp.mean(d * d, axis=-1, keepdims=True)
    return d * lax.rsqrt(var + LN_EPS) * g + b


def _outln_kernel(m_ref, w_ref, x_ref, g_ref, b_ref, o_ref, ob_ref, *, alpha):
    h = _dot(m_ref[...], w_ref[...])
    y = _layer_norm_rows(alpha * x_ref[...] + h, g_ref[...], b_ref[...])
    o_ref[...] = y
    ob_ref[...] = y.astype(BF16).reshape(ob_ref.shape)


def _out_ln(merged, w_out, x, g, b, alpha, bm):
    m = x.shape[0]
    bm = min(bm, m)
    row = pl.BlockSpec((bm, D_MODEL), lambda i: (i, 0))
    row3 = pl.BlockSpec((bm, ROW_TILES, LANES), lambda i: (i, 0, 0))
    vec = pl.BlockSpec((1, D_MODEL), lambda i: (0, 0))
    return pl.pallas_call(
        functools.partial(_outln_kernel, alpha=alpha),
        out_shape=(jax.ShapeDtypeStruct((m, D_MODEL), F32),
                   jax.ShapeDtypeStruct((m, ROW_TILES, LANES), BF16)),
        grid=(m // bm,),
        in_specs=[row, pl.BlockSpec((D_MODEL, D_MODEL), lambda i: (0, 0)), row, vec, vec],
        out_specs=(row, row3),
        compiler_params=_cparams(("arbitrary",), 48),
        name="out_ln",
    )(merged, w_out, x, g, b)


def _router_kernel(x_ref, w_ref, b_ref, id_ref, wt_ref):
    x = x_ref[...]
    xh = x.astype(BF16)
    xl = (x - xh.astype(F32)).astype(BF16)
    w = w_ref[...]
    wh = w.astype(BF16)
    wl = (w - wh.astype(F32)).astype(BF16)
    logits = _dot(xh, wh) + _dot(xh, wl) + _dot(xl, wh) + b_ref[...]
    lane = lax.broadcasted_iota(jnp.int32, logits.shape, 1)
    neg = -jnp.inf
    gl = jnp.where(lane < N_GROUPS, logits, neg)
    gmax = jnp.max(gl, axis=-1, keepdims=True)
    g_sel = jnp.min(jnp.where(gl == gmax, lane, LANES), axis=-1, keepdims=True)
    g_w = 1.0 / jnp.sum(jnp.where(lane < N_GROUPS, jnp.exp(logits - gmax), 0.0), axis=-1, keepdims=True)
    lo = N_GROUPS + g_sel * EXPERTS_PER_GROUP
    el = jnp.where(jnp.logical_and(lane >= lo, lane < lo + EXPERTS_PER_GROUP), logits, neg)
    v1 = jnp.max(el, axis=-1, keepdims=True)
    i1 = jnp.min(jnp.where(el == v1, lane, LANES), axis=-1, keepdims=True)
    el2 = jnp.where(lane == i1, neg, el)
    v2 = jnp.max(el2, axis=-1, keepdims=True)
    i2 = jnp.min(jnp.where(el2 == v2, lane, LANES), axis=-1, keepdims=True)
    e2 = jnp.exp(v2 - v1)
    p1 = 1.0 / (1.0 + e2)
    p2 = e2 / (1.0 + e2)
    id_ref[...] = jnp.where(lane == 0, i1 - N_GROUPS, jnp.where(lane == 1, i2 - N_GROUPS, 0))
    wt_ref[...] = jnp.where(lane == 0, p1 * g_w, jnp.where(lane == 1, p2 * g_w, 0.0))


def _router(x1, w_r, b_r, bm):
    m = x1.shape[0]
    bm = min(bm, m)
    return pl.pallas_call(
        _router_kernel,
        out_shape=(jax.ShapeDtypeStruct((m, LANES), jnp.int32), jax.ShapeDtypeStruct((m, LANES), F32)),
        grid=(m // bm,),
        in_specs=[pl.BlockSpec((bm, D_MODEL), lambda i: (i, 0)),
                  pl.BlockSpec((D_MODEL, LANES), lambda i: (0, 0)),
                  pl.BlockSpec((1, LANES), lambda i: (0, 0))],
        out_specs=(pl.BlockSpec((bm, LANES), lambda i: (i, 0)), pl.BlockSpec((bm, LANES), lambda i: (i, 0))),
        compiler_params=_cparams(("arbitrary",)),
        name="router",
    )(x1, w_r, b_r)


def _start_row_gather(idx_ref, first, stride, n, src_ref, dst_ref, sem):
    def start(r, c):
        pltpu.make_async_copy(src_ref.at[idx_ref[first + stride * r]], dst_ref.at[r], sem).start()
        return c
    lax.fori_loop(0, n, start, 0, unroll=8)


def _wait_row_gather(n, src_ref, dst_ref, sem):
    pltpu.make_async_copy(src_ref.at[pl.ds(0, n)], dst_ref, sem).wait()


def _expert_kernel(tok_ref, be_ref, nv_ref, x_ref, w1_ref, w3_ref, w2_ref, o_ref,
                   xbuf, sem, w1b, w3b, w2b, *, tm):
    i = pl.program_id(0)
    n_valid = nv_ref[0]
    e = be_ref[i]
    prev = be_ref[jnp.maximum(i - 1, 0)]
    slot = lax.rem(i, 2)

    @pl.when(i == 0)
    def _():
        _start_row_gather(tok_ref, 0, 1, tm, x_ref, xbuf.at[0], sem.at[0])

    @pl.when(i + 1 < n_valid)
    def _():
        _start_row_gather(tok_ref, (i + 1) * tm, 1, tm, x_ref, xbuf.at[1 - slot], sem.at[1 - slot])

    @pl.when(jnp.logical_or(i == 0, e != prev))
    def _():
        w1b[...] = w1_ref[0].astype(BF16)
        w3b[...] = w3_ref[0].astype(BF16)
        w2b[...] = w2_ref[0].astype(BF16)

    @pl.when(i < n_valid)
    def _():
        _wait_row_gather(tm, x_ref, xbuf.at[slot], sem.at[slot])
        x = xbuf[slot].reshape(tm, D_MODEL)
        a = _dot(x, w1b[...])
        h = (a * jax.nn.sigmoid(a)) * _dot(x, w3b[...])
        y = _dot(h.astype(BF16), w2b[...])
        o_ref[...] = y.astype(BF16).reshape(o_ref.shape)

    @pl.when(i >= n_valid)
    def _():
        o_ref[...] = jnp.zeros_like(o_ref)


def _experts(row_tok, blk_e, n_valid, x3, w1, w3, w2, tm):
    n = row_tok.shape[0]
    grid_spec = pltpu.PrefetchScalarGridSpec(
        num_scalar_prefetch=3,
        grid=(n // tm,),
        in_specs=[pl.BlockSpec(memory_space=pl.ANY),
                  pl.BlockSpec((1, D_MODEL, D_EXPERT), lambda i, tok, be, nv: (be[i], 0, 0)),
                  pl.BlockSpec((1, D_MODEL, D_EXPERT), lambda i, tok, be, nv: (be[i], 0, 0)),
                  pl.BlockSpec((1, D_EXPERT, D_MODEL), lambda i, tok, be, nv: (be[i], 0, 0))],
        out_specs=pl.BlockSpec((tm, ROW_TILES, LANES), lambda i, tok, be, nv: (i, 0, 0)),
        scratch_shapes=[pltpu.VMEM((2, tm, ROW_TILES, LANES), BF16),
                        pltpu.SemaphoreType.DMA((2,)),
                        pltpu.VMEM((D_MODEL, D_EXPERT), BF16),
                        pltpu.VMEM((D_MODEL, D_EXPERT), BF16),
                        pltpu.VMEM((D_EXPERT, D_MODEL), BF16)],
    )
    return pl.pallas_call(
        functools.partial(_expert_kernel, tm=tm),
        out_shape=jax.ShapeDtypeStruct((n, ROW_TILES, LANES), BF16),
        grid_spec=grid_spec,
        compiler_params=_cparams(("arbitrary",), 52),
        name="experts",
    )(row_tok, blk_e, n_valid, x3, w1, w3, w2)


def _combine_kernel(pos_ref, yb_ref, x_ref, wt_ref, g_ref, b_ref, o_ref, ob_ref, buf, sem, *, tm, alpha):
    i = pl.program_id(0)
    slot = lax.rem(i, 2)

    def start_tile(tile, s):
        for k in range(TOP_K):
            _start_row_gather(pos_ref, TOP_K * tile * tm + k, TOP_K, tm, yb_ref, buf.at[s, k], sem.at[s])

    @pl.when(i == 0)
    def _():
        start_tile(0, 0)

    @pl.when(i + 1 < pl.num_programs(0))
    def _():
        start_tile(i + 1, 1 - slot)

    for k in range(TOP_K):
        _wait_row_gather(tm, yb_ref, buf.at[slot, k], sem.at[slot])
    wt = wt_ref[...]
    h = None
    for k in range(TOP_K):
        term = buf[slot, k].reshape(tm, D_MODEL).astype(F32) * wt[:, k:k + 1]
        h = term if h is None else h + term
    y = _layer_norm_rows(alpha * x_ref[...] + h, g_ref[...], b_ref[...])
    o_ref[...] = y
    ob_ref[...] = y.astype(BF16)


def _combine_ln(pos, yb, x1, wts, g, b, alpha, tm):
    m = x1.shape[0]
    tm = min(tm, m)
    row = lambda i, pos: (i, 0)
    grid_spec = pltpu.PrefetchScalarGridSpec(
        num_scalar_prefetch=1,
        grid=(m // tm,),
        in_specs=[pl.BlockSpec(memory_space=pl.ANY),
                  pl.BlockSpec((tm, D_MODEL), row),
                  pl.BlockSpec((tm, LANES), row),
                  pl.BlockSpec((1, D_MODEL), lambda i, pos: (0, 0)),
                  pl.BlockSpec((1, D_MODEL), lambda i, pos: (0, 0))],
        out_specs=(pl.BlockSpec((tm, D_MODEL), row), pl.BlockSpec((tm, D_MODEL), row)),
        scratch_shapes=[pltpu.VMEM((2, TOP_K, tm, ROW_TILES, LANES), BF16), pltpu.SemaphoreType.DMA((2,))],
    )
    return pl.pallas_call(
        functools.partial(_combine_kernel, tm=tm, alpha=alpha),
        out_shape=(jax.ShapeDtypeStruct((m, D_MODEL), F32), jax.ShapeDtypeStruct((m, D_MODEL), BF16)),
        grid_spec=grid_spec,
        compiler_params=_cparams(("arbitrary",)),
        name="combine_ln",
    )(pos, yb, x1, wts, g, b)


def _dispatch_plan(ids, tm):
    t = ids.shape[0]
    flat_e = ids.reshape(-1)
    n_assign = flat_e.shape[0]
    onehot = (flat_e[:, None] == jnp.arange(N_EXPERTS, dtype=jnp.int32)[None, :]).astype(jnp.int32)
    csum = jnp.cumsum(onehot, axis=0)
    rank = jnp.sum((csum - onehot) * onehot, axis=1)
    counts = csum[-1]
    padded = (counts + tm - 1) // tm * tm
    pend = jnp.cumsum(padded)
    pstart = pend - padded
    dest = jnp.sum(onehot * pstart[None, :], axis=1) + rank
    n_rows = n_assign + N_EXPERTS * tm
    n_blocks = n_rows // tm
    flat_tok = jnp.arange(n_assign, dtype=jnp.int32) // TOP_K
    row_tok = jnp.zeros((n_rows,), jnp.int32).at[dest].set(flat_tok)
    blk_start = jnp.arange(n_blocks, dtype=jnp.int32) * tm
    blk_e = jnp.minimum(jnp.sum((pend[None, :] <= blk_start[:, None]).astype(jnp.int32), axis=1), N_EXPERTS - 1)
    n_valid = (pend[-1] // tm).astype(jnp.int32).reshape(1)
    return row_tok, blk_e.astype(jnp.int32), n_valid, dest.astype(jnp.int32)


def _rope_tables(s):
    half = HEAD_DIM // 2
    inv = ROPE_BASE ** (-jnp.arange(half, dtype=F32) / half)
    ang = jnp.arange(s, dtype=F32)[:, None] * inv[None, :]
    cos, sin = jnp.cos(ang), jnp.sin(ang)
    return jnp.concatenate([cos, cos], axis=1), jnp.concatenate([-sin, sin], axis=1)


def kernel(x, w_in, fox_forget_bias, gla_gate_up, gla_gate_bias, w_branch, w_gate, b_gate, w_out, ln1_g, ln1_b, w_group, b_group, w_expert_router, b_expert_router, w1, w3, w2, ln2_g, ln2_b):
    bsz, s, d = x.shape
    depth = w_in.shape[0]
    alpha = DEEPNORM_ALPHA
    assert bsz == 1 and d == D_MODEL and w_in.shape[2] == IN_WIDTH
    t = bsz * s
    tm_moe = 256

    wg_b = w_gate.astype(BF16)
    wb_b = w_branch.astype(BF16)
    wo_b = w_out.astype(BF16)
    fbias = jnp.pad(fox_forget_bias, ((0, 0), (0, LANES - N_HEADS)))[:, None, :]
    up_pad = jnp.pad(gla_gate_up, ((0, 0), (S_GA, LANES - S_GA - GLA_GATE_RANK), (0, 0)))
    w_r = jnp.pad(jnp.concatenate([w_group, w_expert_router], axis=2),
                  ((0, 0), (0, 0), (0, LANES - N_GROUPS - N_EXPERTS)))
    b_r = jnp.pad(jnp.concatenate([b_group, b_expert_router], axis=1),
                  ((0, 0), (0, LANES - N_GROUPS - N_EXPERTS)))[:, None, :]
    cos_t, sin_t = _rope_tables(s)

    xf = x.reshape(t, d)
    xb = xf.astype(BF16)
    for l in range(depth):
        pm = _proj_main(xb, w_in, l, 1024)
        ps = _proj_small(xb, w_in, l, 1024)
        fcol = _forget_cumsum(ps, fbias[l], 512)
        frow = jnp.pad(fcol[:, :N_HEADS].T, ((0, 8 - N_HEADS), (0, 0)))
        y_fox = _fox_attention(pm, fcol, frow, 512)
        y_ret = _retention(pm, cos_t, sin_t, 256)
        y_sb = _sb_attention(pm, 256)
        y_gla = _gla(pm, ps, up_pad[l], gla_gate_bias[l][None, :], 256)
        merged = _merge(xb, (y_fox, y_ret, y_sb, y_gla), wg_b[l], b_gate[l][:, None, :], wb_b[l], 512, 512)
        x1, x1b3 = _out_ln(merged, wo_b[l], xf, ln1_g[l][None, :], ln1_b[l][None, :], alpha, 512)

        ids, wts = _router(x1, w_r[l], b_r[l], 512)
        row_tok, blk_e, n_valid, dest = _dispatch_plan(ids[:, :TOP_K], tm_moe)
        yb3 = _experts(row_tok, blk_e, n_valid, x1b3, w1[l], w3[l], w2[l], tm_moe)
        xf, xb = _combine_ln(dest, yb3, x1, wts, ln2_g[l][None, :], ln2_b[l][None, :], alpha, 256)
    return xf.reshape(bsz, s, d)
```

```python
import functools
import math

import jax
import jax.numpy as jnp
from jax import lax
from jax.experimental import pallas as pl
from jax.experimental.pallas import tpu as pltpu

F32 = jnp.float32
BF16 = jnp.bfloat16

D_MODEL = 2048
CHUNK = 64
N_HEADS = 4
HEAD_DIM = 128
BRANCH_W = N_HEADS * HEAD_DIM
GLA_DK = 64
GLA_KW = N_HEADS * GLA_DK
GLA_GATE_RANK = 16
GLA_GATE_NORM = 16.0
N_BRANCH = 4
N_GROUPS = 4
EXPERTS_PER_GROUP = 8
N_EXPERTS = N_GROUPS * EXPERTS_PER_GROUP
TOP_K = 2
D_EXPERT = D_MODEL // 4
ROPE_BASE = 10000.0
LN_EPS = 1e-5
DEPTH = 4
DEEPNORM_ALPHA = (2.0 * DEPTH) ** 0.25

LANES = 128
ROW_TILES = D_MODEL // LANES
NEG_BIG = -1e30
SB_EXIT = -104.0

C_FQ, C_FK, C_FV = 0, 512, 1024
C_RQ, C_RK, C_RV, C_RG = 1536, 2048, 2560, 3072
C_SQ, C_SK, C_SV = 3584, 4096, 4608
C_GQ, C_GK, C_GV, C_GR = 5120, 5376, 5632, 6144
MAIN_W = 6656
O_FF, O_GA = 1536, 6660
IN_WIDTH = 6676
S_FF, S_GA = 0, 4


def _cparams(sem, vmem_mb=None):
    kw = dict(dimension_semantics=sem)
    if vmem_mb is not None:
        kw["vmem_limit_bytes"] = vmem_mb * 1024 * 1024
    return pltpu.CompilerParams(**kw)


def _log_sigmoid_parts(z):
    t = jnp.log1p(jnp.exp(-jnp.abs(z)))
    return jnp.minimum(z, 0.0) - t, -(jnp.maximum(z, 0.0) + t)


def _split3(x):
    h1 = x.astype(BF16)
    r1 = x - h1.astype(F32)
    h2 = r1.astype(BF16)
    h3 = (r1 - h2.astype(F32)).astype(BF16)
    return h1, h2, h3


def _dot(a, b):
    return jnp.dot(a, b, preferred_element_type=F32)


def _dot_nt(a, b):
    return lax.dot_general(a, b, (((1,), (1,)), ((), ())), preferred_element_type=F32)


def _dot_tn(a, b):
    return lax.dot_general(a, b, (((0,), (0,)), ((), ())), preferred_element_type=F32)


PROJ_BN = 512
N_ALIGNED_TILES = O_FF // PROJ_BN


def _proj_kernel(x_ref, wa_ref, wb_ref, o_ref, w_scr):
    j = pl.program_id(0)
    first_row_tile = pl.program_id(1) == 0

    @pl.when(jnp.logical_and(first_row_tile, j < N_ALIGNED_TILES))
    def _():
        w_scr[...] = wa_ref[0].astype(BF16)

    @pl.when(jnp.logical_and(first_row_tile, j >= N_ALIGNED_TILES))
    def _():
        ab = jnp.concatenate([wa_ref[0], wb_ref[0]], axis=1)
        width = PROJ_BN + LANES
        w_scr[...] = pltpu.roll(ab, width - N_HEADS, 1)[:, :PROJ_BN].astype(BF16)

    o_ref[...] = _dot(x_ref[...], w_scr[...]).astype(o_ref.dtype)


def _proj_main(xb, w_in, l, bm):
    m = xb.shape[0]
    bm = min(bm, m)
    lanes_per_tile = PROJ_BN // LANES
    return pl.pallas_call(
        _proj_kernel,
        out_shape=jax.ShapeDtypeStruct((m, MAIN_W), BF16),
        grid=(MAIN_W // PROJ_BN, m // bm),
        in_specs=[pl.BlockSpec((bm, D_MODEL), lambda j, i: (i, 0)),
                  pl.BlockSpec((1, D_MODEL, PROJ_BN), lambda j, i: (l, 0, j)),
                  pl.BlockSpec((1, D_MODEL, LANES), lambda j, i: (l, 0, lanes_per_tile * (j + 1)))],
        out_specs=pl.BlockSpec((bm, PROJ_BN), lambda j, i: (i, j)),
        scratch_shapes=[pltpu.VMEM((D_MODEL, PROJ_BN), BF16)],
        compiler_params=_cparams(("arbitrary", "arbitrary"), 48),
        name="proj_main",
    )(xb, w_in, w_in)


def _proj_small_kernel(x_ref, wf_ref, wg_ref, o_ref):
    lane = lax.broadcasted_iota(jnp.int32, (1, LANES), 1)
    w = jnp.where(lane < S_GA, wf_ref[0], jnp.where(lane < S_GA + GLA_GATE_RANK, wg_ref[0], 0.0))
    o_ref[...] = _dot(x_ref[...], w.astype(BF16))


def _proj_small(xb, w_in, l, bm):
    m = xb.shape[0]
    bm = min(bm, m)
    assert O_FF % LANES == 0 and O_GA % LANES == S_GA
    return pl.pallas_call(
        _proj_small_kernel,
        out_shape=jax.ShapeDtypeStruct((m, LANES), F32),
        grid=(m // bm,),
        in_specs=[pl.BlockSpec((bm, D_MODEL), lambda i: (i, 0)),
                  pl.BlockSpec((1, D_MODEL, LANES), lambda i: (l, 0, O_FF // LANES)),
                  pl.BlockSpec((1, D_MODEL, LANES), lambda i: (l, 0, O_GA // LANES))],
        out_specs=pl.BlockSpec((bm, LANES), lambda i: (i, 0)),
        compiler_params=_cparams(("arbitrary",)),
        name="proj_small",
    )(xb, w_in, w_in)


def _fcum_kernel(ps_ref, bias_ref, tri_ref, o_ref, carry_ref):
    @pl.when(pl.program_id(0) == 0)
    def _():
        carry_ref[...] = jnp.zeros_like(carry_ref)

    lf, _ = _log_sigmoid_parts(ps_ref[...] + bias_ref[...])
    tri = tri_ref[...]
    h1, h2, h3 = _split3(lf)
    c = _dot(tri, h1) + _dot(tri, h2) + _dot(tri, h3) + carry_ref[0:1, :]
    o_ref[...] = c
    carry_ref[...] = jnp.broadcast_to(c[-1:, :], carry_ref.shape)


def _forget_cumsum(ps, bias_row, tb):
    s = ps.shape[0]
    tb = min(tb, s)
    r = lax.broadcasted_iota(jnp.int32, (tb, tb), 0)
    c = lax.broadcasted_iota(jnp.int32, (tb, tb), 1)
    tri = (c <= r).astype(BF16)
    return pl.pallas_call(
        _fcum_kernel,
        out_shape=jax.ShapeDtypeStruct((s, LANES), F32),
        grid=(s // tb,),
        in_specs=[pl.BlockSpec((tb, LANES), lambda i: (i, 0)),
                  pl.BlockSpec((1, LANES), lambda i: (0, 0)),
                  pl.BlockSpec((tb, tb), lambda i: (0, 0))],
        out_specs=pl.BlockSpec((tb, LANES), lambda i: (i, 0)),
        scratch_shapes=[pltpu.VMEM((8, LANES), F32)],
        compiler_params=_cparams(("arbitrary",)),
        name="forget_cumsum",
    )(ps, bias_row, tri)


FOX_AUG = 2 * HEAD_DIM
LOG2E = 1.4426950408889634


def _fox_prep_kernel(q_ref, k_ref, v_ref, f_ref, qa_ref, ka_ref, va_ref):
    tb = q_ref.shape[0]
    lane = lax.broadcasted_iota(jnp.int32, (tb, HEAD_DIM), 1)
    inv_scale = HEAD_DIM ** 0.5
    for h in range(N_HEADS):
        sl = slice(h * HEAD_DIM, (h + 1) * HEAD_DIM)
        f = f_ref[:, h:h + 1] * inv_scale
        h1, h2, h3 = (p.astype(F32) for p in _split3(f))
        aq = jnp.where(lane == 0, h1, jnp.where(lane == 1, h2, jnp.where(lane == 2, h3,
                       jnp.where(lane < 6, 1.0, 0.0)))).astype(BF16)
        ak = jnp.where(lane < 3, 1.0, jnp.where(lane == 3, -h1, jnp.where(lane == 4, -h2,
                       jnp.where(lane == 5, -h3, 0.0)))).astype(BF16)
        qa_ref[:, h * FOX_AUG:h * FOX_AUG + HEAD_DIM] = q_ref[:, sl]
        qa_ref[:, h * FOX_AUG + HEAD_DIM:(h + 1) * FOX_AUG] = aq
        ka_ref[:, h * FOX_AUG:h * FOX_AUG + HEAD_DIM] = k_ref[:, sl]
        ka_ref[:, h * FOX_AUG + HEAD_DIM:(h + 1) * FOX_AUG] = ak
        va_ref[:, h * FOX_AUG:h * FOX_AUG + HEAD_DIM] = v_ref[:, sl]
        va_ref[:, h * FOX_AUG + HEAD_DIM:(h + 1) * FOX_AUG] = jnp.where(lane == 0, 1.0, 0.0).astype(BF16)


def _fox_prep(pm, fcol, tb):
    s = pm.shape[0]
    tb = min(tb, s)
    blk = lambda c: pl.BlockSpec((tb, BRANCH_W), lambda i, c=c: (i, c // BRANCH_W))
    out = pl.BlockSpec((tb, N_HEADS * FOX_AUG), lambda i: (i, 0))
    shp = jax.ShapeDtypeStruct((s, N_HEADS * FOX_AUG), BF16)
    return pl.pallas_call(
        _fox_prep_kernel,
        out_shape=(shp, shp, shp),
        grid=(s // tb,),
        in_specs=[blk(C_FQ), blk(C_FK), blk(C_FV), pl.BlockSpec((tb, LANES), lambda i: (i, 0))],
        out_specs=(out, out, out),
        compiler_params=_cparams(("arbitrary",)),
        name="fox_prep",
    )(pm, pm, pm, fcol)


def _fox_kernel(qi_ref, ki_ref, q_ref, k_ref, v_ref, o_ref, m_ref, acc_ref, *, t):
    p_id = pl.program_id(0)
    qi = qi_ref[p_id]
    ki = ki_ref[p_id]
    c1 = HEAD_DIM ** -0.5 * LOG2E

    def step(masked):
        if masked:
            row = lax.broadcasted_iota(jnp.int32, (t, t), 0)
            col = lax.broadcasted_iota(jnp.int32, (t, t), 1)
            keep = col <= row
        for h in range(N_HEADS):
            sl = slice(h * FOX_AUG, (h + 1) * FOX_AUG)
            s = _dot_nt(q_ref[:, sl], k_ref[:, sl]) * c1
            if masked:
                s = jnp.where(keep, s, NEG_BIG)
            m_prev = m_ref[h]
            m_new = jnp.maximum(m_prev, jnp.max(s, axis=-1, keepdims=True))
            alpha = jnp.exp2(m_prev - m_new)
            p = jnp.exp2(s - m_new)
            acc_ref[h] = alpha * acc_ref[h] + _dot(p.astype(BF16), v_ref[:, sl])
            m_ref[h] = m_new

    @pl.when(ki == qi)
    def _():
        m_ref[...] = jnp.full_like(m_ref, NEG_BIG)
        acc_ref[...] = jnp.zeros_like(acc_ref)
        step(True)

    @pl.when(ki != qi)
    def _():
        step(False)

    @pl.when(ki == 0)
    def _():
        for h in range(N_HEADS):
            acc = acc_ref[h]
            o_ref[:, h * HEAD_DIM:(h + 1) * HEAD_DIM] = (
                acc[:, :HEAD_DIM] / acc[:, HEAD_DIM:HEAD_DIM + 1]).astype(o_ref.dtype)


def _fox_attention(pm, fcol, t):
    s = pm.shape[0]
    t = min(t, s)
    nb = s // t
    qa, ka, va = _fox_prep(pm, fcol, t)
    pairs = [(i, j) for i in range(nb) for j in range(i, -1, -1)]
    qi_tab = jnp.asarray([p[0] for p in pairs], jnp.int32)
    ki_tab = jnp.asarray([p[1] for p in pairs], jnp.int32)
    w = N_HEADS * FOX_AUG
    grid_spec = pltpu.PrefetchScalarGridSpec(
        num_scalar_prefetch=2,
        grid=(len(pairs),),
        in_specs=[pl.BlockSpec((t, w), lambda p, qi, ki: (qi[p], 0)),
                  pl.BlockSpec((t, w), lambda p, qi, ki: (ki[p], 0)),
                  pl.BlockSpec((t, w), lambda p, qi, ki: (ki[p], 0))],
        out_specs=pl.BlockSpec((t, BRANCH_W), lambda p, qi, ki: (qi[p], 0)),
        scratch_shapes=[pltpu.VMEM((N_HEADS, t, 1), F32),
                        pltpu.VMEM((N_HEADS, t, FOX_AUG), F32)],
    )
    return pl.pallas_call(
        functools.partial(_fox_kernel, t=t),
        out_shape=jax.ShapeDtypeStruct((s, BRANCH_W), BF16),
        grid_spec=grid_spec,
        compiler_params=_cparams(("arbitrary",), 48),
        name="fox_attention",
    )(qi_tab, ki_tab, qa, ka, va)


def _sb_kernel(q_ref, k_ref, v_ref, gt_ref, o_ref, r_ref, acc_ref, *, t):
    i = pl.program_id(0)
    r_ref[...] = jnp.zeros_like(r_ref)
    acc_ref[...] = jnp.zeros_like(acc_ref)
    scale = HEAD_DIM ** -0.5
    gt = gt_ref[...]

    def cond(c):
        jj, rmax = c
        return jnp.logical_and(jj <= i, rmax > SB_EXIT)

    def body(c):
        jj, _ = c
        kb = i - jj
        k0 = pl.multiple_of(kb * t, t)
        row = i * t + lax.broadcasted_iota(jnp.int32, (t, t), 0)
        col = kb * t + lax.broadcasted_iota(jnp.int32, (t, t), 1)
        keep = col < row
        rmax = jnp.full((1, 1), -jnp.inf, F32)
        for h in range(N_HEADS):
            sl = slice(h * HEAD_DIM, (h + 1) * HEAD_DIM)
            z = _dot_nt(q_ref[:, sl], k_ref[pl.ds(k0, t), sl]) * scale
            lsg, lom = _log_sigmoid_parts(z)
            lom = jnp.where(keep, lom, 0.0)
            hi = lom.astype(BF16)
            lo = (lom - hi.astype(F32)).astype(BF16)
            r_prev = r_ref[h]
            after = _dot(hi, gt) + _dot(lo, gt) + r_prev
            a = jnp.where(keep, jnp.exp(lsg + after), 0.0)
            acc_ref[h] += _dot(a.astype(BF16), v_ref[pl.ds(k0, t), sl])
            r_new = r_prev + jnp.sum(lom, axis=-1, keepdims=True)
            r_ref[h] = r_new
            rmax = jnp.maximum(rmax, jnp.max(r_new, axis=0, keepdims=True))
        return jj + 1, rmax[0, 0]

    lax.while_loop(cond, body, (jnp.int32(0), jnp.float32(0.0)))
    for h in range(N_HEADS):
        sl = slice(h * HEAD_DIM, (h + 1) * HEAD_DIM)
        o_ref[:, sl] = acc_ref[h].astype(o_ref.dtype)


def _sb_attention(pm, t):
    s = pm.shape[0]
    t = min(t, s)
    r = lax.broadcasted_iota(jnp.int32, (t, t), 0)
    c = lax.broadcasted_iota(jnp.int32, (t, t), 1)
    gt = (r > c).astype(BF16)
    return pl.pallas_call(
        functools.partial(_sb_kernel, t=t),
        out_shape=jax.ShapeDtypeStruct((s, BRANCH_W), BF16),
        grid=(s // t,),
        in_specs=[pl.BlockSpec((t, BRANCH_W), lambda i: (i, C_SQ // BRANCH_W)),
                  pl.BlockSpec((s, BRANCH_W), lambda i: (0, C_SK // BRANCH_W)),
                  pl.BlockSpec((s, BRANCH_W), lambda i: (0, C_SV // BRANCH_W)),
                  pl.BlockSpec((t, t), lambda i: (0, 0))],
        out_specs=pl.BlockSpec((t, BRANCH_W), lambda i: (i, 0)),
        scratch_shapes=[pltpu.VMEM((N_HEADS, t, 1), F32),
                        pltpu.VMEM((N_HEADS, t, HEAD_DIM), F32)],
        compiler_params=_cparams(("arbitrary",), 48),
        name="sb_attention",
    )(pm, pm, pm, gt)


def _ret_kernel(q_ref, k_ref, v_ref, g_ref, cos_ref, sin_ref, o_ref, st_ref, *, tb):
    @pl.when(pl.program_id(0) == 0)
    def _():
        st_ref[...] = jnp.zeros_like(st_ref)

    cos = cos_ref[...]
    sin = sin_ref[...]
    ri = lax.broadcasted_iota(jnp.int32, (tb, tb), 0)
    ci = lax.broadcasted_iota(jnp.int32, (tb, tb), 1)
    chunk_ok = (ci // CHUNK) <= (ri // CHUNK)
    dist = jnp.abs(ri - ci).astype(F32)
    idx = lax.broadcasted_iota(jnp.int32, (tb, 1), 0).astype(F32)
    scale = HEAD_DIM ** -0.5
    for h in range(N_HEADS):
        sl = slice(h * HEAD_DIM, (h + 1) * HEAD_DIM)
        lg = math.log(1.0 - 2.0 ** (-5.0 - h))
        q = q_ref[:, sl].astype(F32)
        k = k_ref[:, sl].astype(F32)
        v = v_ref[:, sl]
        qr = q * cos + pltpu.roll(q, HEAD_DIM // 2, 1) * sin
        kr = (k * cos + pltpu.roll(k, HEAD_DIM // 2, 1) * sin) * scale
        decay = jnp.where(chunk_ok, jnp.exp(lg * dist), 0.0)
        scores = _dot_nt(qr.astype(BF16), kr.astype(BF16)) * decay
        intra = _dot(scores.astype(BF16), v)
        q_dec = jnp.exp(lg * (idx + 1.0))
        k_dec = jnp.exp(lg * (tb - 1.0 - idx))
        state = st_ref[h]
        inter = _dot((qr * q_dec).astype(BF16), state.astype(BF16))
        kv = _dot_tn((kr * k_dec).astype(BF16), v)
        st_ref[h] = state * math.exp(lg * tb) + kv
        o = intra + inter
        mu = jnp.mean(o, axis=-1, keepdims=True)
        d = o - mu
        var = jnp.mean(d * d, axis=-1, keepdims=True)
        on = d * lax.rsqrt(var + LN_EPS)
        g = g_ref[:, sl].astype(F32)
        o_ref[:, sl] = (on * (g * jax.nn.sigmoid(g))).astype(o_ref.dtype)


def _retention(pm, cos_t, sin_t, tb):
    s = pm.shape[0]
    tb = min(tb, s)
    blk = lambda c: pl.BlockSpec((tb, BRANCH_W), lambda i, c=c: (i, c // BRANCH_W))
    return pl.pallas_call(
        functools.partial(_ret_kernel, tb=tb),
        out_shape=jax.ShapeDtypeStruct((s, BRANCH_W), BF16),
        grid=(s // tb,),
        in_specs=[blk(C_RQ), blk(C_RK), blk(C_RV), blk(C_RG),
                  pl.BlockSpec((tb, HEAD_DIM), lambda i: (i, 0)),
                  pl.BlockSpec((tb, HEAD_DIM), lambda i: (i, 0))],
        out_specs=pl.BlockSpec((tb, BRANCH_W), lambda i: (i, 0)),
        scratch_shapes=[pltpu.VMEM((N_HEADS, HEAD_DIM, HEAD_DIM), F32)],
        compiler_params=_cparams(("arbitrary",)),
        name="retention",
    )(pm, pm, pm, pm, cos_t, sin_t)


def _gla_kernel(q_ref, k_ref, v_ref, g_ref, ps_ref, up_ref, gb_ref, bd_ref, o_ref, st_ref, *, tb):
    @pl.when(pl.program_id(0) == 0)
    def _():
        st_ref[...] = jnp.zeros_like(st_ref)

    pre = _dot(ps_ref[...].astype(BF16), up_ref[...].astype(BF16)) + gb_ref[...]
    la, _ = _log_sigmoid_parts(pre)
    la = la / GLA_GATE_NORM
    bd = bd_ref[...]
    h1, h2, h3 = _split3(la)
    b = _dot(bd, h1) + _dot(bd, h2) + _dot(bd, h3)
    eb = jnp.exp(b)
    ieb = jnp.exp(-b)
    scale = GLA_DK ** -0.5
    q = q_ref[...].astype(F32) * scale
    k = k_ref[...].astype(F32)
    qe = q * eb
    qi = q * ieb
    ke = (k * ieb).astype(BF16)
    kf = (k * eb).astype(BF16)
    lane = lax.broadcasted_iota(jnp.int32, (1, GLA_KW), 1)
    ri = lax.broadcasted_iota(jnp.int32, (CHUNK, CHUNK), 0)
    ci = lax.broadcasted_iota(jnp.int32, (CHUNK, CHUNK), 1)
    causal = ri >= ci
    for c in range(tb // CHUNK):
        rs = slice(c * CHUNK, (c + 1) * CHUNK)
        b_c = b[rs]
        b_last = b_c[CHUNK - 1:CHUNK, :]
        kd = k[rs] * jnp.exp(b_last - b_c)
        dec = jnp.exp(b_last)
        for h in range(N_HEADS):
            hm = jnp.logical_and(lane >= h * GLA_DK, lane < (h + 1) * GLA_DK)
            vs = slice(h * HEAD_DIM, (h + 1) * HEAD_DIM)
            qe_h = jnp.where(hm, qe[rs], 0.0).astype(BF16)
            qi_h = jnp.where(hm, qi[rs], 0.0).astype(BF16)
            s_past = _dot_nt(qe_h, ke[rs])
            s_future = _dot_nt(qi_h, kf[rs])
            scores = jnp.where(causal, s_past, s_future)
            v_c = v_ref[rs, vs]
            st = st_ref[h]
            o = _dot(scores.astype(BF16), v_c) + _dot_nt(qe_h, st.astype(BF16))
            kd_h = jnp.where(hm, kd, 0.0).astype(BF16)
            st_ref[h] = st * dec + _dot_tn(v_c, kd_h)
            on = o * lax.rsqrt(jnp.mean(o * o, axis=-1, keepdims=True) + LN_EPS)
            g = g_ref[rs, vs].astype(F32)
            o_ref[rs, vs] = (on * (g * jax.nn.sigmoid(g))).astype(o_ref.dtype)


def _gla(pm, ps, up_pad, gbias, tb):
    s = pm.shape[0]
    tb = min(tb, s)
    r = lax.broadcasted_iota(jnp.int32, (tb, tb), 0)
    c = lax.broadcasted_iota(jnp.int32, (tb, tb), 1)
    bd = jnp.logical_and(c <= r, (c // CHUNK) == (r // CHUNK)).astype(BF16)
    return pl.pallas_call(
        functools.partial(_gla_kernel, tb=tb),
        out_shape=jax.ShapeDtypeStruct((s, BRANCH_W), BF16),
        grid=(s // tb,),
        in_specs=[pl.BlockSpec((tb, GLA_KW), lambda i: (i, C_GQ // GLA_KW)),
                  pl.BlockSpec((tb, GLA_KW), lambda i: (i, C_GK // GLA_KW)),
                  pl.BlockSpec((tb, BRANCH_W), lambda i: (i, C_GV // BRANCH_W)),
                  pl.BlockSpec((tb, BRANCH_W), lambda i: (i, C_GR // BRANCH_W)),
                  pl.BlockSpec((tb, LANES), lambda i: (i, 0)),
                  pl.BlockSpec((LANES, GLA_KW), lambda i: (0, 0)),
                  pl.BlockSpec((1, GLA_KW), lambda i: (0, 0)),
                  pl.BlockSpec((tb, tb), lambda i: (0, 0))],
        out_specs=pl.BlockSpec((tb, BRANCH_W), lambda i: (i, 0)),
        scratch_shapes=[pltpu.VMEM((N_HEADS, HEAD_DIM, GLA_KW), F32)],
        compiler_params=_cparams(("arbitrary",)),
        name="gla",
    )(pm, pm, pm, pm, ps, up_pad, gbias, bd)


def _merge_kernel(x_ref, y0_ref, y1_ref, y2_ref, y3_ref, wg_ref, bg_ref, wb_ref, o_ref):
    x = x_ref[...]
    acc = None
    for n, y_ref in enumerate((y0_ref, y1_ref, y2_ref, y3_ref)):
        gate = jax.nn.sigmoid(_dot(x, wg_ref[0, n]) + bg_ref[n])
        term = gate * _dot(y_ref[...], wb_ref[0, n])
        acc = term if acc is None else acc + term
    o_ref[...] = acc.astype(o_ref.dtype)


def _merge(xb, ys, wg, bg, wb, l, bm, bn):
    m = xb.shape[0]
    bm = min(bm, m)
    yspec = pl.BlockSpec((bm, BRANCH_W), lambda j, i: (i, 0))
    return pl.pallas_call(
        _merge_kernel,
        out_shape=jax.ShapeDtypeStruct((m, D_MODEL), BF16),
        grid=(D_MODEL // bn, m // bm),
        in_specs=[pl.BlockSpec((bm, D_MODEL), lambda j, i: (i, 0)),
                  yspec, yspec, yspec, yspec,
                  pl.BlockSpec((1, N_BRANCH, D_MODEL, bn), lambda j, i: (l, 0, 0, j)),
                  pl.BlockSpec((N_BRANCH, 1, bn), lambda j, i: (0, 0, j)),
                  pl.BlockSpec((1, N_BRANCH, BRANCH_W, bn), lambda j, i: (l, 0, 0, j))],
        out_specs=pl.BlockSpec((bm, bn), lambda j, i: (i, j)),
        compiler_params=_cparams(("arbitrary", "arbitrary"), 52),
        name="merge",
    )(xb, *ys, wg, bg, wb)


def _layer_norm_rows(z, g, b):
    mu = jnp.mean(z, axis=-1, keepdims=True)
    d = z - mu
    var = jnp.mean(d * d, axis=-1, keepdims=True)
    return d * lax.rsqrt(var + LN_EPS) * g + b


def _outln_kernel(m_ref, w_ref, x_ref, g_ref, b_ref, o_ref, ob_ref, *, alpha):
    h = _dot(m_ref[...], w_ref[0])
    y = _layer_norm_rows(alpha * x_ref[...] + h, g_ref[...], b_ref[...])
    o_ref[...] = y
    ob_ref[...] = y.astype(BF16).reshape(ob_ref.shape)


def _out_ln(merged, w_out, l, x, g, b, alpha, bm):
    m = x.shape[0]
    bm = min(bm, m)
    row = pl.BlockSpec((bm, D_MODEL), lambda i: (i, 0))
    row3 = pl.BlockSpec((bm, ROW_TILES, LANES), lambda i: (i, 0, 0))
    vec = pl.BlockSpec((1, D_MODEL), lambda i: (0, 0))
    return pl.pallas_call(
        functools.partial(_outln_kernel, alpha=alpha),
        out_shape=(jax.ShapeDtypeStruct((m, D_MODEL), F32),
                   jax.ShapeDtypeStruct((m, ROW_TILES, LANES), BF16)),
        grid=(m // bm,),
        in_specs=[row, pl.BlockSpec((1, D_MODEL, D_MODEL), lambda i: (l, 0, 0)), row, vec, vec],
        out_specs=(row, row3),
        compiler_params=_cparams(("arbitrary",), 48),
        name="out_ln",
    )(merged, w_out, x, g, b)


def _router_kernel(x_ref, w_ref, b_ref, id_ref, wt_ref):
    x = x_ref[...]
    xh = x.astype(BF16)
    xl = (x - xh.astype(F32)).astype(BF16)
    w = w_ref[...]
    wh = w.astype(BF16)
    wl = (w - wh.astype(F32)).astype(BF16)
    logits = _dot(xh, wh) + _dot(xh, wl) + _dot(xl, wh) + b_ref[...]
    lane = lax.broadcasted_iota(jnp.int32, logits.shape, 1)
    neg = -jnp.inf
    gl = jnp.where(lane < N_GROUPS, logits, neg)
    gmax = jnp.max(gl, axis=-1, keepdims=True)
    g_sel = jnp.min(jnp.where(gl == gmax, lane, LANES), axis=-1, keepdims=True)
    g_w = 1.0 / jnp.sum(jnp.where(lane < N_GROUPS, jnp.exp(logits - gmax), 0.0), axis=-1, keepdims=True)
    lo = N_GROUPS + g_sel * EXPERTS_PER_GROUP
    el = jnp.where(jnp.logical_and(lane >= lo, lane < lo + EXPERTS_PER_GROUP), logits, neg)
    v1 = jnp.max(el, axis=-1, keepdims=True)
    i1 = jnp.min(jnp.where(el == v1, lane, LANES), axis=-1, keepdims=True)
    el2 = jnp.where(lane == i1, neg, el)
    v2 = jnp.max(el2, axis=-1, keepdims=True)
    i2 = jnp.min(jnp.where(el2 == v2, lane, LANES), axis=-1, keepdims=True)
    e2 = jnp.exp(v2 - v1)
    p1 = 1.0 / (1.0 + e2)
    p2 = e2 / (1.0 + e2)
    id_ref[...] = jnp.where(lane == 0, i1 - N_GROUPS, jnp.where(lane == 1, i2 - N_GROUPS, 0))
    wt_ref[...] = jnp.where(lane == 0, p1 * g_w, jnp.where(lane == 1, p2 * g_w, 0.0))


def _router(x1, w_r, b_r, bm):
    m = x1.shape[0]
    bm = min(bm, m)
    return pl.pallas_call(
        _router_kernel,
        out_shape=(jax.ShapeDtypeStruct((m, LANES), jnp.int32), jax.ShapeDtypeStruct((m, LANES), F32)),
        grid=(m // bm,),
        in_specs=[pl.BlockSpec((bm, D_MODEL), lambda i: (i, 0)),
                  pl.BlockSpec((D_MODEL, LANES), lambda i: (0, 0)),
                  pl.BlockSpec((1, LANES), lambda i: (0, 0))],
        out_specs=(pl.BlockSpec((bm, LANES), lambda i: (i, 0)), pl.BlockSpec((bm, LANES), lambda i: (i, 0))),
        compiler_params=_cparams(("arbitrary",)),
        name="router",
    )(x1, w_r, b_r)


def _start_row_gather(idx_ref, first, stride, n, src_ref, dst_ref, sem):
    def start(r, c):
        pltpu.make_async_copy(src_ref.at[idx_ref[first + stride * r]], dst_ref.at[r], sem).start()
        return c
    lax.fori_loop(0, n, start, 0, unroll=8)


def _wait_row_gather(n, src_ref, dst_ref, sem):
    pltpu.make_async_copy(src_ref.at[pl.ds(0, n)], dst_ref, sem).wait()


def _expert_kernel(tok_ref, be_ref, nv_ref, run_ref, nxt_ref, x_ref, w1_ref, w3_ref, w2_ref, o_ref,
                   xbuf, sem, w1s, w3s, w2s, wsem, w1b, w3b, w2b, *, tm, l):
    i = pl.program_id(0)
    n_valid = nv_ref[0]
    e = be_ref[i]
    prev = be_ref[jnp.maximum(i - 1, 0)]
    slot = lax.rem(i, 2)
    wslot = lax.rem(run_ref[i], 2)
    run_start = jnp.logical_and(i < n_valid, jnp.logical_or(i == 0, e != prev))

    def weight_copies(expert, s):
        return (pltpu.make_async_copy(w1_ref.at[l, expert], w1s.at[s], wsem.at[s]),
                pltpu.make_async_copy(w3_ref.at[l, expert], w3s.at[s], wsem.at[s]),
                pltpu.make_async_copy(w2_ref.at[l, expert], w2s.at[s], wsem.at[s]))

    @pl.when(i == 0)
    def _():
        for c in weight_copies(e, 0):
            c.start()
        _start_row_gather(tok_ref, 0, 1, tm, x_ref, xbuf.at[0], sem.at[0])

    @pl.when(i + 1 < n_valid)
    def _():
        _start_row_gather(tok_ref, (i + 1) * tm, 1, tm, x_ref, xbuf.at[1 - slot], sem.at[1 - slot])

    @pl.when(run_start)
    def _():
        for c in weight_copies(e, wslot):
            c.wait()
        nxt = nxt_ref[i]

        @pl.when(nxt >= 0)
        def _():
            for c in weight_copies(nxt, 1 - wslot):
                c.start()

        w1b[...] = w1s[wslot].astype(BF16)
        w3b[...] = w3s[wslot].astype(BF16)
        w2b[...] = w2s[wslot].astype(BF16)

    @pl.when(i < n_valid)
    def _():
        _wait_row_gather(tm, x_ref, xbuf.at[slot], sem.at[slot])
        x = xbuf[slot].reshape(tm, D_MODEL)
        a = _dot(x, w1b[...])
        h = (a * jax.nn.sigmoid(a)) * _dot(x, w3b[...])
        y = _dot(h.astype(BF16), w2b[...])
        o_ref[...] = y.astype(BF16).reshape(o_ref.shape)

    @pl.when(i >= n_valid)
    def _():
        o_ref[...] = jnp.zeros_like(o_ref)


def _experts(plan, x3, w1, w3, w2, l, tm):
    n = plan["row_tok"].shape[0]
    any_spec = pl.BlockSpec(memory_space=pl.ANY)
    grid_spec = pltpu.PrefetchScalarGridSpec(
        num_scalar_prefetch=5,
        grid=(n // tm,),
        in_specs=[any_spec, any_spec, any_spec, any_spec],
        out_specs=pl.BlockSpec((tm, ROW_TILES, LANES), lambda i, *_: (i, 0, 0)),
        scratch_shapes=[pltpu.VMEM((2, tm, ROW_TILES, LANES), BF16),
                        pltpu.SemaphoreType.DMA((2,)),
                        pltpu.VMEM((2, D_MODEL, D_EXPERT), F32),
                        pltpu.VMEM((2, D_MODEL, D_EXPERT), F32),
                        pltpu.VMEM((2, D_EXPERT, D_MODEL), F32),
                        pltpu.SemaphoreType.DMA((2,)),
                        pltpu.VMEM((D_MODEL, D_EXPERT), BF16),
                        pltpu.VMEM((D_MODEL, D_EXPERT), BF16),
                        pltpu.VMEM((D_EXPERT, D_MODEL), BF16)],
    )
    return pl.pallas_call(
        functools.partial(_expert_kernel, tm=tm, l=l),
        out_shape=jax.ShapeDtypeStruct((n, ROW_TILES, LANES), BF16),
        grid_spec=grid_spec,
        compiler_params=_cparams(("arbitrary",), 52),
        name="experts",
    )(plan["row_tok"], plan["blk_e"], plan["n_valid"], plan["run_id"], plan["next_e"], x3, w1, w3, w2)


def _combine_kernel(pos_ref, yb_ref, x_ref, wt_ref, g_ref, b_ref, o_ref, ob_ref, buf, sem, *, tm, alpha):
    i = pl.program_id(0)
    slot = lax.rem(i, 2)

    def start_tile(tile, s):
        for k in range(TOP_K):
            _start_row_gather(pos_ref, TOP_K * tile * tm + k, TOP_K, tm, yb_ref, buf.at[s, k], sem.at[s])

    @pl.when(i == 0)
    def _():
        start_tile(0, 0)

    @pl.when(i + 1 < pl.num_programs(0))
    def _():
        start_tile(i + 1, 1 - slot)

    for k in range(TOP_K):
        _wait_row_gather(tm, yb_ref, buf.at[slot, k], sem.at[slot])
    wt = wt_ref[...]
    h = None
    for k in range(TOP_K):
        term = buf[slot, k].reshape(tm, D_MODEL).astype(F32) * wt[:, k:k + 1]
        h = term if h is None else h + term
    y = _layer_norm_rows(alpha * x_ref[...] + h, g_ref[...], b_ref[...])
    o_ref[...] = y
    ob_ref[...] = y.astype(BF16)


def _combine_ln(pos, yb, x1, wts, g, b, alpha, tm):
    m = x1.shape[0]
    tm = min(tm, m)
    row = lambda i, pos: (i, 0)
    grid_spec = pltpu.PrefetchScalarGridSpec(
        num_scalar_prefetch=1,
        grid=(m // tm,),
        in_specs=[pl.BlockSpec(memory_space=pl.ANY),
                  pl.BlockSpec((tm, D_MODEL), row),
                  pl.BlockSpec((tm, LANES), row),
                  pl.BlockSpec((1, D_MODEL), lambda i, pos: (0, 0)),
                  pl.BlockSpec((1, D_MODEL), lambda i, pos: (0, 0))],
        out_specs=(pl.BlockSpec((tm, D_MODEL), row), pl.BlockSpec((tm, D_MODEL), row)),
        scratch_shapes=[pltpu.VMEM((2, TOP_K, tm, ROW_TILES, LANES), BF16), pltpu.SemaphoreType.DMA((2,))],
    )
    return pl.pallas_call(
        functools.partial(_combine_kernel, tm=tm, alpha=alpha),
        out_shape=(jax.ShapeDtypeStruct((m, D_MODEL), F32), jax.ShapeDtypeStruct((m, D_MODEL), BF16)),
        grid_spec=grid_spec,
        compiler_params=_cparams(("arbitrary",)),
        name="combine_ln",
    )(pos, yb, x1, wts, g, b)


def _dispatch_plan(ids, tm):
    t = ids.shape[0]
    flat_e = ids.reshape(-1)
    n_assign = flat_e.shape[0]
    onehot = (flat_e[:, None] == jnp.arange(N_EXPERTS, dtype=jnp.int32)[None, :]).astype(jnp.int32)
    csum = jnp.cumsum(onehot, axis=0)
    rank = jnp.sum((csum - onehot) * onehot, axis=1)
    counts = csum[-1]
    padded = (counts + tm - 1) // tm * tm
    pend = jnp.cumsum(padded)
    pstart = pend - padded
    dest = jnp.sum(onehot * pstart[None, :], axis=1) + rank
    n_rows = n_assign + N_EXPERTS * tm
    n_blocks = n_rows // tm
    flat_tok = jnp.arange(n_assign, dtype=jnp.int32) // TOP_K
    row_tok = jnp.zeros((n_rows,), jnp.int32).at[dest].set(flat_tok)
    blk_start = jnp.arange(n_blocks, dtype=jnp.int32) * tm
    blk_e = jnp.minimum(jnp.sum((pend[None, :] <= blk_start[:, None]).astype(jnp.int32), axis=1),
                        N_EXPERTS - 1).astype(jnp.int32)
    n_valid = (pend[-1] // tm).astype(jnp.int32)
    blk = jnp.arange(n_blocks, dtype=jnp.int32)
    is_start = jnp.logical_and(blk < n_valid, jnp.logical_or(blk == 0, blk_e != jnp.roll(blk_e, 1)))
    run_id = jnp.cumsum(is_start.astype(jnp.int32)) - 1
    start_pos = jnp.where(is_start, blk, n_blocks)
    next_start = lax.cummin(jnp.concatenate([start_pos[1:], jnp.full((1,), n_blocks, jnp.int32)]), reverse=True)
    next_e = jnp.where(next_start < n_blocks, blk_e[jnp.minimum(next_start, n_blocks - 1)], -1)
    return dict(row_tok=row_tok, blk_e=blk_e, n_valid=n_valid.reshape(1), run_id=run_id.astype(jnp.int32),
                next_e=next_e.astype(jnp.int32), dest=dest.astype(jnp.int32))


def _rope_tables(s):
    half = HEAD_DIM // 2
    inv = ROPE_BASE ** (-jnp.arange(half, dtype=F32) / half)
    ang = jnp.arange(s, dtype=F32)[:, None] * inv[None, :]
    cos, sin = jnp.cos(ang), jnp.sin(ang)
    return jnp.concatenate([cos, cos], axis=1), jnp.concatenate([-sin, sin], axis=1)


def kernel(x, w_in, fox_forget_bias, gla_gate_up, gla_gate_bias, w_branch, w_gate, b_gate, w_out, ln1_g, ln1_b, w_group, b_group, w_expert_router, b_expert_router, w1, w3, w2, ln2_g, ln2_b):
    bsz, s, d = x.shape
    depth = w_in.shape[0]
    alpha = DEEPNORM_ALPHA
    assert bsz == 1 and d == D_MODEL and w_in.shape[2] == IN_WIDTH
    t = bsz * s
    tm_moe = 256

    wg_b = w_gate.astype(BF16)
    wb_b = w_branch.astype(BF16)
    wo_b = w_out.astype(BF16)
    fbias = jnp.pad(fox_forget_bias, ((0, 0), (0, LANES - N_HEADS)))[:, None, :]
    up_pad = jnp.pad(gla_gate_up, ((0, 0), (S_GA, LANES - S_GA - GLA_GATE_RANK), (0, 0)))
    w_r = jnp.pad(jnp.concatenate([w_group, w_expert_router], axis=2),
                  ((0, 0), (0, 0), (0, LANES - N_GROUPS - N_EXPERTS)))
    b_r = jnp.pad(jnp.concatenate([b_group, b_expert_router], axis=1),
                  ((0, 0), (0, LANES - N_GROUPS - N_EXPERTS)))[:, None, :]
    cos_t, sin_t = _rope_tables(s)

    xf = x.reshape(t, d)
    xb = xf.astype(BF16)
    for l in range(depth):
        pm = _proj_main(xb, w_in, l, 1024)
        ps = _proj_small(xb, w_in, l, 1024)
        fcol = _forget_cumsum(ps, fbias[l], 512)
        y_fox = _fox_attention(pm, fcol, 512)
        y_ret = _retention(pm, cos_t, sin_t, 256)
        y_sb = _sb_attention(pm, 256)
        y_gla = _gla(pm, ps, up_pad[l], gla_gate_bias[l][None, :], 256)
        merged = _merge(xb, (y_fox, y_ret, y_sb, y_gla), wg_b, b_gate[l][:, None, :], wb_b, l, 512, 512)
        x1, x1b3 = _out_ln(merged, wo_b, l, xf, ln1_g[l][None, :], ln1_b[l][None, :], alpha, 512)

        ids, wts = _router(x1, w_r[l], b_r[l], 512)
        plan = _dispatch_plan(ids[:, :TOP_K], tm_moe)
        yb3 = _experts(plan, x1b3, w1, w3, w2, l, tm_moe)
        xf, xb = _combine_ln(plan["dest"], yb3, x1, wts, ln2_g[l][None, :], ln2_b[l][None, :], alpha, 256)
    return xf.reshape(bsz, s, d)
```

```python
import functools
import math

import jax
import jax.numpy as jnp
from jax import lax
from jax.experimental import pallas as pl
from jax.experimental.pallas import tpu as pltpu

F32 = jnp.float32
BF16 = jnp.bfloat16

D_MODEL = 2048
CHUNK = 64
N_HEADS = 4
HEAD_DIM = 128
BRANCH_W = N_HEADS * HEAD_DIM
GLA_DK = 64
GLA_KW = N_HEADS * GLA_DK
GLA_GATE_RANK = 16
GLA_GATE_NORM = 16.0
N_BRANCH = 4
N_GROUPS = 4
EXPERTS_PER_GROUP = 8
N_EXPERTS = N_GROUPS * EXPERTS_PER_GROUP
TOP_K = 2
D_EXPERT = D_MODEL // 4
ROPE_BASE = 10000.0
LN_EPS = 1e-5
DEPTH = 4
DEEPNORM_ALPHA = (2.0 * DEPTH) ** 0.25

LANES = 128
ROW_TILES = D_MODEL // LANES
NEG_BIG = -1e30
SB_EXIT = -104.0

C_FQ, C_FK, C_FV = 0, 512, 1024
C_RQ, C_RK, C_RV, C_RG = 1536, 2048, 2560, 3072
C_SQ, C_SK, C_SV = 3584, 4096, 4608
C_GQ, C_GK, C_GV, C_GR = 5120, 5376, 5632, 6144
MAIN_W = 6656
O_FF, O_GA = 1536, 6660
IN_WIDTH = 6676
S_FF, S_GA = 0, 4


def _cparams(sem, vmem_mb=None):
    kw = dict(dimension_semantics=sem)
    if vmem_mb is not None:
        kw["vmem_limit_bytes"] = vmem_mb * 1024 * 1024
    return pltpu.CompilerParams(**kw)


def _log_sigmoid_parts(z):
    t = jnp.log1p(jnp.exp(-jnp.abs(z)))
    return jnp.minimum(z, 0.0) - t, -(jnp.maximum(z, 0.0) + t)


def _split3(x):
    h1 = x.astype(BF16)
    r1 = x - h1.astype(F32)
    h2 = r1.astype(BF16)
    h3 = (r1 - h2.astype(F32)).astype(BF16)
    return h1, h2, h3


def _dot(a, b):
    return jnp.dot(a, b, preferred_element_type=F32)


def _dot_nt(a, b):
    return lax.dot_general(a, b, (((1,), (1,)), ((), ())), preferred_element_type=F32)


def _dot_tn(a, b):
    return lax.dot_general(a, b, (((0,), (0,)), ((), ())), preferred_element_type=F32)


PROJ_BN = 512
N_ALIGNED_TILES = O_FF // PROJ_BN


def _proj_kernel(x_ref, wa_ref, wb_ref, o_ref, w_scr):
    j = pl.program_id(0)
    first_row_tile = pl.program_id(1) == 0

    @pl.when(jnp.logical_and(first_row_tile, j < N_ALIGNED_TILES))
    def _():
        w_scr[...] = wa_ref[0]

    @pl.when(jnp.logical_and(first_row_tile, j >= N_ALIGNED_TILES))
    def _():
        ab = jnp.concatenate([wa_ref[0], wb_ref[0]], axis=1).astype(F32)
        width = PROJ_BN + LANES
        w_scr[...] = pltpu.roll(ab, width - N_HEADS, 1)[:, :PROJ_BN].astype(BF16)

    o_ref[...] = _dot(x_ref[...], w_scr[...]).astype(o_ref.dtype)


def _proj_main(xb, w_in, l, bm):
    m = xb.shape[0]
    bm = min(bm, m)
    lanes_per_tile = PROJ_BN // LANES
    return pl.pallas_call(
        _proj_kernel,
        out_shape=jax.ShapeDtypeStruct((m, MAIN_W), BF16),
        grid=(MAIN_W // PROJ_BN, m // bm),
        in_specs=[pl.BlockSpec((bm, D_MODEL), lambda j, i: (i, 0)),
                  pl.BlockSpec((1, D_MODEL, PROJ_BN), lambda j, i: (l, 0, j)),
                  pl.BlockSpec((1, D_MODEL, LANES), lambda j, i: (l, 0, lanes_per_tile * (j + 1)))],
        out_specs=pl.BlockSpec((bm, PROJ_BN), lambda j, i: (i, j)),
        scratch_shapes=[pltpu.VMEM((D_MODEL, PROJ_BN), BF16)],
        compiler_params=_cparams(("arbitrary", "arbitrary"), 48),
        name="proj_main",
    )(xb, w_in, w_in)


def _proj_small_kernel(x_ref, wf_ref, wg_ref, o_ref):
    lane = lax.broadcasted_iota(jnp.int32, (1, LANES), 1)
    w = jnp.where(lane < S_GA, wf_ref[0].astype(F32),
                  jnp.where(lane < S_GA + GLA_GATE_RANK, wg_ref[0].astype(F32), 0.0))
    o_ref[...] = _dot(x_ref[...], w.astype(BF16))


def _proj_small(xb, w_in, l, bm):
    m = xb.shape[0]
    bm = min(bm, m)
    assert O_FF % LANES == 0 and O_GA % LANES == S_GA
    return pl.pallas_call(
        _proj_small_kernel,
        out_shape=jax.ShapeDtypeStruct((m, LANES), F32),
        grid=(m // bm,),
        in_specs=[pl.BlockSpec((bm, D_MODEL), lambda i: (i, 0)),
                  pl.BlockSpec((1, D_MODEL, LANES), lambda i: (l, 0, O_FF // LANES)),
                  pl.BlockSpec((1, D_MODEL, LANES), lambda i: (l, 0, O_GA // LANES))],
        out_specs=pl.BlockSpec((bm, LANES), lambda i: (i, 0)),
        compiler_params=_cparams(("arbitrary",)),
        name="proj_small",
    )(xb, w_in, w_in)


def _fcum_kernel(ps_ref, bias_ref, tri_ref, o_ref, carry_ref):
    @pl.when(pl.program_id(0) == 0)
    def _():
        carry_ref[...] = jnp.zeros_like(carry_ref)

    lf, _ = _log_sigmoid_parts(ps_ref[...] + bias_ref[...])
    tri = tri_ref[...]
    h1, h2, h3 = _split3(lf)
    c = _dot(tri, h1) + _dot(tri, h2) + _dot(tri, h3) + carry_ref[0:1, :]
    o_ref[...] = c
    carry_ref[...] = jnp.broadcast_to(c[-1:, :], carry_ref.shape)


def _forget_cumsum(ps, bias_row, tb):
    s = ps.shape[0]
    tb = min(tb, s)
    r = lax.broadcasted_iota(jnp.int32, (tb, tb), 0)
    c = lax.broadcasted_iota(jnp.int32, (tb, tb), 1)
    tri = (c <= r).astype(BF16)
    return pl.pallas_call(
        _fcum_kernel,
        out_shape=jax.ShapeDtypeStruct((s, LANES), F32),
        grid=(s // tb,),
        in_specs=[pl.BlockSpec((tb, LANES), lambda i: (i, 0)),
                  pl.BlockSpec((1, LANES), lambda i: (0, 0)),
                  pl.BlockSpec((tb, tb), lambda i: (0, 0))],
        out_specs=pl.BlockSpec((tb, LANES), lambda i: (i, 0)),
        scratch_shapes=[pltpu.VMEM((8, LANES), F32)],
        compiler_params=_cparams(("arbitrary",)),
        name="forget_cumsum",
    )(ps, bias_row, tri)


FOX_AUG = 2 * HEAD_DIM
LOG2E = 1.4426950408889634


def _fox_prep_kernel(q_ref, k_ref, v_ref, f_ref, qa_ref, ka_ref, va_ref, st_ref):
    tb = q_ref.shape[0]
    lane = lax.broadcasted_iota(jnp.int32, (tb, HEAD_DIM), 1)
    lane1 = lax.broadcasted_iota(jnp.int32, (1, LANES), 1)
    inv_scale = HEAD_DIM ** 0.5
    c1 = HEAD_DIM ** -0.5 * LOG2E
    stats = jnp.zeros((1, LANES), F32)

    def max_norm(x_ref, sl):
        x = x_ref[:, sl].astype(F32)
        n2 = jnp.sum(x * x, axis=-1, keepdims=True)
        return jnp.sqrt(jnp.max(n2, axis=0, keepdims=True))

    for h in range(N_HEADS):
        sl = slice(h * HEAD_DIM, (h + 1) * HEAD_DIM)
        stats = jnp.where(lane1 == h, c1 * max_norm(q_ref, sl), stats)
        stats = jnp.where(lane1 == N_HEADS + h, max_norm(k_ref, sl), stats)
        stats = jnp.where(lane1 == 2 * N_HEADS + h, f_ref[0:1, h:h + 1] * LOG2E, stats)
        stats = jnp.where(lane1 == 3 * N_HEADS + h, f_ref[tb - 1:tb, h:h + 1] * LOG2E, stats)
        f = f_ref[:, h:h + 1] * inv_scale
        h1, h2, h3 = (p.astype(F32) for p in _split3(f))
        aq = jnp.where(lane == 0, h1, jnp.where(lane == 1, h2, jnp.where(lane == 2, h3,
                       jnp.where(lane < 6, 1.0, 0.0)))).astype(BF16)
        ak = jnp.where(lane < 3, 1.0, jnp.where(lane == 3, -h1, jnp.where(lane == 4, -h2,
                       jnp.where(lane == 5, -h3, 0.0)))).astype(BF16)
        qa_ref[:, h * FOX_AUG:h * FOX_AUG + HEAD_DIM] = q_ref[:, sl]
        qa_ref[:, h * FOX_AUG + HEAD_DIM:(h + 1) * FOX_AUG] = aq
        ka_ref[:, h * FOX_AUG:h * FOX_AUG + HEAD_DIM] = k_ref[:, sl]
        ka_ref[:, h * FOX_AUG + HEAD_DIM:(h + 1) * FOX_AUG] = ak
        va_ref[:, h * FOX_AUG:h * FOX_AUG + HEAD_DIM] = v_ref[:, sl]
        va_ref[:, h * FOX_AUG + HEAD_DIM:(h + 1) * FOX_AUG] = jnp.where(lane == 0, 1.0, 0.0).astype(BF16)
    st_ref[...] = jnp.broadcast_to(stats, st_ref.shape)


def _fox_prep(pm, fcol, tb):
    s = pm.shape[0]
    tb = min(tb, s)
    blk = lambda c: pl.BlockSpec((tb, BRANCH_W), lambda i, c=c: (i, c // BRANCH_W))
    out = pl.BlockSpec((tb, N_HEADS * FOX_AUG), lambda i: (i, 0))
    shp = jax.ShapeDtypeStruct((s, N_HEADS * FOX_AUG), BF16)
    return pl.pallas_call(
        _fox_prep_kernel,
        out_shape=(shp, shp, shp, jax.ShapeDtypeStruct((s // tb * 8, LANES), F32)),
        grid=(s // tb,),
        in_specs=[blk(C_FQ), blk(C_FK), blk(C_FV), pl.BlockSpec((tb, LANES), lambda i: (i, 0))],
        out_specs=(out, out, out, pl.BlockSpec((8, LANES), lambda i: (i, 0))),
        compiler_params=_cparams(("arbitrary",)),
        name="fox_prep",
    )(pm, pm, pm, fcol)


FOX_STATS = 4 * N_HEADS
FOX_SKIP = -160.0


def _fox_kernel(qi_ref, ki_ref, st_ref, q_ref, k_ref, v_ref, o_ref, m_ref, acc_ref, *, t):
    p_id = pl.program_id(0)
    qi = qi_ref[p_id]
    ki = ki_ref[p_id]
    c1 = HEAD_DIM ** -0.5 * LOG2E

    def block_is_dead():
        bound = jnp.float32(-jnp.inf)
        m_min = jnp.full((1, 1), jnp.inf, F32)
        for h in range(N_HEADS):
            qb, kb = qi * FOX_STATS + h, ki * FOX_STATS + h
            bound = jnp.maximum(bound, st_ref[qb] * st_ref[kb + N_HEADS]
                                + st_ref[qb + 2 * N_HEADS] - st_ref[kb + 3 * N_HEADS])
            m_min = jnp.minimum(m_min, jnp.min(m_ref[h], axis=0, keepdims=True))
        return bound - m_min[0, 0] < FOX_SKIP

    def step(masked):
        if masked:
            row = lax.broadcasted_iota(jnp.int32, (t, t), 0)
            col = lax.broadcasted_iota(jnp.int32, (t, t), 1)
            keep = col <= row
        for h in range(N_HEADS):
            sl = slice(h * FOX_AUG, (h + 1) * FOX_AUG)
            s = _dot_nt(q_ref[:, sl], k_ref[:, sl]) * c1
            if masked:
                s = jnp.where(keep, s, NEG_BIG)
            m_prev = m_ref[h]
            m_new = jnp.maximum(m_prev, jnp.max(s, axis=-1, keepdims=True))
            alpha = jnp.exp2(m_prev - m_new)
            p = jnp.exp2(s - m_new)
            acc_ref[h] = alpha * acc_ref[h] + _dot(p.astype(BF16), v_ref[:, sl])
            m_ref[h] = m_new

    @pl.when(ki == qi)
    def _():
        m_ref[...] = jnp.full_like(m_ref, NEG_BIG)
        acc_ref[...] = jnp.zeros_like(acc_ref)
        step(True)

    @pl.when(ki != qi)
    def _():
        @pl.when(jnp.logical_not(block_is_dead()))
        def _():
            step(False)

    @pl.when(ki == 0)
    def _():
        for h in range(N_HEADS):
            acc = acc_ref[h]
            o_ref[:, h * HEAD_DIM:(h + 1) * HEAD_DIM] = (
                acc[:, :HEAD_DIM] / acc[:, HEAD_DIM:HEAD_DIM + 1]).astype(o_ref.dtype)


def _fox_attention(pm, fcol, t):
    s = pm.shape[0]
    t = min(t, s)
    nb = s // t
    qa, ka, va, st = _fox_prep(pm, fcol, t)
    stats = st.reshape(nb, 8, LANES)[:, 0, :FOX_STATS].reshape(nb * FOX_STATS)
    pairs = [(i, j) for i in range(nb) for j in range(i, -1, -1)]
    qi_tab = jnp.asarray([p[0] for p in pairs], jnp.int32)
    ki_tab = jnp.asarray([p[1] for p in pairs], jnp.int32)
    w = N_HEADS * FOX_AUG
    grid_spec = pltpu.PrefetchScalarGridSpec(
        num_scalar_prefetch=3,
        grid=(len(pairs),),
        in_specs=[pl.BlockSpec((t, w), lambda p, qi, ki, st: (qi[p], 0)),
                  pl.BlockSpec((t, w), lambda p, qi, ki, st: (ki[p], 0)),
                  pl.BlockSpec((t, w), lambda p, qi, ki, st: (ki[p], 0))],
        out_specs=pl.BlockSpec((t, BRANCH_W), lambda p, qi, ki, st: (qi[p], 0)),
        scratch_shapes=[pltpu.VMEM((N_HEADS, t, 1), F32),
                        pltpu.VMEM((N_HEADS, t, FOX_AUG), F32)],
    )
    return pl.pallas_call(
        functools.partial(_fox_kernel, t=t),
        out_shape=jax.ShapeDtypeStruct((s, BRANCH_W), BF16),
        grid_spec=grid_spec,
        compiler_params=_cparams(("arbitrary",), 48),
        name="fox_attention",
    )(qi_tab, ki_tab, stats, qa, ka, va)


def _sb_kernel(q_ref, k_ref, v_ref, gt_ref, o_ref, r_ref, acc_ref, *, t):
    i = pl.program_id(0)
    r_ref[...] = jnp.zeros_like(r_ref)
    acc_ref[...] = jnp.zeros_like(acc_ref)
    scale = HEAD_DIM ** -0.5
    gt = gt_ref[...]

    def cond(c):
        jj, rmax = c
        return jnp.logical_and(jj <= i, rmax > SB_EXIT)

    def body(c):
        jj, _ = c
        kb = i - jj
        k0 = pl.multiple_of(kb * t, t)
        row = i * t + lax.broadcasted_iota(jnp.int32, (t, t), 0)
        col = kb * t + lax.broadcasted_iota(jnp.int32, (t, t), 1)
        keep = col < row
        rmax = jnp.full((1, 1), -jnp.inf, F32)
        for h in range(N_HEADS):
            sl = slice(h * HEAD_DIM, (h + 1) * HEAD_DIM)
            z = _dot_nt(q_ref[:, sl], k_ref[pl.ds(k0, t), sl]) * scale
            lsg, lom = _log_sigmoid_parts(z)
            lom = jnp.where(keep, lom, 0.0)
            hi = lom.astype(BF16)
            lo = (lom - hi.astype(F32)).astype(BF16)
            r_prev = r_ref[h]
            after = _dot(hi, gt) + _dot(lo, gt) + r_prev
            a = jnp.where(keep, jnp.exp(lsg + after), 0.0)
            acc_ref[h] += _dot(a.astype(BF16), v_ref[pl.ds(k0, t), sl])
            r_new = r_prev + jnp.sum(lom, axis=-1, keepdims=True)
            r_ref[h] = r_new
            rmax = jnp.maximum(rmax, jnp.max(r_new, axis=0, keepdims=True))
        return jj + 1, rmax[0, 0]

    lax.while_loop(cond, body, (jnp.int32(0), jnp.float32(0.0)))
    for h in range(N_HEADS):
        sl = slice(h * HEAD_DIM, (h + 1) * HEAD_DIM)
        o_ref[:, sl] = acc_ref[h].astype(o_ref.dtype)


def _sb_attention(pm, t):
    s = pm.shape[0]
    t = min(t, s)
    r = lax.broadcasted_iota(jnp.int32, (t, t), 0)
    c = lax.broadcasted_iota(jnp.int32, (t, t), 1)
    gt = (r > c).astype(BF16)
    return pl.pallas_call(
        functools.partial(_sb_kernel, t=t),
        out_shape=jax.ShapeDtypeStruct((s, BRANCH_W), BF16),
        grid=(s // t,),
        in_specs=[pl.BlockSpec((t, BRANCH_W), lambda i: (i, C_SQ // BRANCH_W)),
                  pl.BlockSpec((s, BRANCH_W), lambda i: (0, C_SK // BRANCH_W)),
                  pl.BlockSpec((s, BRANCH_W), lambda i: (0, C_SV // BRANCH_W)),
                  pl.BlockSpec((t, t), lambda i: (0, 0))],
        out_specs=pl.BlockSpec((t, BRANCH_W), lambda i: (i, 0)),
        scratch_shapes=[pltpu.VMEM((N_HEADS, t, 1), F32),
                        pltpu.VMEM((N_HEADS, t, HEAD_DIM), F32)],
        compiler_params=_cparams(("arbitrary",), 48),
        name="sb_attention",
    )(pm, pm, pm, gt)


def _ret_kernel(q_ref, k_ref, v_ref, g_ref, cos_ref, sin_ref, o_ref, st_ref, *, tb):
    @pl.when(pl.program_id(0) == 0)
    def _():
        st_ref[...] = jnp.zeros_like(st_ref)

    cos = cos_ref[...]
    sin = sin_ref[...]
    ri = lax.broadcasted_iota(jnp.int32, (tb, tb), 0)
    ci = lax.broadcasted_iota(jnp.int32, (tb, tb), 1)
    chunk_ok = (ci // CHUNK) <= (ri // CHUNK)
    dist = jnp.abs(ri - ci).astype(F32)
    idx = lax.broadcasted_iota(jnp.int32, (tb, 1), 0).astype(F32)
    scale = HEAD_DIM ** -0.5
    for h in range(N_HEADS):
        sl = slice(h * HEAD_DIM, (h + 1) * HEAD_DIM)
        lg = math.log(1.0 - 2.0 ** (-5.0 - h))
        q = q_ref[:, sl].astype(F32)
        k = k_ref[:, sl].astype(F32)
        v = v_ref[:, sl]
        qr = q * cos + pltpu.roll(q, HEAD_DIM // 2, 1) * sin
        kr = (k * cos + pltpu.roll(k, HEAD_DIM // 2, 1) * sin) * scale
        decay = jnp.where(chunk_ok, jnp.exp(lg * dist), 0.0)
        scores = _dot_nt(qr.astype(BF16), kr.astype(BF16)) * decay
        intra = _dot(scores.astype(BF16), v)
        q_dec = jnp.exp(lg * (idx + 1.0))
        k_dec = jnp.exp(lg * (tb - 1.0 - idx))
        state = st_ref[h]
        inter = _dot((qr * q_dec).astype(BF16), state.astype(BF16))
        kv = _dot_tn((kr * k_dec).astype(BF16), v)
        st_ref[h] = state * math.exp(lg * tb) + kv
        o = intra + inter
        mu = jnp.mean(o, axis=-1, keepdims=True)
        d = o - mu
        var = jnp.mean(d * d, axis=-1, keepdims=True)
        on = d * lax.rsqrt(var + LN_EPS)
        g = g_ref[:, sl].astype(F32)
        o_ref[:, sl] = (on * (g * jax.nn.sigmoid(g))).astype(o_ref.dtype)


def _retention(pm, cos_t, sin_t, tb):
    s = pm.shape[0]
    tb = min(tb, s)
    blk = lambda c: pl.BlockSpec((tb, BRANCH_W), lambda i, c=c: (i, c // BRANCH_W))
    return pl.pallas_call(
        functools.partial(_ret_kernel, tb=tb),
        out_shape=jax.ShapeDtypeStruct((s, BRANCH_W), BF16),
        grid=(s // tb,),
        in_specs=[blk(C_RQ), blk(C_RK), blk(C_RV), blk(C_RG),
                  pl.BlockSpec((tb, HEAD_DIM), lambda i: (i, 0)),
                  pl.BlockSpec((tb, HEAD_DIM), lambda i: (i, 0))],
        out_specs=pl.BlockSpec((tb, BRANCH_W), lambda i: (i, 0)),
        scratch_shapes=[pltpu.VMEM((N_HEADS, HEAD_DIM, HEAD_DIM), F32)],
        compiler_params=_cparams(("arbitrary",)),
        name="retention",
    )(pm, pm, pm, pm, cos_t, sin_t)


def _gla_kernel(q_ref, k_ref, v_ref, g_ref, ps_ref, up_ref, gb_ref, bd_ref, o_ref, st_ref, *, tb):
    @pl.when(pl.program_id(0) == 0)
    def _():
        st_ref[...] = jnp.zeros_like(st_ref)

    pre = _dot(ps_ref[...].astype(BF16), up_ref[...].astype(BF16)) + gb_ref[...]
    la, _ = _log_sigmoid_parts(pre)
    la = la / GLA_GATE_NORM
    bd = bd_ref[...]
    h1, h2, h3 = _split3(la)
    b = _dot(bd, h1) + _dot(bd, h2) + _dot(bd, h3)
    eb = jnp.exp(b)
    ieb = jnp.exp(-b)
    scale = GLA_DK ** -0.5
    q = q_ref[...].astype(F32) * scale
    k = k_ref[...].astype(F32)
    qe = q * eb
    qi = q * ieb
    ke = (k * ieb).astype(BF16)
    kf = (k * eb).astype(BF16)
    lane = lax.broadcasted_iota(jnp.int32, (1, GLA_KW), 1)
    ri = lax.broadcasted_iota(jnp.int32, (CHUNK, CHUNK), 0)
    ci = lax.broadcasted_iota(jnp.int32, (CHUNK, CHUNK), 1)
    causal = ri >= ci
    for c in range(tb // CHUNK):
        rs = slice(c * CHUNK, (c + 1) * CHUNK)
        b_c = b[rs]
        b_last = b_c[CHUNK - 1:CHUNK, :]
        kd = k[rs] * jnp.exp(b_last - b_c)
        dec = jnp.exp(b_last)
        for h in range(N_HEADS):
            hm = jnp.logical_and(lane >= h * GLA_DK, lane < (h + 1) * GLA_DK)
            vs = slice(h * HEAD_DIM, (h + 1) * HEAD_DIM)
            qe_h = jnp.where(hm, qe[rs], 0.0).astype(BF16)
            qi_h = jnp.where(hm, qi[rs], 0.0).astype(BF16)
            s_past = _dot_nt(qe_h, ke[rs])
            s_future = _dot_nt(qi_h, kf[rs])
            scores = jnp.where(causal, s_past, s_future)
            v_c = v_ref[rs, vs]
            st = st_ref[h]
            o = _dot(scores.astype(BF16), v_c) + _dot_nt(qe_h, st.astype(BF16))
            kd_h = jnp.where(hm, kd, 0.0).astype(BF16)
            st_ref[h] = st * dec + _dot_tn(v_c, kd_h)
            on = o * lax.rsqrt(jnp.mean(o * o, axis=-1, keepdims=True) + LN_EPS)
            g = g_ref[rs, vs].astype(F32)
            o_ref[rs, vs] = (on * (g * jax.nn.sigmoid(g))).astype(o_ref.dtype)


def _gla(pm, ps, up_pad, gbias, tb):
    s = pm.shape[0]
    tb = min(tb, s)
    r = lax.broadcasted_iota(jnp.int32, (tb, tb), 0)
    c = lax.broadcasted_iota(jnp.int32, (tb, tb), 1)
    bd = jnp.logical_and(c <= r, (c // CHUNK) == (r // CHUNK)).astype(BF16)
    return pl.pallas_call(
        functools.partial(_gla_kernel, tb=tb),
        out_shape=jax.ShapeDtypeStruct((s, BRANCH_W), BF16),
        grid=(s // tb,),
        in_specs=[pl.BlockSpec((tb, GLA_KW), lambda i: (i, C_GQ // GLA_KW)),
                  pl.BlockSpec((tb, GLA_KW), lambda i: (i, C_GK // GLA_KW)),
                  pl.BlockSpec((tb, BRANCH_W), lambda i: (i, C_GV // BRANCH_W)),
                  pl.BlockSpec((tb, BRANCH_W), lambda i: (i, C_GR // BRANCH_W)),
                  pl.BlockSpec((tb, LANES), lambda i: (i, 0)),
                  pl.BlockSpec((LANES, GLA_KW), lambda i: (0, 0)),
                  pl.BlockSpec((1, GLA_KW), lambda i: (0, 0)),
                  pl.BlockSpec((tb, tb), lambda i: (0, 0))],
        out_specs=pl.BlockSpec((tb, BRANCH_W), lambda i: (i, 0)),
        scratch_shapes=[pltpu.VMEM((N_HEADS, HEAD_DIM, GLA_KW), F32)],
        compiler_params=_cparams(("arbitrary",)),
        name="gla",
    )(pm, pm, pm, pm, ps, up_pad, gbias, bd)


def _merge_kernel(x_ref, y0_ref, y1_ref, y2_ref, y3_ref, wg_ref, bg_ref, wb_ref, o_ref):
    x = x_ref[...]
    acc = None
    for n, y_ref in enumerate((y0_ref, y1_ref, y2_ref, y3_ref)):
        gate = jax.nn.sigmoid(_dot(x, wg_ref[0, n]) + bg_ref[n])
        term = gate * _dot(y_ref[...], wb_ref[0, n])
        acc = term if acc is None else acc + term
    o_ref[...] = acc.astype(o_ref.dtype)


def _merge(xb, ys, wg, bg, wb, l, bm, bn):
    m = xb.shape[0]
    bm = min(bm, m)
    yspec = pl.BlockSpec((bm, BRANCH_W), lambda j, i: (i, 0))
    return pl.pallas_call(
        _merge_kernel,
        out_shape=jax.ShapeDtypeStruct((m, D_MODEL), BF16),
        grid=(D_MODEL // bn, m // bm),
        in_specs=[pl.BlockSpec((bm, D_MODEL), lambda j, i: (i, 0)),
                  yspec, yspec, yspec, yspec,
                  pl.BlockSpec((1, N_BRANCH, D_MODEL, bn), lambda j, i: (l, 0, 0, j)),
                  pl.BlockSpec((N_BRANCH, 1, bn), lambda j, i: (0, 0, j)),
                  pl.BlockSpec((1, N_BRANCH, BRANCH_W, bn), lambda j, i: (l, 0, 0, j))],
        out_specs=pl.BlockSpec((bm, bn), lambda j, i: (i, j)),
        compiler_params=_cparams(("arbitrary", "arbitrary"), 52),
        name="merge",
    )(xb, *ys, wg, bg, wb)


def _layer_norm_rows(z, g, b):
    mu = jnp.mean(z, axis=-1, keepdims=True)
    d = z - mu
    var = jnp.mean(d * d, axis=-1, keepdims=True)
    return d * lax.rsqrt(var + LN_EPS) * g + b


def _outln_kernel(m_ref, w_ref, x_ref, g_ref, b_ref, o_ref, ob_ref, *, alpha):
    h = _dot(m_ref[...], w_ref[0])
    y = _layer_norm_rows(alpha * x_ref[...] + h, g_ref[...], b_ref[...])
    o_ref[...] = y
    ob_ref[...] = y.astype(BF16).reshape(ob_ref.shape)


def _out_ln(merged, w_out, l, x, g, b, alpha, bm):
    m = x.shape[0]
    bm = min(bm, m)
    row = pl.BlockSpec((bm, D_MODEL), lambda i: (i, 0))
    row3 = pl.BlockSpec((bm, ROW_TILES, LANES), lambda i: (i, 0, 0))
    vec = pl.BlockSpec((1, D_MODEL), lambda i: (0, 0))
    return pl.pallas_call(
        functools.partial(_outln_kernel, alpha=alpha),
        out_shape=(jax.ShapeDtypeStruct((m, D_MODEL), F32),
                   jax.ShapeDtypeStruct((m, ROW_TILES, LANES), BF16)),
        grid=(m // bm,),
        in_specs=[row, pl.BlockSpec((1, D_MODEL, D_MODEL), lambda i: (l, 0, 0)), row, vec, vec],
        out_specs=(row, row3),
        compiler_params=_cparams(("arbitrary",), 48),
        name="out_ln",
    )(merged, w_out, x, g, b)


def _router_kernel(x_ref, w_ref, b_ref, id_ref, wt_ref):
    x = x_ref[...]
    xh = x.astype(BF16)
    xl = (x - xh.astype(F32)).astype(BF16)
    w = w_ref[...]
    wh = w.astype(BF16)
    wl = (w - wh.astype(F32)).astype(BF16)
    logits = _dot(xh, wh) + _dot(xh, wl) + _dot(xl, wh) + b_ref[...]
    lane = lax.broadcasted_iota(jnp.int32, logits.shape, 1)
    neg = -jnp.inf
    gl = jnp.where(lane < N_GROUPS, logits, neg)
    gmax = jnp.max(gl, axis=-1, keepdims=True)
    g_sel = jnp.min(jnp.where(gl == gmax, lane, LANES), axis=-1, keepdims=True)
    g_w = 1.0 / jnp.sum(jnp.where(lane < N_GROUPS, jnp.exp(logits - gmax), 0.0), axis=-1, keepdims=True)
    lo = N_GROUPS + g_sel * EXPERTS_PER_GROUP
    el = jnp.where(jnp.logical_and(lane >= lo, lane < lo + EXPERTS_PER_GROUP), logits, neg)
    v1 = jnp.max(el, axis=-1, keepdims=True)
    i1 = jnp.min(jnp.where(el == v1, lane, LANES), axis=-1, keepdims=True)
    el2 = jnp.where(lane == i1, neg, el)
    v2 = jnp.max(el2, axis=-1, keepdims=True)
    i2 = jnp.min(jnp.where(el2 == v2, lane, LANES), axis=-1, keepdims=True)
    e2 = jnp.exp(v2 - v1)
    p1 = 1.0 / (1.0 + e2)
    p2 = e2 / (1.0 + e2)
    id_ref[...] = jnp.where(lane == 0, i1 - N_GROUPS, jnp.where(lane == 1, i2 - N_GROUPS, 0))
    wt_ref[...] = jnp.where(lane == 0, p1 * g_w, jnp.where(lane == 1, p2 * g_w, 0.0))


def _router(x1, w_r, b_r, bm):
    m = x1.shape[0]
    bm = min(bm, m)
    return pl.pallas_call(
        _router_kernel,
        out_shape=(jax.ShapeDtypeStruct((m, LANES), jnp.int32), jax.ShapeDtypeStruct((m, LANES), F32)),
        grid=(m // bm,),
        in_specs=[pl.BlockSpec((bm, D_MODEL), lambda i: (i, 0)),
                  pl.BlockSpec((D_MODEL, LANES), lambda i: (0, 0)),
                  pl.BlockSpec((1, LANES), lambda i: (0, 0))],
        out_specs=(pl.BlockSpec((bm, LANES), lambda i: (i, 0)), pl.BlockSpec((bm, LANES), lambda i: (i, 0))),
        compiler_params=_cparams(("arbitrary",)),
        name="router",
    )(x1, w_r, b_r)


def _start_row_gather(idx_ref, first, stride, n, src_ref, dst_ref, sem, both_queues=False):
    def copy(r):
        return pltpu.make_async_copy(src_ref.at[idx_ref[first + stride * r]], dst_ref.at[r], sem)

    if both_queues:
        def start2(r2, c):
            copy(2 * r2).start(priority=0)
            copy(2 * r2 + 1).start(priority=1)
            return c
        lax.fori_loop(0, n // 2, start2, 0, unroll=4)
    else:
        def start(r, c):
            copy(r).start()
            return c
        lax.fori_loop(0, n, start, 0, unroll=8)


def _wait_row_gather(n, src_ref, dst_ref, sem):
    pltpu.make_async_copy(src_ref.at[pl.ds(0, n)], dst_ref, sem).wait()


def _expert_kernel(tok_ref, be_ref, nv_ref, run_ref, nxt_ref, x_ref, w1_ref, w3_ref, w2_ref, o_ref,
                   xbuf, sem, w1s, w3s, w2s, wsem, w1b, w3b, w2b, *, tm, l):
    i = pl.program_id(0)
    n_valid = nv_ref[0]
    e = be_ref[i]
    prev = be_ref[jnp.maximum(i - 1, 0)]
    slot = lax.rem(i, 2)
    wslot = lax.rem(run_ref[i], 2)
    run_start = jnp.logical_and(i < n_valid, jnp.logical_or(i == 0, e != prev))

    def weight_copies(expert, s):
        return (pltpu.make_async_copy(w1_ref.at[l, expert], w1s.at[s], wsem.at[s]),
                pltpu.make_async_copy(w3_ref.at[l, expert], w3s.at[s], wsem.at[s]),
                pltpu.make_async_copy(w2_ref.at[l, expert], w2s.at[s], wsem.at[s]))

    @pl.when(i == 0)
    def _():
        for c in weight_copies(e, 0):
            c.start(priority=1)
        _start_row_gather(tok_ref, 0, 1, tm, x_ref, xbuf.at[0], sem.at[0])

    @pl.when(i + 1 < n_valid)
    def _():
        _start_row_gather(tok_ref, (i + 1) * tm, 1, tm, x_ref, xbuf.at[1 - slot], sem.at[1 - slot])

    @pl.when(run_start)
    def _():
        for c in weight_copies(e, wslot):
            c.wait()
        nxt = nxt_ref[i]

        @pl.when(nxt >= 0)
        def _():
            for c in weight_copies(nxt, 1 - wslot):
                c.start(priority=1)

        w1b[...] = w1s[wslot].astype(BF16)
        w3b[...] = w3s[wslot].astype(BF16)
        w2b[...] = w2s[wslot].astype(BF16)

    @pl.when(i < n_valid)
    def _():
        _wait_row_gather(tm, x_ref, xbuf.at[slot], sem.at[slot])
        x = xbuf[slot].reshape(tm, D_MODEL)
        a = _dot(x, w1b[...])
        h = (a * jax.nn.sigmoid(a)) * _dot(x, w3b[...])
        y = _dot(h.astype(BF16), w2b[...])
        o_ref[...] = y.astype(BF16).reshape(o_ref.shape)

    @pl.when(i >= n_valid)
    def _():
        o_ref[...] = jnp.zeros_like(o_ref)


def _experts(plan, x3, w1, w3, w2, l, tm):
    n = plan["row_tok"].shape[0]
    any_spec = pl.BlockSpec(memory_space=pl.ANY)
    grid_spec = pltpu.PrefetchScalarGridSpec(
        num_scalar_prefetch=5,
        grid=(n // tm,),
        in_specs=[any_spec, any_spec, any_spec, any_spec],
        out_specs=pl.BlockSpec((tm, ROW_TILES, LANES), lambda i, *_: (i, 0, 0)),
        scratch_shapes=[pltpu.VMEM((2, tm, ROW_TILES, LANES), BF16),
                        pltpu.SemaphoreType.DMA((2,)),
                        pltpu.VMEM((2, D_MODEL, D_EXPERT), F32),
                        pltpu.VMEM((2, D_MODEL, D_EXPERT), F32),
                        pltpu.VMEM((2, D_EXPERT, D_MODEL), F32),
                        pltpu.SemaphoreType.DMA((2,)),
                        pltpu.VMEM((D_MODEL, D_EXPERT), BF16),
                        pltpu.VMEM((D_MODEL, D_EXPERT), BF16),
                        pltpu.VMEM((D_EXPERT, D_MODEL), BF16)],
    )
    return pl.pallas_call(
        functools.partial(_expert_kernel, tm=tm, l=l),
        out_shape=jax.ShapeDtypeStruct((n, ROW_TILES, LANES), BF16),
        grid_spec=grid_spec,
        compiler_params=_cparams(("arbitrary",), 52),
        name="experts",
    )(plan["row_tok"], plan["blk_e"], plan["n_valid"], plan["run_id"], plan["next_e"], x3, w1, w3, w2)


def _combine_kernel(pos_ref, yb_ref, x_ref, wt_ref, g_ref, b_ref, o_ref, ob_ref, buf, sem, *, tm, alpha):
    i = pl.program_id(0)
    slot = lax.rem(i, 2)

    def start_tile(tile, s):
        for k in range(TOP_K):
            _start_row_gather(pos_ref, TOP_K * tile * tm + k, TOP_K, tm, yb_ref, buf.at[s, k], sem.at[s],
                              both_queues=True)

    @pl.when(i == 0)
    def _():
        start_tile(0, 0)

    @pl.when(i + 1 < pl.num_programs(0))
    def _():
        start_tile(i + 1, 1 - slot)

    for k in range(TOP_K):
        _wait_row_gather(tm, yb_ref, buf.at[slot, k], sem.at[slot])
    wt = wt_ref[...]
    h = None
    for k in range(TOP_K):
        term = buf[slot, k].reshape(tm, D_MODEL).astype(F32) * wt[:, k:k + 1]
        h = term if h is None else h + term
    y = _layer_norm_rows(alpha * x_ref[...] + h, g_ref[...], b_ref[...])
    o_ref[...] = y
    ob_ref[...] = y.astype(BF16)


def _combine_ln(pos, yb, x1, wts, g, b, alpha, tm):
    m = x1.shape[0]
    tm = min(tm, m)
    row = lambda i, pos: (i, 0)
    grid_spec = pltpu.PrefetchScalarGridSpec(
        num_scalar_prefetch=1,
        grid=(m // tm,),
        in_specs=[pl.BlockSpec(memory_space=pl.ANY),
                  pl.BlockSpec((tm, D_MODEL), row),
                  pl.BlockSpec((tm, LANES), row),
                  pl.BlockSpec((1, D_MODEL), lambda i, pos: (0, 0)),
                  pl.BlockSpec((1, D_MODEL), lambda i, pos: (0, 0))],
        out_specs=(pl.BlockSpec((tm, D_MODEL), row), pl.BlockSpec((tm, D_MODEL), row)),
        scratch_shapes=[pltpu.VMEM((2, TOP_K, tm, ROW_TILES, LANES), BF16), pltpu.SemaphoreType.DMA((2,))],
    )
    return pl.pallas_call(
        functools.partial(_combine_kernel, tm=tm, alpha=alpha),
        out_shape=(jax.ShapeDtypeStruct((m, D_MODEL), F32), jax.ShapeDtypeStruct((m, D_MODEL), BF16)),
        grid_spec=grid_spec,
        compiler_params=_cparams(("arbitrary",)),
        name="combine_ln",
    )(pos, yb, x1, wts, g, b)


def _dispatch_plan(ids, tm):
    t = ids.shape[0]
    flat_e = ids.reshape(-1)
    n_assign = flat_e.shape[0]
    onehot = (flat_e[:, None] == jnp.arange(N_EXPERTS, dtype=jnp.int32)[None, :]).astype(jnp.int32)
    csum = jnp.cumsum(onehot, axis=0)
    rank = jnp.sum((csum - onehot) * onehot, axis=1)
    counts = csum[-1]
    padded = (counts + tm - 1) // tm * tm
    pend = jnp.cumsum(padded)
    pstart = pend - padded
    dest = jnp.sum(onehot * pstart[None, :], axis=1) + rank
    n_rows = n_assign + N_EXPERTS * tm
    n_blocks = n_rows // tm
    flat_tok = jnp.arange(n_assign, dtype=jnp.int32) // TOP_K
    row_tok = jnp.zeros((n_rows,), jnp.int32).at[dest].set(flat_tok)
    blk_start = jnp.arange(n_blocks, dtype=jnp.int32) * tm
    blk_e = jnp.minimum(jnp.sum((pend[None, :] <= blk_start[:, None]).astype(jnp.int32), axis=1),
                        N_EXPERTS - 1).astype(jnp.int32)
    n_valid = (pend[-1] // tm).astype(jnp.int32)
    blk = jnp.arange(n_blocks, dtype=jnp.int32)
    is_start = jnp.logical_and(blk < n_valid, jnp.logical_or(blk == 0, blk_e != jnp.roll(blk_e, 1)))
    run_id = jnp.cumsum(is_start.astype(jnp.int32)) - 1
    start_pos = jnp.where(is_start, blk, n_blocks)
    next_start = lax.cummin(jnp.concatenate([start_pos[1:], jnp.full((1,), n_blocks, jnp.int32)]), reverse=True)
    next_e = jnp.where(next_start < n_blocks, blk_e[jnp.minimum(next_start, n_blocks - 1)], -1)
    return dict(row_tok=row_tok, blk_e=blk_e, n_valid=n_valid.reshape(1), run_id=run_id.astype(jnp.int32),
                next_e=next_e.astype(jnp.int32), dest=dest.astype(jnp.int32))


def _rope_tables(s):
    half = HEAD_DIM // 2
    inv = ROPE_BASE ** (-jnp.arange(half, dtype=F32) / half)
    ang = jnp.arange(s, dtype=F32)[:, None] * inv[None, :]
    cos, sin = jnp.cos(ang), jnp.sin(ang)
    return jnp.concatenate([cos, cos], axis=1), jnp.concatenate([-sin, sin], axis=1)


def kernel(x, w_in, fox_forget_bias, gla_gate_up, gla_gate_bias, w_branch, w_gate, b_gate, w_out, ln1_g, ln1_b, w_group, b_group, w_expert_router, b_expert_router, w1, w3, w2, ln2_g, ln2_b):
    bsz, s, d = x.shape
    depth = w_in.shape[0]
    alpha = DEEPNORM_ALPHA
    assert bsz == 1 and d == D_MODEL and w_in.shape[2] == IN_WIDTH
    t = bsz * s
    tm_moe = 256

    w_in_b = w_in.astype(BF16)
    wg_b = w_gate.astype(BF16)
    wb_b = w_branch.astype(BF16)
    wo_b = w_out.astype(BF16)
    fbias = jnp.pad(fox_forget_bias, ((0, 0), (0, LANES - N_HEADS)))[:, None, :]
    up_pad = jnp.pad(gla_gate_up, ((0, 0), (S_GA, LANES - S_GA - GLA_GATE_RANK), (0, 0)))
    w_r = jnp.pad(jnp.concatenate([w_group, w_expert_router], axis=2),
                  ((0, 0), (0, 0), (0, LANES - N_GROUPS - N_EXPERTS)))
    b_r = jnp.pad(jnp.concatenate([b_group, b_expert_router], axis=1),
                  ((0, 0), (0, LANES - N_GROUPS - N_EXPERTS)))[:, None, :]
    cos_t, sin_t = _rope_tables(s)

    xf = x.reshape(t, d)
    xb = xf.astype(BF16)
    for l in range(depth):
        pm = _proj_main(xb, w_in_b, l, 1024)
        ps = _proj_small(xb, w_in_b, l, 1024)
        fcol = _forget_cumsum(ps, fbias[l], 512)
        y_fox = _fox_attention(pm, fcol, 512)
        y_ret = _retention(pm, cos_t, sin_t, 256)
        y_sb = _sb_attention(pm, 256)
        y_gla = _gla(pm, ps, up_pad[l], gla_gate_bias[l][None, :], 256)
        merged = _merge(xb, (y_fox, y_ret, y_sb, y_gla), wg_b, b_gate[l][:, None, :], wb_b, l, 512, 512)
        x1, x1b3 = _out_ln(merged, wo_b, l, xf, ln1_g[l][None, :], ln1_b[l][None, :], alpha, 512)

        ids, wts = _router(x1, w_r[l], b_r[l], 512)
        plan = _dispatch_plan(ids[:, :TOP_K], tm_moe)
        yb3 = _experts(plan, x1b3, w1, w3, w2, l, tm_moe)
        xf, xb = _combine_ln(plan["dest"], yb3, x1, wts, ln2_g[l][None, :], ln2_b[l][None, :], alpha, 256)
    return xf.reshape(bsz, s, d)
```

```python
import functools
import math

import jax
import jax.numpy as jnp
from jax import lax
from jax.experimental import pallas as pl
from jax.experimental.pallas import tpu as pltpu

F32 = jnp.float32
BF16 = jnp.bfloat16

D_MODEL = 2048
CHUNK = 64
N_HEADS = 4
HEAD_DIM = 128
BRANCH_W = N_HEADS * HEAD_DIM
GLA_DK = 64
GLA_KW = N_HEADS * GLA_DK
GLA_GATE_RANK = 16
GLA_GATE_NORM = 16.0
N_BRANCH = 4
N_GROUPS = 4
EXPERTS_PER_GROUP = 8
N_EXPERTS = N_GROUPS * EXPERTS_PER_GROUP
TOP_K = 2
D_EXPERT = D_MODEL // 4
ROPE_BASE = 10000.0
LN_EPS = 1e-5
DEPTH = 4
DEEPNORM_ALPHA = (2.0 * DEPTH) ** 0.25

LANES = 128
ROW_TILES = D_MODEL // LANES
NEG_BIG = -1e30
SB_EXIT = -104.0

C_FQ, C_FK, C_FV = 0, 512, 1024
C_RQ, C_RK, C_RV, C_RG = 1536, 2048, 2560, 3072
C_SQ, C_SK, C_SV = 3584, 4096, 4608
C_GQ, C_GK, C_GV, C_GR = 5120, 5376, 5632, 6144
MAIN_W = 6656
O_FF, O_GA = 1536, 6660
IN_WIDTH = 6676
S_FF, S_GA = 0, 4


def _cparams(sem, vmem_mb=None):
    kw = dict(dimension_semantics=sem)
    if vmem_mb is not None:
        kw["vmem_limit_bytes"] = vmem_mb * 1024 * 1024
    return pltpu.CompilerParams(**kw)


def _log_sigmoid_parts(z):
    t = jnp.log(1.0 + jnp.exp(-jnp.abs(z)))
    return jnp.minimum(z, 0.0) - t, -(jnp.maximum(z, 0.0) + t)


def _split3(x):
    h1 = x.astype(BF16)
    r1 = x - h1.astype(F32)
    h2 = r1.astype(BF16)
    h3 = (r1 - h2.astype(F32)).astype(BF16)
    return h1, h2, h3


def _dot(a, b):
    return jnp.dot(a, b, preferred_element_type=F32)


def _dot_nt(a, b):
    return lax.dot_general(a, b, (((1,), (1,)), ((), ())), preferred_element_type=F32)


def _dot_tn(a, b):
    return lax.dot_general(a, b, (((0,), (0,)), ((), ())), preferred_element_type=F32)


PROJ_BN = 512
N_ALIGNED_TILES = O_FF // PROJ_BN


def _proj_kernel(x_ref, wa_ref, wb_ref, o_ref, w_scr):
    j = pl.program_id(0)
    first_row_tile = pl.program_id(1) == 0

    @pl.when(jnp.logical_and(first_row_tile, j < N_ALIGNED_TILES))
    def _():
        w_scr[...] = wa_ref[0]

    @pl.when(jnp.logical_and(first_row_tile, j >= N_ALIGNED_TILES))
    def _():
        ab = jnp.concatenate([wa_ref[0], wb_ref[0]], axis=1).astype(F32)
        width = PROJ_BN + LANES
        w_scr[...] = pltpu.roll(ab, width - N_HEADS, 1)[:, :PROJ_BN].astype(BF16)

    o_ref[...] = _dot(x_ref[...], w_scr[...]).astype(o_ref.dtype)


def _proj_main(xb, w_in, l, bm):
    m = xb.shape[0]
    bm = min(bm, m)
    lanes_per_tile = PROJ_BN // LANES
    return pl.pallas_call(
        _proj_kernel,
        out_shape=jax.ShapeDtypeStruct((m, MAIN_W), BF16),
        grid=(MAIN_W // PROJ_BN, m // bm),
        in_specs=[pl.BlockSpec((bm, D_MODEL), lambda j, i: (i, 0)),
                  pl.BlockSpec((1, D_MODEL, PROJ_BN), lambda j, i: (l, 0, j)),
                  pl.BlockSpec((1, D_MODEL, LANES), lambda j, i: (l, 0, lanes_per_tile * (j + 1)))],
        out_specs=pl.BlockSpec((bm, PROJ_BN), lambda j, i: (i, j)),
        scratch_shapes=[pltpu.VMEM((D_MODEL, PROJ_BN), BF16)],
        compiler_params=_cparams(("arbitrary", "arbitrary"), 48),
        name="proj_main",
    )(xb, w_in, w_in)


def _proj_small_kernel(x_ref, wf_ref, wg_ref, o_ref):
    lane = lax.broadcasted_iota(jnp.int32, (1, LANES), 1)
    w = jnp.where(lane < S_GA, wf_ref[0].astype(F32),
                  jnp.where(lane < S_GA + GLA_GATE_RANK, wg_ref[0].astype(F32), 0.0))
    o_ref[...] = _dot(x_ref[...], w.astype(BF16))


def _proj_small(xb, w_in, l, bm):
    m = xb.shape[0]
    bm = min(bm, m)
    assert O_FF % LANES == 0 and O_GA % LANES == S_GA
    return pl.pallas_call(
        _proj_small_kernel,
        out_shape=jax.ShapeDtypeStruct((m, LANES), F32),
        grid=(m // bm,),
        in_specs=[pl.BlockSpec((bm, D_MODEL), lambda i: (i, 0)),
                  pl.BlockSpec((1, D_MODEL, LANES), lambda i: (l, 0, O_FF // LANES)),
                  pl.BlockSpec((1, D_MODEL, LANES), lambda i: (l, 0, O_GA // LANES))],
        out_specs=pl.BlockSpec((bm, LANES), lambda i: (i, 0)),
        compiler_params=_cparams(("arbitrary",)),
        name="proj_small",
    )(xb, w_in, w_in)


def _fcum_kernel(ps_ref, bias_ref, tri_ref, o_ref, carry_ref):
    @pl.when(pl.program_id(0) == 0)
    def _():
        carry_ref[...] = jnp.zeros_like(carry_ref)

    lf, _ = _log_sigmoid_parts(ps_ref[...] + bias_ref[...])
    tri = tri_ref[...]
    h1, h2, h3 = _split3(lf)
    c = _dot(tri, h1) + _dot(tri, h2) + _dot(tri, h3) + carry_ref[0:1, :]
    o_ref[...] = c
    carry_ref[...] = jnp.broadcast_to(c[-1:, :], carry_ref.shape)


def _forget_cumsum(ps, bias_row, tb):
    s = ps.shape[0]
    tb = min(tb, s)
    r = lax.broadcasted_iota(jnp.int32, (tb, tb), 0)
    c = lax.broadcasted_iota(jnp.int32, (tb, tb), 1)
    tri = (c <= r).astype(BF16)
    return pl.pallas_call(
        _fcum_kernel,
        out_shape=jax.ShapeDtypeStruct((s, LANES), F32),
        grid=(s // tb,),
        in_specs=[pl.BlockSpec((tb, LANES), lambda i: (i, 0)),
                  pl.BlockSpec((1, LANES), lambda i: (0, 0)),
                  pl.BlockSpec((tb, tb), lambda i: (0, 0))],
        out_specs=pl.BlockSpec((tb, LANES), lambda i: (i, 0)),
        scratch_shapes=[pltpu.VMEM((8, LANES), F32)],
        compiler_params=_cparams(("arbitrary",)),
        name="forget_cumsum",
    )(ps, bias_row, tri)


FOX_AUG = 2 * HEAD_DIM
LOG2E = 1.4426950408889634


def _fox_prep_kernel(q_ref, k_ref, v_ref, f_ref, qa_ref, ka_ref, va_ref, st_ref):
    tb = q_ref.shape[0]
    lane = lax.broadcasted_iota(jnp.int32, (tb, HEAD_DIM), 1)
    lane1 = lax.broadcasted_iota(jnp.int32, (1, LANES), 1)
    inv_scale = HEAD_DIM ** 0.5
    c1 = HEAD_DIM ** -0.5 * LOG2E
    stats = jnp.zeros((1, LANES), F32)

    def max_norm(x_ref, sl):
        x = x_ref[:, sl].astype(F32)
        n2 = jnp.sum(x * x, axis=-1, keepdims=True)
        return jnp.sqrt(jnp.max(n2, axis=0, keepdims=True))

    for h in range(N_HEADS):
        sl = slice(h * HEAD_DIM, (h + 1) * HEAD_DIM)
        stats = jnp.where(lane1 == h, c1 * max_norm(q_ref, sl), stats)
        stats = jnp.where(lane1 == N_HEADS + h, max_norm(k_ref, sl), stats)
        stats = jnp.where(lane1 == 2 * N_HEADS + h, f_ref[0:1, h:h + 1] * LOG2E, stats)
        stats = jnp.where(lane1 == 3 * N_HEADS + h, f_ref[tb - 1:tb, h:h + 1] * LOG2E, stats)
        f = f_ref[:, h:h + 1] * inv_scale
        h1, h2, h3 = (p.astype(F32) for p in _split3(f))
        aq = jnp.where(lane == 0, h1, jnp.where(lane == 1, h2, jnp.where(lane == 2, h3,
                       jnp.where(lane < 6, 1.0, 0.0)))).astype(BF16)
        ak = jnp.where(lane < 3, 1.0, jnp.where(lane == 3, -h1, jnp.where(lane == 4, -h2,
                       jnp.where(lane == 5, -h3, 0.0)))).astype(BF16)
        qa_ref[:, h * FOX_AUG:h * FOX_AUG + HEAD_DIM] = q_ref[:, sl]
        qa_ref[:, h * FOX_AUG + HEAD_DIM:(h + 1) * FOX_AUG] = aq
        ka_ref[:, h * FOX_AUG:h * FOX_AUG + HEAD_DIM] = k_ref[:, sl]
        ka_ref[:, h * FOX_AUG + HEAD_DIM:(h + 1) * FOX_AUG] = ak
        va_ref[:, h * FOX_AUG:h * FOX_AUG + HEAD_DIM] = v_ref[:, sl]
        va_ref[:, h * FOX_AUG + HEAD_DIM:(h + 1) * FOX_AUG] = jnp.where(lane == 0, 1.0, 0.0).astype(BF16)
    st_ref[...] = jnp.broadcast_to(stats, st_ref.shape)


def _fox_prep(pm, fcol, tb):
    s = pm.shape[0]
    tb = min(tb, s)
    blk = lambda c: pl.BlockSpec((tb, BRANCH_W), lambda i, c=c: (i, c // BRANCH_W))
    out = pl.BlockSpec((tb, N_HEADS * FOX_AUG), lambda i: (i, 0))
    shp = jax.ShapeDtypeStruct((s, N_HEADS * FOX_AUG), BF16)
    return pl.pallas_call(
        _fox_prep_kernel,
        out_shape=(shp, shp, shp, jax.ShapeDtypeStruct((s // tb * 8, LANES), F32)),
        grid=(s // tb,),
        in_specs=[blk(C_FQ), blk(C_FK), blk(C_FV), pl.BlockSpec((tb, LANES), lambda i: (i, 0))],
        out_specs=(out, out, out, pl.BlockSpec((8, LANES), lambda i: (i, 0))),
        compiler_params=_cparams(("arbitrary",)),
        name="fox_prep",
    )(pm, pm, pm, fcol)


FOX_STATS = 4 * N_HEADS
FOX_SKIP = -160.0


def _fox_kernel(qi_ref, ki_ref, st_ref, q_ref, k_ref, v_ref, o_ref, m_ref, acc_ref, *, t):
    p_id = pl.program_id(0)
    qi = qi_ref[p_id]
    ki = ki_ref[p_id]
    c1 = HEAD_DIM ** -0.5 * LOG2E

    def block_is_dead():
        bound = jnp.float32(-jnp.inf)
        m_min = jnp.full((1, 1), jnp.inf, F32)
        for h in range(N_HEADS):
            qb, kb = qi * FOX_STATS + h, ki * FOX_STATS + h
            bound = jnp.maximum(bound, st_ref[qb] * st_ref[kb + N_HEADS]
                                + st_ref[qb + 2 * N_HEADS] - st_ref[kb + 3 * N_HEADS])
            m_min = jnp.minimum(m_min, jnp.min(m_ref[h], axis=0, keepdims=True))
        return bound - m_min[0, 0] < FOX_SKIP

    def step(masked):
        if masked:
            row = lax.broadcasted_iota(jnp.int32, (t, t), 0)
            col = lax.broadcasted_iota(jnp.int32, (t, t), 1)
            keep = col <= row
        for h in range(N_HEADS):
            sl = slice(h * FOX_AUG, (h + 1) * FOX_AUG)
            s = _dot_nt(q_ref[:, sl], k_ref[:, sl]) * c1
            if masked:
                s = jnp.where(keep, s, NEG_BIG)
            m_prev = m_ref[h]
            m_new = jnp.maximum(m_prev, jnp.max(s, axis=-1, keepdims=True))
            alpha = jnp.exp2(m_prev - m_new)
            p = jnp.exp2(s - m_new)
            acc_ref[h] = alpha * acc_ref[h] + _dot(p.astype(BF16), v_ref[:, sl])
            m_ref[h] = m_new

    @pl.when(ki == qi)
    def _():
        m_ref[...] = jnp.full_like(m_ref, NEG_BIG)
        acc_ref[...] = jnp.zeros_like(acc_ref)
        step(True)

    @pl.when(ki != qi)
    def _():
        @pl.when(jnp.logical_not(block_is_dead()))
        def _():
            step(False)

    @pl.when(ki == 0)
    def _():
        for h in range(N_HEADS):
            acc = acc_ref[h]
            o_ref[:, h * HEAD_DIM:(h + 1) * HEAD_DIM] = (
                acc[:, :HEAD_DIM] / acc[:, HEAD_DIM:HEAD_DIM + 1]).astype(o_ref.dtype)


def _fox_attention(pm, fcol, t):
    s = pm.shape[0]
    t = min(t, s)
    nb = s // t
    qa, ka, va, st = _fox_prep(pm, fcol, t)
    stats = st.reshape(nb, 8, LANES)[:, 0, :FOX_STATS].reshape(nb * FOX_STATS)
    pairs = [(i, j) for i in range(nb) for j in range(i, -1, -1)]
    qi_tab = jnp.asarray([p[0] for p in pairs], jnp.int32)
    ki_tab = jnp.asarray([p[1] for p in pairs], jnp.int32)
    w = N_HEADS * FOX_AUG
    grid_spec = pltpu.PrefetchScalarGridSpec(
        num_scalar_prefetch=3,
        grid=(len(pairs),),
        in_specs=[pl.BlockSpec((t, w), lambda p, qi, ki, st: (qi[p], 0)),
                  pl.BlockSpec((t, w), lambda p, qi, ki, st: (ki[p], 0)),
                  pl.BlockSpec((t, w), lambda p, qi, ki, st: (ki[p], 0))],
        out_specs=pl.BlockSpec((t, BRANCH_W), lambda p, qi, ki, st: (qi[p], 0)),
        scratch_shapes=[pltpu.VMEM((N_HEADS, t, 1), F32),
                        pltpu.VMEM((N_HEADS, t, FOX_AUG), F32)],
    )
    return pl.pallas_call(
        functools.partial(_fox_kernel, t=t),
        out_shape=jax.ShapeDtypeStruct((s, BRANCH_W), BF16),
        grid_spec=grid_spec,
        compiler_params=_cparams(("arbitrary",), 48),
        name="fox_attention",
    )(qi_tab, ki_tab, stats, qa, ka, va)


def _sb_kernel(q_ref, k_ref, v_ref, gt_ref, o_ref, r_ref, acc_ref, *, t):
    i = pl.program_id(0)
    r_ref[...] = jnp.zeros_like(r_ref)
    acc_ref[...] = jnp.zeros_like(acc_ref)
    scale = HEAD_DIM ** -0.5
    gt = gt_ref[...]

    def block(kb, diagonal):
        k0 = pl.multiple_of(kb * t, t)
        if diagonal:
            keep = (lax.broadcasted_iota(jnp.int32, (t, t), 1)
                    < lax.broadcasted_iota(jnp.int32, (t, t), 0))
        rmax = jnp.full((1, 1), -jnp.inf, F32)
        for h in range(N_HEADS):
            sl = slice(h * HEAD_DIM, (h + 1) * HEAD_DIM)
            z = _dot_nt(q_ref[:, sl], k_ref[pl.ds(k0, t), sl]) * scale
            lsg, lom = _log_sigmoid_parts(z)
            if diagonal:
                lom = jnp.where(keep, lom, 0.0)
            r_prev = r_ref[h]
            after = _dot(lom.astype(BF16), gt) + r_prev
            a = jnp.exp(lsg + after)
            if diagonal:
                a = jnp.where(keep, a, 0.0)
            acc_ref[h] += _dot(a.astype(BF16), v_ref[pl.ds(k0, t), sl])
            r_new = r_prev + jnp.sum(lom, axis=-1, keepdims=True)
            r_ref[h] = r_new
            rmax = jnp.maximum(rmax, jnp.max(r_new, axis=0, keepdims=True))
        return rmax[0, 0]

    def cond(c):
        jj, rmax = c
        return jnp.logical_and(jj <= i, rmax > SB_EXIT)

    def body(c):
        jj, _ = c
        return jj + 1, block(i - jj, False)

    lax.while_loop(cond, body, (jnp.int32(1), block(i, True)))
    for h in range(N_HEADS):
        sl = slice(h * HEAD_DIM, (h + 1) * HEAD_DIM)
        o_ref[:, sl] = acc_ref[h].astype(o_ref.dtype)


def _sb_attention(pm, t):
    s = pm.shape[0]
    t = min(t, s)
    r = lax.broadcasted_iota(jnp.int32, (t, t), 0)
    c = lax.broadcasted_iota(jnp.int32, (t, t), 1)
    gt = (r > c).astype(BF16)
    return pl.pallas_call(
        functools.partial(_sb_kernel, t=t),
        out_shape=jax.ShapeDtypeStruct((s, BRANCH_W), BF16),
        grid=(s // t,),
        in_specs=[pl.BlockSpec((t, BRANCH_W), lambda i: (i, C_SQ // BRANCH_W)),
                  pl.BlockSpec((s, BRANCH_W), lambda i: (0, C_SK // BRANCH_W)),
                  pl.BlockSpec((s, BRANCH_W), lambda i: (0, C_SV // BRANCH_W)),
                  pl.BlockSpec((t, t), lambda i: (0, 0))],
        out_specs=pl.BlockSpec((t, BRANCH_W), lambda i: (i, 0)),
        scratch_shapes=[pltpu.VMEM((N_HEADS, t, 1), F32),
                        pltpu.VMEM((N_HEADS, t, HEAD_DIM), F32)],
        compiler_params=_cparams(("arbitrary",), 48),
        name="sb_attention",
    )(pm, pm, pm, gt)


def _ret_kernel(q_ref, k_ref, v_ref, g_ref, cos_ref, sin_ref, o_ref, st_ref, *, tb):
    @pl.when(pl.program_id(0) == 0)
    def _():
        st_ref[...] = jnp.zeros_like(st_ref)

    cos = cos_ref[...]
    sin = sin_ref[...]
    ri = lax.broadcasted_iota(jnp.int32, (tb, tb), 0)
    ci = lax.broadcasted_iota(jnp.int32, (tb, tb), 1)
    chunk_ok = (ci // CHUNK) <= (ri // CHUNK)
    dist = jnp.abs(ri - ci).astype(F32)
    idx = lax.broadcasted_iota(jnp.int32, (tb, 1), 0).astype(F32)
    scale = HEAD_DIM ** -0.5
    for h in range(N_HEADS):
        sl = slice(h * HEAD_DIM, (h + 1) * HEAD_DIM)
        lg = math.log(1.0 - 2.0 ** (-5.0 - h))
        q = q_ref[:, sl].astype(F32)
        k = k_ref[:, sl].astype(F32)
        v = v_ref[:, sl]
        qr = q * cos + pltpu.roll(q, HEAD_DIM // 2, 1) * sin
        kr = (k * cos + pltpu.roll(k, HEAD_DIM // 2, 1) * sin) * scale
        decay = jnp.where(chunk_ok, jnp.exp(lg * dist), 0.0)
        scores = _dot_nt(qr.astype(BF16), kr.astype(BF16)) * decay
        intra = _dot(scores.astype(BF16), v)
        q_dec = jnp.exp(lg * (idx + 1.0))
        k_dec = jnp.exp(lg * (tb - 1.0 - idx))
        state = st_ref[h]
        inter = _dot((qr * q_dec).astype(BF16), state.astype(BF16))
        kv = _dot_tn((kr * k_dec).astype(BF16), v)
        st_ref[h] = state * math.exp(lg * tb) + kv
        o = intra + inter
        mu = jnp.mean(o, axis=-1, keepdims=True)
        d = o - mu
        var = jnp.mean(d * d, axis=-1, keepdims=True)
        on = d * lax.rsqrt(var + LN_EPS)
        g = g_ref[:, sl].astype(F32)
        o_ref[:, sl] = (on * (g * jax.nn.sigmoid(g))).astype(o_ref.dtype)


def _retention(pm, cos_t, sin_t, tb):
    s = pm.shape[0]
    tb = min(tb, s)
    blk = lambda c: pl.BlockSpec((tb, BRANCH_W), lambda i, c=c: (i, c // BRANCH_W))
    return pl.pallas_call(
        functools.partial(_ret_kernel, tb=tb),
        out_shape=jax.ShapeDtypeStruct((s, BRANCH_W), BF16),
        grid=(s // tb,),
        in_specs=[blk(C_RQ), blk(C_RK), blk(C_RV), blk(C_RG),
                  pl.BlockSpec((tb, HEAD_DIM), lambda i: (i, 0)),
                  pl.BlockSpec((tb, HEAD_DIM), lambda i: (i, 0))],
        out_specs=pl.BlockSpec((tb, BRANCH_W), lambda i: (i, 0)),
        scratch_shapes=[pltpu.VMEM((N_HEADS, HEAD_DIM, HEAD_DIM), F32)],
        compiler_params=_cparams(("arbitrary",)),
        name="retention",
    )(pm, pm, pm, pm, cos_t, sin_t)


def _gla_kernel(q_ref, k_ref, v_ref, g_ref, ps_ref, up_ref, gb_ref, bd_ref, o_ref, st_ref, *, tb):
    @pl.when(pl.program_id(0) == 0)
    def _():
        st_ref[...] = jnp.zeros_like(st_ref)

    pre = _dot(ps_ref[...].astype(BF16), up_ref[...].astype(BF16)) + gb_ref[...]
    la, _ = _log_sigmoid_parts(pre)
    la = la / GLA_GATE_NORM
    bd = bd_ref[...]
    h1, h2, h3 = _split3(la)
    b = _dot(bd, h1) + _dot(bd, h2) + _dot(bd, h3)
    eb = jnp.exp(b)
    ieb = jnp.exp(-b)
    scale = GLA_DK ** -0.5
    q = q_ref[...].astype(F32) * scale
    k = k_ref[...].astype(F32)
    qe = q * eb
    qi = q * ieb
    ke = (k * ieb).astype(BF16)
    kf = (k * eb).astype(BF16)
    lane = lax.broadcasted_iota(jnp.int32, (1, GLA_KW), 1)
    ri = lax.broadcasted_iota(jnp.int32, (CHUNK, CHUNK), 0)
    ci = lax.broadcasted_iota(jnp.int32, (CHUNK, CHUNK), 1)
    causal = ri >= ci
    for c in range(tb // CHUNK):
        rs = slice(c * CHUNK, (c + 1) * CHUNK)
        b_c = b[rs]
        b_last = b_c[CHUNK - 1:CHUNK, :]
        kd = k[rs] * jnp.exp(b_last - b_c)
        dec = jnp.exp(b_last)
        for h in range(N_HEADS):
            hm = jnp.logical_and(lane >= h * GLA_DK, lane < (h + 1) * GLA_DK)
            vs = slice(h * HEAD_DIM, (h + 1) * HEAD_DIM)
            qe_h = jnp.where(hm, qe[rs], 0.0).astype(BF16)
            qi_h = jnp.where(hm, qi[rs], 0.0).astype(BF16)
            s_past = _dot_nt(qe_h, ke[rs])
            s_future = _dot_nt(qi_h, kf[rs])
            scores = jnp.where(causal, s_past, s_future)
            v_c = v_ref[rs, vs]
            st = st_ref[h]
            o = _dot(scores.astype(BF16), v_c) + _dot_nt(qe_h, st.astype(BF16))
            kd_h = jnp.where(hm, kd, 0.0).astype(BF16)
            st_ref[h] = st * dec + _dot_tn(v_c, kd_h)
            on = o * lax.rsqrt(jnp.mean(o * o, axis=-1, keepdims=True) + LN_EPS)
            g = g_ref[rs, vs].astype(F32)
            o_ref[rs, vs] = (on * (g * jax.nn.sigmoid(g))).astype(o_ref.dtype)


def _gla(pm, ps, up_pad, gbias, tb):
    s = pm.shape[0]
    tb = min(tb, s)
    r = lax.broadcasted_iota(jnp.int32, (tb, tb), 0)
    c = lax.broadcasted_iota(jnp.int32, (tb, tb), 1)
    bd = jnp.logical_and(c <= r, (c // CHUNK) == (r // CHUNK)).astype(BF16)
    return pl.pallas_call(
        functools.partial(_gla_kernel, tb=tb),
        out_shape=jax.ShapeDtypeStruct((s, BRANCH_W), BF16),
        grid=(s // tb,),
        in_specs=[pl.BlockSpec((tb, GLA_KW), lambda i: (i, C_GQ // GLA_KW)),
                  pl.BlockSpec((tb, GLA_KW), lambda i: (i, C_GK // GLA_KW)),
                  pl.BlockSpec((tb, BRANCH_W), lambda i: (i, C_GV // BRANCH_W)),
                  pl.BlockSpec((tb, BRANCH_W), lambda i: (i, C_GR // BRANCH_W)),
                  pl.BlockSpec((tb, LANES), lambda i: (i, 0)),
                  pl.BlockSpec((LANES, GLA_KW), lambda i: (0, 0)),
                  pl.BlockSpec((1, GLA_KW), lambda i: (0, 0)),
                  pl.BlockSpec((tb, tb), lambda i: (0, 0))],
        out_specs=pl.BlockSpec((tb, BRANCH_W), lambda i: (i, 0)),
        scratch_shapes=[pltpu.VMEM((N_HEADS, HEAD_DIM, GLA_KW), F32)],
        compiler_params=_cparams(("arbitrary",)),
        name="gla",
    )(pm, pm, pm, pm, ps, up_pad, gbias, bd)


def _merge_kernel(x_ref, y0_ref, y1_ref, y2_ref, y3_ref, wg_ref, bg_ref, wb_ref, o_ref):
    x = x_ref[...]
    acc = None
    for n, y_ref in enumerate((y0_ref, y1_ref, y2_ref, y3_ref)):
        gate = jax.nn.sigmoid(_dot(x, wg_ref[0, n]) + bg_ref[n])
        term = gate * _dot(y_ref[...], wb_ref[0, n])
        acc = term if acc is None else acc + term
    o_ref[...] = acc.astype(o_ref.dtype)


def _merge(xb, ys, wg, bg, wb, l, bm, bn):
    m = xb.shape[0]
    bm = min(bm, m)
    yspec = pl.BlockSpec((bm, BRANCH_W), lambda j, i: (i, 0))
    return pl.pallas_call(
        _merge_kernel,
        out_shape=jax.ShapeDtypeStruct((m, D_MODEL), BF16),
        grid=(D_MODEL // bn, m // bm),
        in_specs=[pl.BlockSpec((bm, D_MODEL), lambda j, i: (i, 0)),
                  yspec, yspec, yspec, yspec,
                  pl.BlockSpec((1, N_BRANCH, D_MODEL, bn), lambda j, i: (l, 0, 0, j)),
                  pl.BlockSpec((N_BRANCH, 1, bn), lambda j, i: (0, 0, j)),
                  pl.BlockSpec((1, N_BRANCH, BRANCH_W, bn), lambda j, i: (l, 0, 0, j))],
        out_specs=pl.BlockSpec((bm, bn), lambda j, i: (i, j)),
        compiler_params=_cparams(("arbitrary", "arbitrary"), 52),
        name="merge",
    )(xb, *ys, wg, bg, wb)


def _layer_norm_rows(z, g, b):
    mu = jnp.mean(z, axis=-1, keepdims=True)
    d = z - mu
    var = jnp.mean(d * d, axis=-1, keepdims=True)
    return d * lax.rsqrt(var + LN_EPS) * g + b


def _outln_kernel(m_ref, w_ref, x_ref, g_ref, b_ref, o_ref, ob_ref, *, alpha):
    h = _dot(m_ref[...], w_ref[0])
    y = _layer_norm_rows(alpha * x_ref[...] + h, g_ref[...], b_ref[...])
    o_ref[...] = y
    ob_ref[...] = y.astype(BF16).reshape(ob_ref.shape)


def _out_ln(merged, w_out, l, x, g, b, alpha, bm):
    m = x.shape[0]
    bm = min(bm, m)
    row = pl.BlockSpec((bm, D_MODEL), lambda i: (i, 0))
    row3 = pl.BlockSpec((bm, ROW_TILES, LANES), lambda i: (i, 0, 0))
    vec = pl.BlockSpec((1, D_MODEL), lambda i: (0, 0))
    return pl.pallas_call(
        functools.partial(_outln_kernel, alpha=alpha),
        out_shape=(jax.ShapeDtypeStruct((m, D_MODEL), F32),
                   jax.ShapeDtypeStruct((m, ROW_TILES, LANES), BF16)),
        grid=(m // bm,),
        in_specs=[row, pl.BlockSpec((1, D_MODEL, D_MODEL), lambda i: (l, 0, 0)), row, vec, vec],
        out_specs=(row, row3),
        compiler_params=_cparams(("arbitrary",), 48),
        name="out_ln",
    )(merged, w_out, x, g, b)


def _router_kernel(x_ref, w_ref, b_ref, id_ref, wt_ref):
    x = x_ref[...]
    xh = x.astype(BF16)
    xl = (x - xh.astype(F32)).astype(BF16)
    w = w_ref[...]
    wh = w.astype(BF16)
    wl = (w - wh.astype(F32)).astype(BF16)
    logits = _dot(xh, wh) + _dot(xh, wl) + _dot(xl, wh) + b_ref[...]
    lane = lax.broadcasted_iota(jnp.int32, logits.shape, 1)
    neg = -jnp.inf
    gl = jnp.where(lane < N_GROUPS, logits, neg)
    gmax = jnp.max(gl, axis=-1, keepdims=True)
    g_sel = jnp.min(jnp.where(gl == gmax, lane, LANES), axis=-1, keepdims=True)
    g_w = 1.0 / jnp.sum(jnp.where(lane < N_GROUPS, jnp.exp(logits - gmax), 0.0), axis=-1, keepdims=True)
    lo = N_GROUPS + g_sel * EXPERTS_PER_GROUP
    el = jnp.where(jnp.logical_and(lane >= lo, lane < lo + EXPERTS_PER_GROUP), logits, neg)
    v1 = jnp.max(el, axis=-1, keepdims=True)
    i1 = jnp.min(jnp.where(el == v1, lane, LANES), axis=-1, keepdims=True)
    el2 = jnp.where(lane == i1, neg, el)
    v2 = jnp.max(el2, axis=-1, keepdims=True)
    i2 = jnp.min(jnp.where(el2 == v2, lane, LANES), axis=-1, keepdims=True)
    e2 = jnp.exp(v2 - v1)
    p1 = 1.0 / (1.0 + e2)
    p2 = e2 / (1.0 + e2)
    id_ref[...] = jnp.where(lane == 0, i1 - N_GROUPS, jnp.where(lane == 1, i2 - N_GROUPS, 0))
    wt_ref[...] = jnp.where(lane == 0, p1 * g_w, jnp.where(lane == 1, p2 * g_w, 0.0))


def _router(x1, w_r, b_r, bm):
    m = x1.shape[0]
    bm = min(bm, m)
    return pl.pallas_call(
        _router_kernel,
        out_shape=(jax.ShapeDtypeStruct((m, LANES), jnp.int32), jax.ShapeDtypeStruct((m, LANES), F32)),
        grid=(m // bm,),
        in_specs=[pl.BlockSpec((bm, D_MODEL), lambda i: (i, 0)),
                  pl.BlockSpec((D_MODEL, LANES), lambda i: (0, 0)),
                  pl.BlockSpec((1, LANES), lambda i: (0, 0))],
        out_specs=(pl.BlockSpec((bm, LANES), lambda i: (i, 0)), pl.BlockSpec((bm, LANES), lambda i: (i, 0))),
        compiler_params=_cparams(("arbitrary",)),
        name="router",
    )(x1, w_r, b_r)


def _start_row_gather(idx_ref, first, stride, n, src_ref, dst_ref, sem, both_queues=False):
    def copy(r):
        return pltpu.make_async_copy(src_ref.at[idx_ref[first + stride * r]], dst_ref.at[r], sem)

    if both_queues:
        def start2(r2, c):
            copy(2 * r2).start(priority=0)
            copy(2 * r2 + 1).start(priority=1)
            return c
        lax.fori_loop(0, n // 2, start2, 0, unroll=4)
    else:
        def start(r, c):
            copy(r).start()
            return c
        lax.fori_loop(0, n, start, 0, unroll=8)


def _wait_row_gather(n, src_ref, dst_ref, sem):
    pltpu.make_async_copy(src_ref.at[pl.ds(0, n)], dst_ref, sem).wait()


ROW_LOOKAHEAD = 2
ROW_SLOTS = ROW_LOOKAHEAD + 1


def _expert_kernel(tok_ref, be_ref, nv_ref, run_ref, nxt_ref, x_ref, w1_ref, w3_ref, w2_ref, o_ref,
                   xbuf, sem, w1s, w3s, w2s, wsem, w1b, w3b, w2b, *, tm, l):
    i = pl.program_id(0)
    n_valid = nv_ref[0]
    e = be_ref[i]
    prev = be_ref[jnp.maximum(i - 1, 0)]
    slot = lax.rem(i, ROW_SLOTS)
    wslot = lax.rem(run_ref[i], 2)
    run_start = jnp.logical_and(i < n_valid, jnp.logical_or(i == 0, e != prev))

    def weight_copies(expert, s):
        return (pltpu.make_async_copy(w1_ref.at[l, expert], w1s.at[s], wsem.at[s]),
                pltpu.make_async_copy(w3_ref.at[l, expert], w3s.at[s], wsem.at[s]),
                pltpu.make_async_copy(w2_ref.at[l, expert], w2s.at[s], wsem.at[s]))

    @pl.when(i == 0)
    def _():
        for c in weight_copies(e, 0):
            c.start(priority=1)
        for b in range(ROW_LOOKAHEAD):
            @pl.when(b < n_valid)
            def _(b=b):
                _start_row_gather(tok_ref, b * tm, 1, tm, x_ref, xbuf.at[b], sem.at[b])

    @pl.when(i + ROW_LOOKAHEAD < n_valid)
    def _():
        ahead = lax.rem(i + ROW_LOOKAHEAD, ROW_SLOTS)
        _start_row_gather(tok_ref, (i + ROW_LOOKAHEAD) * tm, 1, tm, x_ref, xbuf.at[ahead], sem.at[ahead])

    @pl.when(run_start)
    def _():
        for c in weight_copies(e, wslot):
            c.wait()
        nxt = nxt_ref[i]

        @pl.when(nxt >= 0)
        def _():
            for c in weight_copies(nxt, 1 - wslot):
                c.start(priority=1)

        w1b[...] = w1s[wslot].astype(BF16)
        w3b[...] = w3s[wslot].astype(BF16)
        w2b[...] = w2s[wslot].astype(BF16)

    @pl.when(i < n_valid)
    def _():
        _wait_row_gather(tm, x_ref, xbuf.at[slot], sem.at[slot])
        x = xbuf[slot].reshape(tm, D_MODEL)
        a = _dot(x, w1b[...])
        h = (a * jax.nn.sigmoid(a)) * _dot(x, w3b[...])
        y = _dot(h.astype(BF16), w2b[...])
        o_ref[...] = y.astype(BF16).reshape(o_ref.shape)

    @pl.when(i >= n_valid)
    def _():
        o_ref[...] = jnp.zeros_like(o_ref)


def _experts(plan, x3, w1, w3, w2, l, tm):
    n = plan["row_tok"].shape[0]
    any_spec = pl.BlockSpec(memory_space=pl.ANY)
    grid_spec = pltpu.PrefetchScalarGridSpec(
        num_scalar_prefetch=5,
        grid=(n // tm,),
        in_specs=[any_spec, any_spec, any_spec, any_spec],
        out_specs=pl.BlockSpec((tm, ROW_TILES, LANES), lambda i, *_: (i, 0, 0)),
        scratch_shapes=[pltpu.VMEM((ROW_SLOTS, tm, ROW_TILES, LANES), BF16),
                        pltpu.SemaphoreType.DMA((ROW_SLOTS,)),
                        pltpu.VMEM((2, D_MODEL, D_EXPERT), F32),
                        pltpu.VMEM((2, D_MODEL, D_EXPERT), F32),
                        pltpu.VMEM((2, D_EXPERT, D_MODEL), F32),
                        pltpu.SemaphoreType.DMA((2,)),
                        pltpu.VMEM((D_MODEL, D_EXPERT), BF16),
                        pltpu.VMEM((D_MODEL, D_EXPERT), BF16),
                        pltpu.VMEM((D_EXPERT, D_MODEL), BF16)],
    )
    return pl.pallas_call(
        functools.partial(_expert_kernel, tm=tm, l=l),
        out_shape=jax.ShapeDtypeStruct((n, ROW_TILES, LANES), BF16),
        grid_spec=grid_spec,
        compiler_params=_cparams(("arbitrary",), 52),
        name="experts",
    )(plan["row_tok"], plan["blk_e"], plan["n_valid"], plan["run_id"], plan["next_e"], x3, w1, w3, w2)


def _combine_kernel(pos_ref, yb_ref, x_ref, wt_ref, g_ref, b_ref, o_ref, ob_ref, buf, sem, *, tm, alpha):
    i = pl.program_id(0)
    slot = lax.rem(i, 2)

    def start_tile(tile, s):
        for k in range(TOP_K):
            _start_row_gather(pos_ref, TOP_K * tile * tm + k, TOP_K, tm, yb_ref, buf.at[s, k], sem.at[s],
                              both_queues=True)

    @pl.when(i == 0)
    def _():
        start_tile(0, 0)

    @pl.when(i + 1 < pl.num_programs(0))
    def _():
        start_tile(i + 1, 1 - slot)

    for k in range(TOP_K):
        _wait_row_gather(tm, yb_ref, buf.at[slot, k], sem.at[slot])
    wt = wt_ref[...]
    h = None
    for k in range(TOP_K):
        term = buf[slot, k].reshape(tm, D_MODEL).astype(F32) * wt[:, k:k + 1]
        h = term if h is None else h + term
    y = _layer_norm_rows(alpha * x_ref[...] + h, g_ref[...], b_ref[...])
    o_ref[...] = y
    ob_ref[...] = y.astype(BF16)


def _combine_ln(pos, yb, x1, wts, g, b, alpha, tm):
    m = x1.shape[0]
    tm = min(tm, m)
    row = lambda i, pos: (i, 0)
    grid_spec = pltpu.PrefetchScalarGridSpec(
        num_scalar_prefetch=1,
        grid=(m // tm,),
        in_specs=[pl.BlockSpec(memory_space=pl.ANY),
                  pl.BlockSpec((tm, D_MODEL), row),
                  pl.BlockSpec((tm, LANES), row),
                  pl.BlockSpec((1, D_MODEL), lambda i, pos: (0, 0)),
                  pl.BlockSpec((1, D_MODEL), lambda i, pos: (0, 0))],
        out_specs=(pl.BlockSpec((tm, D_MODEL), row), pl.BlockSpec((tm, D_MODEL), row)),
        scratch_shapes=[pltpu.VMEM((2, TOP_K, tm, ROW_TILES, LANES), BF16), pltpu.SemaphoreType.DMA((2,))],
    )
    return pl.pallas_call(
        functools.partial(_combine_kernel, tm=tm, alpha=alpha),
        out_shape=(jax.ShapeDtypeStruct((m, D_MODEL), F32), jax.ShapeDtypeStruct((m, D_MODEL), BF16)),
        grid_spec=grid_spec,
        compiler_params=_cparams(("arbitrary",)),
        name="combine_ln",
    )(pos, yb, x1, wts, g, b)


def _dispatch_plan(ids, tm):
    t = ids.shape[0]
    flat_e = ids.reshape(-1)
    n_assign = flat_e.shape[0]
    onehot = (flat_e[:, None] == jnp.arange(N_EXPERTS, dtype=jnp.int32)[None, :]).astype(jnp.int32)
    csum = jnp.cumsum(onehot, axis=0)
    rank = jnp.sum((csum - onehot) * onehot, axis=1)
    counts = csum[-1]
    padded = (counts + tm - 1) // tm * tm
    pend = jnp.cumsum(padded)
    pstart = pend - padded
    dest = jnp.sum(onehot * pstart[None, :], axis=1) + rank
    n_rows = n_assign + N_EXPERTS * tm
    n_blocks = n_rows // tm
    flat_tok = jnp.arange(n_assign, dtype=jnp.int32) // TOP_K
    row_tok = jnp.zeros((n_rows,), jnp.int32).at[dest].set(flat_tok)
    blk_start = jnp.arange(n_blocks, dtype=jnp.int32) * tm
    blk_e = jnp.minimum(jnp.sum((pend[None, :] <= blk_start[:, None]).astype(jnp.int32), axis=1),
                        N_EXPERTS - 1).astype(jnp.int32)
    n_valid = (pend[-1] // tm).astype(jnp.int32)
    blk = jnp.arange(n_blocks, dtype=jnp.int32)
    is_start = jnp.logical_and(blk < n_valid, jnp.logical_or(blk == 0, blk_e != jnp.roll(blk_e, 1)))
    run_id = jnp.cumsum(is_start.astype(jnp.int32)) - 1
    start_pos = jnp.where(is_start, blk, n_blocks)
    next_start = lax.cummin(jnp.concatenate([start_pos[1:], jnp.full((1,), n_blocks, jnp.int32)]), reverse=True)
    next_e = jnp.where(next_start < n_blocks, blk_e[jnp.minimum(next_start, n_blocks - 1)], -1)
    return dict(row_tok=row_tok, blk_e=blk_e, n_valid=n_valid.reshape(1), run_id=run_id.astype(jnp.int32),
                next_e=next_e.astype(jnp.int32), dest=dest.astype(jnp.int32))


def _rope_tables(s):
    half = HEAD_DIM // 2
    inv = ROPE_BASE ** (-jnp.arange(half, dtype=F32) / half)
    ang = jnp.arange(s, dtype=F32)[:, None] * inv[None, :]
    cos, sin = jnp.cos(ang), jnp.sin(ang)
    return jnp.concatenate([cos, cos], axis=1), jnp.concatenate([-sin, sin], axis=1)


def kernel(x, w_in, fox_forget_bias, gla_gate_up, gla_gate_bias, w_branch, w_gate, b_gate, w_out, ln1_g, ln1_b, w_group, b_group, w_expert_router, b_expert_router, w1, w3, w2, ln2_g, ln2_b):
    bsz, s, d = x.shape
    depth = w_in.shape[0]
    alpha = DEEPNORM_ALPHA
    assert bsz == 1 and d == D_MODEL and w_in.shape[2] == IN_WIDTH
    t = bsz * s
    tm_moe = 256

    w_in_b = w_in.astype(BF16)
    wg_b = w_gate.astype(BF16)
    wb_b = w_branch.astype(BF16)
    wo_b = w_out.astype(BF16)
    fbias = jnp.pad(fox_forget_bias, ((0, 0), (0, LANES - N_HEADS)))[:, None, :]
    up_pad = jnp.pad(gla_gate_up, ((0, 0), (S_GA, LANES - S_GA - GLA_GATE_RANK), (0, 0)))
    w_r = jnp.pad(jnp.concatenate([w_group, w_expert_router], axis=2),
                  ((0, 0), (0, 0), (0, LANES - N_GROUPS - N_EXPERTS)))
    b_r = jnp.pad(jnp.concatenate([b_group, b_expert_router], axis=1),
                  ((0, 0), (0, LANES - N_GROUPS - N_EXPERTS)))[:, None, :]
    cos_t, sin_t = _rope_tables(s)

    xf = x.reshape(t, d)
    xb = xf.astype(BF16)
    for l in range(depth):
        pm = _proj_main(xb, w_in_b, l, 2048)
        ps = _proj_small(xb, w_in_b, l, 1024)
        fcol = _forget_cumsum(ps, fbias[l], 512)
        y_fox = _fox_attention(pm, fcol, 512)
        y_ret = _retention(pm, cos_t, sin_t, 256)
        y_sb = _sb_attention(pm, 256)
        y_gla = _gla(pm, ps, up_pad[l], gla_gate_bias[l][None, :], 256)
        merged = _merge(xb, (y_fox, y_ret, y_sb, y_gla), wg_b, b_gate[l][:, None, :], wb_b, l, 512, 512)
        x1, x1b3 = _out_ln(merged, wo_b, l, xf, ln1_g[l][None, :], ln1_b[l][None, :], alpha, 512)

        ids, wts = _router(x1, w_r[l], b_r[l], 512)
        plan = _dispatch_plan(ids[:, :TOP_K], tm_moe)
        yb3 = _experts(plan, x1b3, w1, w3, w2, l, tm_moe)
        xf, xb = _combine_ln(plan["dest"], yb3, x1, wts, ln2_g[l][None, :], ln2_b[l][None, :], alpha, 256)
    return xf.reshape(bsz, s, d)
```

```python
import functools
import math

import jax
import jax.numpy as jnp
from jax import lax
from jax.experimental import pallas as pl
from jax.experimental.pallas import tpu as pltpu

F32 = jnp.float32
BF16 = jnp.bfloat16

D_MODEL = 2048
CHUNK = 64
N_HEADS = 4
HEAD_DIM = 128
BRANCH_W = N_HEADS * HEAD_DIM
GLA_DK = 64
GLA_KW = N_HEADS * GLA_DK
GLA_GATE_RANK = 16
GLA_GATE_NORM = 16.0
N_BRANCH = 4
N_GROUPS = 4
EXPERTS_PER_GROUP = 8
N_EXPERTS = N_GROUPS * EXPERTS_PER_GROUP
TOP_K = 2
D_EXPERT = D_MODEL // 4
ROPE_BASE = 10000.0
LN_EPS = 1e-5
DEPTH = 4
DEEPNORM_ALPHA = (2.0 * DEPTH) ** 0.25

LANES = 128
ROW_TILES = D_MODEL // LANES
NEG_BIG = -1e30
SB_EXIT = -104.0

C_FQ, C_FK, C_FV = 0, 512, 1024
C_RQ, C_RK, C_RV, C_RG = 1536, 2048, 2560, 3072
C_SQ, C_SK, C_SV = 3584, 4096, 4608
C_GQ, C_GK, C_GV, C_GR = 5120, 5376, 5632, 6144
MAIN_W = 6656
O_FF, O_GA = 1536, 6660
IN_WIDTH = 6676
S_FF, S_GA = 0, 4


def _cparams(sem, vmem_mb=None):
    kw = dict(dimension_semantics=sem)
    if vmem_mb is not None:
        kw["vmem_limit_bytes"] = vmem_mb * 1024 * 1024
    return pltpu.CompilerParams(**kw)


def _log_sigmoid_parts(z):
    t = jnp.log(1.0 + jnp.exp(-jnp.abs(z)))
    return jnp.minimum(z, 0.0) - t, -(jnp.maximum(z, 0.0) + t)


def _split3(x):
    h1 = x.astype(BF16)
    r1 = x - h1.astype(F32)
    h2 = r1.astype(BF16)
    h3 = (r1 - h2.astype(F32)).astype(BF16)
    return h1, h2, h3


def _dot(a, b):
    return jnp.dot(a, b, preferred_element_type=F32)


def _dot_nt(a, b):
    return lax.dot_general(a, b, (((1,), (1,)), ((), ())), preferred_element_type=F32)


def _dot_tn(a, b):
    return lax.dot_general(a, b, (((0,), (0,)), ((), ())), preferred_element_type=F32)


PROJ_BN = 512
N_ALIGNED_TILES = O_FF // PROJ_BN


def _proj_kernel(x_ref, wa_ref, wb_ref, o_ref, w_scr):
    j = pl.program_id(0)
    first_row_tile = pl.program_id(1) == 0

    @pl.when(jnp.logical_and(first_row_tile, j < N_ALIGNED_TILES))
    def _():
        w_scr[...] = wa_ref[0]

    @pl.when(jnp.logical_and(first_row_tile, j >= N_ALIGNED_TILES))
    def _():
        ab = jnp.concatenate([wa_ref[0], wb_ref[0]], axis=1).astype(F32)
        width = PROJ_BN + LANES
        w_scr[...] = pltpu.roll(ab, width - N_HEADS, 1)[:, :PROJ_BN].astype(BF16)

    o_ref[...] = _dot(x_ref[...], w_scr[...]).astype(o_ref.dtype)


def _proj_main(xb, w_in, l, bm):
    m = xb.shape[0]
    bm = min(bm, m)
    lanes_per_tile = PROJ_BN // LANES
    return pl.pallas_call(
        _proj_kernel,
        out_shape=jax.ShapeDtypeStruct((m, MAIN_W), BF16),
        grid=(MAIN_W // PROJ_BN, m // bm),
        in_specs=[pl.BlockSpec((bm, D_MODEL), lambda j, i: (i, 0)),
                  pl.BlockSpec((1, D_MODEL, PROJ_BN), lambda j, i: (l, 0, j)),
                  pl.BlockSpec((1, D_MODEL, LANES), lambda j, i: (l, 0, lanes_per_tile * (j + 1)))],
        out_specs=pl.BlockSpec((bm, PROJ_BN), lambda j, i: (i, j)),
        scratch_shapes=[pltpu.VMEM((D_MODEL, PROJ_BN), BF16)],
        compiler_params=_cparams(("arbitrary", "arbitrary"), 48),
        name="proj_main",
    )(xb, w_in, w_in)


def _proj_small_kernel(x_ref, wf_ref, wg_ref, o_ref):
    lane = lax.broadcasted_iota(jnp.int32, (1, LANES), 1)
    w = jnp.where(lane < S_GA, wf_ref[0].astype(F32),
                  jnp.where(lane < S_GA + GLA_GATE_RANK, wg_ref[0].astype(F32), 0.0))
    o_ref[...] = _dot(x_ref[...], w.astype(BF16))


def _proj_small(xb, w_in, l, bm):
    m = xb.shape[0]
    bm = min(bm, m)
    assert O_FF % LANES == 0 and O_GA % LANES == S_GA
    return pl.pallas_call(
        _proj_small_kernel,
        out_shape=jax.ShapeDtypeStruct((m, LANES), F32),
        grid=(m // bm,),
        in_specs=[pl.BlockSpec((bm, D_MODEL), lambda i: (i, 0)),
                  pl.BlockSpec((1, D_MODEL, LANES), lambda i: (l, 0, O_FF // LANES)),
                  pl.BlockSpec((1, D_MODEL, LANES), lambda i: (l, 0, O_GA // LANES))],
        out_specs=pl.BlockSpec((bm, LANES), lambda i: (i, 0)),
        compiler_params=_cparams(("arbitrary",)),
        name="proj_small",
    )(xb, w_in, w_in)


def _fcum_kernel(ps_ref, bias_ref, tri_ref, o_ref, carry_ref):
    @pl.when(pl.program_id(0) == 0)
    def _():
        carry_ref[...] = jnp.zeros_like(carry_ref)

    lf, _ = _log_sigmoid_parts(ps_ref[...] + bias_ref[...])
    tri = tri_ref[...]
    h1, h2, h3 = _split3(lf)
    c = _dot(tri, h1) + _dot(tri, h2) + _dot(tri, h3) + carry_ref[0:1, :]
    o_ref[...] = c
    carry_ref[...] = jnp.broadcast_to(c[-1:, :], carry_ref.shape)


def _forget_cumsum(ps, bias_row, tb):
    s = ps.shape[0]
    tb = min(tb, s)
    r = lax.broadcasted_iota(jnp.int32, (tb, tb), 0)
    c = lax.broadcasted_iota(jnp.int32, (tb, tb), 1)
    tri = (c <= r).astype(BF16)
    return pl.pallas_call(
        _fcum_kernel,
        out_shape=jax.ShapeDtypeStruct((s, LANES), F32),
        grid=(s // tb,),
        in_specs=[pl.BlockSpec((tb, LANES), lambda i: (i, 0)),
                  pl.BlockSpec((1, LANES), lambda i: (0, 0)),
                  pl.BlockSpec((tb, tb), lambda i: (0, 0))],
        out_specs=pl.BlockSpec((tb, LANES), lambda i: (i, 0)),
        scratch_shapes=[pltpu.VMEM((8, LANES), F32)],
        compiler_params=_cparams(("arbitrary",)),
        name="forget_cumsum",
    )(ps, bias_row, tri)


FOX_AUG = 2 * HEAD_DIM
LOG2E = 1.4426950408889634


def _fox_prep_kernel(q_ref, k_ref, v_ref, f_ref, qa_ref, ka_ref, va_ref, st_ref):
    tb = q_ref.shape[0]
    lane = lax.broadcasted_iota(jnp.int32, (tb, HEAD_DIM), 1)
    lane1 = lax.broadcasted_iota(jnp.int32, (1, LANES), 1)
    inv_scale = HEAD_DIM ** 0.5
    c1 = HEAD_DIM ** -0.5 * LOG2E
    stats = jnp.zeros((1, LANES), F32)

    def max_norm(x_ref, sl):
        x = x_ref[:, sl].astype(F32)
        n2 = jnp.sum(x * x, axis=-1, keepdims=True)
        return jnp.sqrt(jnp.max(n2, axis=0, keepdims=True))

    for h in range(N_HEADS):
        sl = slice(h * HEAD_DIM, (h + 1) * HEAD_DIM)
        stats = jnp.where(lane1 == h, c1 * max_norm(q_ref, sl), stats)
        stats = jnp.where(lane1 == N_HEADS + h, max_norm(k_ref, sl), stats)
        stats = jnp.where(lane1 == 2 * N_HEADS + h, f_ref[0:1, h:h + 1] * LOG2E, stats)
        stats = jnp.where(lane1 == 3 * N_HEADS + h, f_ref[tb - 1:tb, h:h + 1] * LOG2E, stats)
        f = f_ref[:, h:h + 1] * inv_scale
        h1, h2, h3 = (p.astype(F32) for p in _split3(f))
        aq = jnp.where(lane == 0, h1, jnp.where(lane == 1, h2, jnp.where(lane == 2, h3,
                       jnp.where(lane < 6, 1.0, 0.0)))).astype(BF16)
        ak = jnp.where(lane < 3, 1.0, jnp.where(lane == 3, -h1, jnp.where(lane == 4, -h2,
                       jnp.where(lane == 5, -h3, 0.0)))).astype(BF16)
        qa_ref[:, h * FOX_AUG:h * FOX_AUG + HEAD_DIM] = q_ref[:, sl]
        qa_ref[:, h * FOX_AUG + HEAD_DIM:(h + 1) * FOX_AUG] = aq
        ka_ref[:, h * FOX_AUG:h * FOX_AUG + HEAD_DIM] = k_ref[:, sl]
        ka_ref[:, h * FOX_AUG + HEAD_DIM:(h + 1) * FOX_AUG] = ak
        va_ref[:, h * FOX_AUG:h * FOX_AUG + HEAD_DIM] = v_ref[:, sl]
        va_ref[:, h * FOX_AUG + HEAD_DIM:(h + 1) * FOX_AUG] = jnp.where(lane == 0, 1.0, 0.0).astype(BF16)
    st_ref[...] = jnp.broadcast_to(stats, st_ref.shape)


def _fox_prep(pm, fcol, tb):
    s = pm.shape[0]
    tb = min(tb, s)
    blk = lambda c: pl.BlockSpec((tb, BRANCH_W), lambda i, c=c: (i, c // BRANCH_W))
    out = pl.BlockSpec((tb, N_HEADS * FOX_AUG), lambda i: (i, 0))
    shp = jax.ShapeDtypeStruct((s, N_HEADS * FOX_AUG), BF16)
    return pl.pallas_call(
        _fox_prep_kernel,
        out_shape=(shp, shp, shp, jax.ShapeDtypeStruct((s // tb * 8, LANES), F32)),
        grid=(s // tb,),
        in_specs=[blk(C_FQ), blk(C_FK), blk(C_FV), pl.BlockSpec((tb, LANES), lambda i: (i, 0))],
        out_specs=(out, out, out, pl.BlockSpec((8, LANES), lambda i: (i, 0))),
        compiler_params=_cparams(("arbitrary",)),
        name="fox_prep",
    )(pm, pm, pm, fcol)


FOX_STATS = 4 * N_HEADS
FOX_SKIP = -160.0


def _fox_kernel(qi_ref, kj_ref, st_ref, q_ref, k_ref, v_ref, o_ref, m_ref, acc_ref, *, tq, tk):
    r = tk // tq
    p_id = pl.program_id(0)
    qi = qi_ref[p_id]
    kj = kj_ref[p_id]
    c1 = HEAD_DIM ** -0.5 * LOG2E
    on_diag = kj == lax.div(qi, r)
    sub = lax.rem(qi, r)

    def tile_is_dead():
        bound = jnp.float32(-jnp.inf)
        m_min = jnp.full((1, 1), jnp.inf, F32)
        last = (kj * r + r - 1) * FOX_STATS
        for h in range(N_HEADS):
            qb = qi * FOX_STATS + h
            kn = st_ref[kj * r * FOX_STATS + N_HEADS + h]
            for b in range(1, r):
                kn = jnp.maximum(kn, st_ref[(kj * r + b) * FOX_STATS + N_HEADS + h])
            bound = jnp.maximum(bound, st_ref[qb] * kn + st_ref[qb + 2 * N_HEADS]
                                - st_ref[last + 3 * N_HEADS + h])
            m_min = jnp.minimum(m_min, jnp.min(m_ref[h], axis=0, keepdims=True))
        return bound - m_min[0, 0] < FOX_SKIP

    def tile(width, keep):
        for h in range(N_HEADS):
            sl = slice(h * FOX_AUG, (h + 1) * FOX_AUG)
            s = _dot_nt(q_ref[:, sl], k_ref[0:width, sl]) * c1
            if keep is not None:
                s = jnp.where(keep, s, NEG_BIG)
            m_prev = m_ref[h]
            m_new = jnp.maximum(m_prev, jnp.max(s, axis=-1, keepdims=True))
            alpha = jnp.exp2(m_prev - m_new)
            p = jnp.exp2(s - m_new)
            acc_ref[h] = alpha * acc_ref[h] + _dot(p.astype(BF16), v_ref[0:width, sl])
            m_ref[h] = m_new

    @pl.when(on_diag)
    def _():
        m_ref[...] = jnp.full_like(m_ref, NEG_BIG)
        acc_ref[...] = jnp.zeros_like(acc_ref)

    for v in range(r):
        @pl.when(jnp.logical_and(on_diag, sub == v))
        def _(v=v):
            width = (v + 1) * tq
            row = lax.broadcasted_iota(jnp.int32, (tq, width), 0)
            col = lax.broadcasted_iota(jnp.int32, (tq, width), 1)
            tile(width, col - v * tq <= row)

    @pl.when(jnp.logical_not(on_diag))
    def _():
        @pl.when(jnp.logical_not(tile_is_dead()))
        def _():
            tile(tk, None)

    @pl.when(kj == 0)
    def _():
        for h in range(N_HEADS):
            acc = acc_ref[h]
            o_ref[:, h * HEAD_DIM:(h + 1) * HEAD_DIM] = (
                acc[:, :HEAD_DIM] / acc[:, HEAD_DIM:HEAD_DIM + 1]).astype(o_ref.dtype)


def _fox_attention(pm, fcol, tq, tk):
    s = pm.shape[0]
    tq = min(tq, s)
    tk = min(tk, s)
    assert tk % tq == 0 and s % tk == 0
    r = tk // tq
    nb = s // tq
    qa, ka, va, st = _fox_prep(pm, fcol, tq)
    stats = st.reshape(nb, 8, LANES)[:, 0, :FOX_STATS].reshape(nb * FOX_STATS)
    pairs = [(i, j) for i in range(nb) for j in range(i // r, -1, -1)]
    qi_tab = jnp.asarray([p[0] for p in pairs], jnp.int32)
    kj_tab = jnp.asarray([p[1] for p in pairs], jnp.int32)
    w = N_HEADS * FOX_AUG
    grid_spec = pltpu.PrefetchScalarGridSpec(
        num_scalar_prefetch=3,
        grid=(len(pairs),),
        in_specs=[pl.BlockSpec((tq, w), lambda p, qi, kj, st: (qi[p], 0)),
                  pl.BlockSpec((tk, w), lambda p, qi, kj, st: (kj[p], 0)),
                  pl.BlockSpec((tk, w), lambda p, qi, kj, st: (kj[p], 0))],
        out_specs=pl.BlockSpec((tq, BRANCH_W), lambda p, qi, kj, st: (qi[p], 0)),
        scratch_shapes=[pltpu.VMEM((N_HEADS, tq, 1), F32),
                        pltpu.VMEM((N_HEADS, tq, FOX_AUG), F32)],
    )
    return pl.pallas_call(
        functools.partial(_fox_kernel, tq=tq, tk=tk),
        out_shape=jax.ShapeDtypeStruct((s, BRANCH_W), BF16),
        grid_spec=grid_spec,
        compiler_params=_cparams(("arbitrary",), 48),
        name="fox_attention",
    )(qi_tab, kj_tab, stats, qa, ka, va)


def _sb_kernel(q_ref, k_ref, v_ref, gt_ref, o_ref, r_ref, acc_ref, *, t):
    i = pl.program_id(0)
    r_ref[...] = jnp.zeros_like(r_ref)
    acc_ref[...] = jnp.zeros_like(acc_ref)
    scale = HEAD_DIM ** -0.5
    gt = gt_ref[...]

    def block(kb, diagonal):
        k0 = pl.multiple_of(kb * t, t)
        if diagonal:
            keep = (lax.broadcasted_iota(jnp.int32, (t, t), 1)
                    < lax.broadcasted_iota(jnp.int32, (t, t), 0))
        rmax = jnp.full((1, 1), -jnp.inf, F32)
        for h in range(N_HEADS):
            sl = slice(h * HEAD_DIM, (h + 1) * HEAD_DIM)
            z = _dot_nt(q_ref[:, sl], k_ref[pl.ds(k0, t), sl]) * scale
            lsg, lom = _log_sigmoid_parts(z)
            if diagonal:
                lom = jnp.where(keep, lom, 0.0)
            r_prev = r_ref[h]
            after = _dot(lom.astype(BF16), gt) + r_prev
            a = jnp.exp(lsg + after)
            if diagonal:
                a = jnp.where(keep, a, 0.0)
            acc_ref[h] += _dot(a.astype(BF16), v_ref[pl.ds(k0, t), sl])
            r_new = r_prev + jnp.sum(lom, axis=-1, keepdims=True)
            r_ref[h] = r_new
            rmax = jnp.maximum(rmax, jnp.max(r_new, axis=0, keepdims=True))
        return rmax[0, 0]

    def cond(c):
        jj, rmax = c
        return jnp.logical_and(jj <= i, rmax > SB_EXIT)

    def body(c):
        jj, _ = c
        return jj + 1, block(i - jj, False)

    lax.while_loop(cond, body, (jnp.int32(1), block(i, True)))
    for h in range(N_HEADS):
        sl = slice(h * HEAD_DIM, (h + 1) * HEAD_DIM)
        o_ref[:, sl] = acc_ref[h].astype(o_ref.dtype)


def _sb_attention(pm, t):
    s = pm.shape[0]
    t = min(t, s)
    r = lax.broadcasted_iota(jnp.int32, (t, t), 0)
    c = lax.broadcasted_iota(jnp.int32, (t, t), 1)
    gt = (r > c).astype(BF16)
    return pl.pallas_call(
        functools.partial(_sb_kernel, t=t),
        out_shape=jax.ShapeDtypeStruct((s, BRANCH_W), BF16),
        grid=(s // t,),
        in_specs=[pl.BlockSpec((t, BRANCH_W), lambda i: (i, C_SQ // BRANCH_W)),
                  pl.BlockSpec((s, BRANCH_W), lambda i: (0, C_SK // BRANCH_W)),
                  pl.BlockSpec((s, BRANCH_W), lambda i: (0, C_SV // BRANCH_W)),
                  pl.BlockSpec((t, t), lambda i: (0, 0))],
        out_specs=pl.BlockSpec((t, BRANCH_W), lambda i: (i, 0)),
        scratch_shapes=[pltpu.VMEM((N_HEADS, t, 1), F32),
                        pltpu.VMEM((N_HEADS, t, HEAD_DIM), F32)],
        compiler_params=_cparams(("arbitrary",), 48),
        name="sb_attention",
    )(pm, pm, pm, gt)


def _ret_kernel(q_ref, k_ref, v_ref, g_ref, cos_ref, sin_ref, o_ref, st_ref, *, tb):
    @pl.when(pl.program_id(0) == 0)
    def _():
        st_ref[...] = jnp.zeros_like(st_ref)

    cos = cos_ref[...]
    sin = sin_ref[...]
    ri = lax.broadcasted_iota(jnp.int32, (tb, tb), 0)
    ci = lax.broadcasted_iota(jnp.int32, (tb, tb), 1)
    chunk_ok = (ci // CHUNK) <= (ri // CHUNK)
    dist = jnp.abs(ri - ci).astype(F32)
    idx = lax.broadcasted_iota(jnp.int32, (tb, 1), 0).astype(F32)
    scale = HEAD_DIM ** -0.5
    for h in range(N_HEADS):
        sl = slice(h * HEAD_DIM, (h + 1) * HEAD_DIM)
        lg = math.log(1.0 - 2.0 ** (-5.0 - h))
        q = q_ref[:, sl].astype(F32)
        k = k_ref[:, sl].astype(F32)
        v = v_ref[:, sl]
        qr = q * cos + pltpu.roll(q, HEAD_DIM // 2, 1) * sin
        kr = (k * cos + pltpu.roll(k, HEAD_DIM // 2, 1) * sin) * scale
        decay = jnp.where(chunk_ok, jnp.exp(lg * dist), 0.0)
        scores = _dot_nt(qr.astype(BF16), kr.astype(BF16)) * decay
        intra = _dot(scores.astype(BF16), v)
        q_dec = jnp.exp(lg * (idx + 1.0))
        k_dec = jnp.exp(lg * (tb - 1.0 - idx))
        state = st_ref[h]
        inter = _dot((qr * q_dec).astype(BF16), state.astype(BF16))
        kv = _dot_tn((kr * k_dec).astype(BF16), v)
        st_ref[h] = state * math.exp(lg * tb) + kv
        o = intra + inter
        mu = jnp.mean(o, axis=-1, keepdims=True)
        d = o - mu
        var = jnp.mean(d * d, axis=-1, keepdims=True)
        on = d * lax.rsqrt(var + LN_EPS)
        g = g_ref[:, sl].astype(F32)
        o_ref[:, sl] = (on * (g * jax.nn.sigmoid(g))).astype(o_ref.dtype)


def _retention(pm, cos_t, sin_t, tb):
    s = pm.shape[0]
    tb = min(tb, s)
    blk = lambda c: pl.BlockSpec((tb, BRANCH_W), lambda i, c=c: (i, c // BRANCH_W))
    return pl.pallas_call(
        functools.partial(_ret_kernel, tb=tb),
        out_shape=jax.ShapeDtypeStruct((s, BRANCH_W), BF16),
        grid=(s // tb,),
        in_specs=[blk(C_RQ), blk(C_RK), blk(C_RV), blk(C_RG),
                  pl.BlockSpec((tb, HEAD_DIM), lambda i: (i, 0)),
                  pl.BlockSpec((tb, HEAD_DIM), lambda i: (i, 0))],
        out_specs=pl.BlockSpec((tb, BRANCH_W), lambda i: (i, 0)),
        scratch_shapes=[pltpu.VMEM((N_HEADS, HEAD_DIM, HEAD_DIM), F32)],
        compiler_params=_cparams(("arbitrary",)),
        name="retention",
    )(pm, pm, pm, pm, cos_t, sin_t)


def _gla_kernel(q_ref, k_ref, v_ref, g_ref, ps_ref, up_ref, gb_ref, bd_ref, o_ref, st_ref, *, tb):
    @pl.when(pl.program_id(0) == 0)
    def _():
        st_ref[...] = jnp.zeros_like(st_ref)

    pre = _dot(ps_ref[...].astype(BF16), up_ref[...].astype(BF16)) + gb_ref[...]
    la, _ = _log_sigmoid_parts(pre)
    la = la / GLA_GATE_NORM
    bd = bd_ref[...]
    h1, h2, h3 = _split3(la)
    b = _dot(bd, h1) + _dot(bd, h2) + _dot(bd, h3)
    eb = jnp.exp(b)
    ieb = jnp.exp(-b)
    scale = GLA_DK ** -0.5
    q = q_ref[...].astype(F32) * scale
    k = k_ref[...].astype(F32)
    qe = q * eb
    qi = q * ieb
    ke = (k * ieb).astype(BF16)
    kf = (k * eb).astype(BF16)
    lane = lax.broadcasted_iota(jnp.int32, (1, GLA_KW), 1)
    ri = lax.broadcasted_iota(jnp.int32, (tb, tb), 0)
    ci = lax.broadcasted_iota(jnp.int32, (tb, tb), 1)
    same_chunk = (ri // CHUNK) == (ci // CHUNK)
    past = jnp.logical_and(same_chunk, ri >= ci)
    n_chunks = tb // CHUNK
    kd_parts, dec_parts = [], []
    for c in range(n_chunks):
        rs = slice(c * CHUNK, (c + 1) * CHUNK)
        b_last = b[(c + 1) * CHUNK - 1:(c + 1) * CHUNK, :]
        kd_parts.append(k[rs] * jnp.exp(b_last - b[rs]))
        dec_parts.append(jnp.exp(b_last))
    for h in range(N_HEADS):
        hm = jnp.logical_and(lane >= h * GLA_DK, lane < (h + 1) * GLA_DK)
        vs = slice(h * HEAD_DIM, (h + 1) * HEAD_DIM)
        qe_h = jnp.where(hm, qe, 0.0).astype(BF16)
        qi_h = jnp.where(hm, qi, 0.0).astype(BF16)
        scores = jnp.where(past, _dot_nt(qe_h, ke), jnp.where(same_chunk, _dot_nt(qi_h, kf), 0.0))
        intra = _dot(scores.astype(BF16), v_ref[:, vs])
        for c in range(n_chunks):
            rs = slice(c * CHUNK, (c + 1) * CHUNK)
            st = st_ref[h]
            o = intra[rs] + _dot_nt(qe_h[rs], st.astype(BF16))
            kd_h = jnp.where(hm, kd_parts[c], 0.0).astype(BF16)
            st_ref[h] = st * dec_parts[c] + _dot_tn(v_ref[rs, vs], kd_h)
            on = o * lax.rsqrt(jnp.mean(o * o, axis=-1, keepdims=True) + LN_EPS)
            g = g_ref[rs, vs].astype(F32)
            o_ref[rs, vs] = (on * (g * jax.nn.sigmoid(g))).astype(o_ref.dtype)


def _gla(pm, ps, up_pad, gbias, tb):
    s = pm.shape[0]
    tb = min(tb, s)
    r = lax.broadcasted_iota(jnp.int32, (tb, tb), 0)
    c = lax.broadcasted_iota(jnp.int32, (tb, tb), 1)
    bd = jnp.logical_and(c <= r, (c // CHUNK) == (r // CHUNK)).astype(BF16)
    return pl.pallas_call(
        functools.partial(_gla_kernel, tb=tb),
        out_shape=jax.ShapeDtypeStruct((s, BRANCH_W), BF16),
        grid=(s // tb,),
        in_specs=[pl.BlockSpec((tb, GLA_KW), lambda i: (i, C_GQ // GLA_KW)),
                  pl.BlockSpec((tb, GLA_KW), lambda i: (i, C_GK // GLA_KW)),
                  pl.BlockSpec((tb, BRANCH_W), lambda i: (i, C_GV // BRANCH_W)),
                  pl.BlockSpec((tb, BRANCH_W), lambda i: (i, C_GR // BRANCH_W)),
                  pl.BlockSpec((tb, LANES), lambda i: (i, 0)),
                  pl.BlockSpec((LANES, GLA_KW), lambda i: (0, 0)),
                  pl.BlockSpec((1, GLA_KW), lambda i: (0, 0)),
                  pl.BlockSpec((tb, tb), lambda i: (0, 0))],
        out_specs=pl.BlockSpec((tb, BRANCH_W), lambda i: (i, 0)),
        scratch_shapes=[pltpu.VMEM((N_HEADS, HEAD_DIM, GLA_KW), F32)],
        compiler_params=_cparams(("arbitrary",)),
        name="gla",
    )(pm, pm, pm, pm, ps, up_pad, gbias, bd)


def _merge_kernel(x_ref, y0_ref, y1_ref, y2_ref, y3_ref, wg_ref, bg_ref, wb_ref, o_ref, wg_s, wb_s):
    @pl.when(pl.program_id(1) == 0)
    def _():
        for n in range(N_BRANCH):
            wg_s[n] = wg_ref[0, n].astype(BF16)
            wb_s[n] = wb_ref[0, n].astype(BF16)

    x = x_ref[...]
    acc = None
    for n, y_ref in enumerate((y0_ref, y1_ref, y2_ref, y3_ref)):
        gate = jax.nn.sigmoid(_dot(x, wg_s[n]) + bg_ref[n])
        term = gate * _dot(y_ref[...], wb_s[n])
        acc = term if acc is None else acc + term
    o_ref[...] = acc.astype(o_ref.dtype)


def _merge(xb, ys, wg, bg, wb, l, bm, bn):
    m = xb.shape[0]
    bm = min(bm, m)
    yspec = pl.BlockSpec((bm, BRANCH_W), lambda j, i: (i, 0))
    return pl.pallas_call(
        _merge_kernel,
        out_shape=jax.ShapeDtypeStruct((m, D_MODEL), BF16),
        grid=(D_MODEL // bn, m // bm),
        in_specs=[pl.BlockSpec((bm, D_MODEL), lambda j, i: (i, 0)),
                  yspec, yspec, yspec, yspec,
                  pl.BlockSpec((1, N_BRANCH, D_MODEL, bn), lambda j, i: (l, 0, 0, j)),
                  pl.BlockSpec((N_BRANCH, 1, bn), lambda j, i: (0, 0, j)),
                  pl.BlockSpec((1, N_BRANCH, BRANCH_W, bn), lambda j, i: (l, 0, 0, j))],
        out_specs=pl.BlockSpec((bm, bn), lambda j, i: (i, j)),
        scratch_shapes=[pltpu.VMEM((N_BRANCH, D_MODEL, bn), BF16),
                        pltpu.VMEM((N_BRANCH, BRANCH_W, bn), BF16)],
        compiler_params=_cparams(("arbitrary", "arbitrary"), 52),
        name="merge",
    )(xb, *ys, wg, bg, wb)


def _layer_norm_rows(z, g, b):
    mu = jnp.mean(z, axis=-1, keepdims=True)
    d = z - mu
    var = jnp.mean(d * d, axis=-1, keepdims=True)
    return d * lax.rsqrt(var + LN_EPS) * g + b


def _outln_kernel(m_ref, w_ref, x_ref, g_ref, b_ref, o_ref, ob_ref, *, alpha):
    h = _dot(m_ref[...], w_ref[0])
    y = _layer_norm_rows(alpha * x_ref[...] + h, g_ref[...], b_ref[...])
    o_ref[...] = y
    ob_ref[...] = y.astype(BF16).reshape(ob_ref.shape)


def _out_ln(merged, w_out, l, x, g, b, alpha, bm):
    m = x.shape[0]
    bm = min(bm, m)
    row = pl.BlockSpec((bm, D_MODEL), lambda i: (i, 0))
    row3 = pl.BlockSpec((bm, ROW_TILES, LANES), lambda i: (i, 0, 0))
    vec = pl.BlockSpec((1, D_MODEL), lambda i: (0, 0))
    return pl.pallas_call(
        functools.partial(_outln_kernel, alpha=alpha),
        out_shape=(jax.ShapeDtypeStruct((m, D_MODEL), F32),
                   jax.ShapeDtypeStruct((m, ROW_TILES, LANES), BF16)),
        grid=(m // bm,),
        in_specs=[row, pl.BlockSpec((1, D_MODEL, D_MODEL), lambda i: (l, 0, 0)), row, vec, vec],
        out_specs=(row, row3),
        compiler_params=_cparams(("arbitrary",), 48),
        name="out_ln",
    )(merged, w_out, x, g, b)


def _router_kernel(x_ref, w_ref, b_ref, id_ref, wt_ref):
    x = x_ref[...]
    xh = x.astype(BF16)
    xl = (x - xh.astype(F32)).astype(BF16)
    w = w_ref[...]
    wh = w.astype(BF16)
    wl = (w - wh.astype(F32)).astype(BF16)
    logits = _dot(xh, wh) + _dot(xh, wl) + _dot(xl, wh) + b_ref[...]
    lane = lax.broadcasted_iota(jnp.int32, logits.shape, 1)
    neg = -jnp.inf
    gl = jnp.where(lane < N_GROUPS, logits, neg)
    gmax = jnp.max(gl, axis=-1, keepdims=True)
    g_sel = jnp.min(jnp.where(gl == gmax, lane, LANES), axis=-1, keepdims=True)
    g_w = 1.0 / jnp.sum(jnp.where(lane < N_GROUPS, jnp.exp(logits - gmax), 0.0), axis=-1, keepdims=True)
    lo = N_GROUPS + g_sel * EXPERTS_PER_GROUP
    el = jnp.where(jnp.logical_and(lane >= lo, lane < lo + EXPERTS_PER_GROUP), logits, neg)
    v1 = jnp.max(el, axis=-1, keepdims=True)
    i1 = jnp.min(jnp.where(el == v1, lane, LANES), axis=-1, keepdims=True)
    el2 = jnp.where(lane == i1, neg, el)
    v2 = jnp.max(el2, axis=-1, keepdims=True)
    i2 = jnp.min(jnp.where(el2 == v2, lane, LANES), axis=-1, keepdims=True)
    e2 = jnp.exp(v2 - v1)
    p1 = 1.0 / (1.0 + e2)
    p2 = e2 / (1.0 + e2)
    id_ref[...] = jnp.where(lane == 0, i1 - N_GROUPS, jnp.where(lane == 1, i2 - N_GROUPS, 0))
    wt_ref[...] = jnp.where(lane == 0, p1 * g_w, jnp.where(lane == 1, p2 * g_w, 0.0))


def _router(x1, w_r, b_r, bm):
    m = x1.shape[0]
    bm = min(bm, m)
    return pl.pallas_call(
        _router_kernel,
        out_shape=(jax.ShapeDtypeStruct((m, LANES), jnp.int32), jax.ShapeDtypeStruct((m, LANES), F32)),
        grid=(m // bm,),
        in_specs=[pl.BlockSpec((bm, D_MODEL), lambda i: (i, 0)),
                  pl.BlockSpec((D_MODEL, LANES), lambda i: (0, 0)),
                  pl.BlockSpec((1, LANES), lambda i: (0, 0))],
        out_specs=(pl.BlockSpec((bm, LANES), lambda i: (i, 0)), pl.BlockSpec((bm, LANES), lambda i: (i, 0))),
        compiler_params=_cparams(("arbitrary",)),
        name="router",
    )(x1, w_r, b_r)


def _start_row_gather(idx_ref, first, stride, n, src_ref, dst_ref, sem, both_queues=False, dst_first=0):
    def copy(r):
        return pltpu.make_async_copy(src_ref.at[idx_ref[first + stride * r]], dst_ref.at[dst_first + r], sem)

    if both_queues:
        def start2(r2, c):
            copy(2 * r2).start(priority=0)
            copy(2 * r2 + 1).start(priority=1)
            return c
        lax.fori_loop(0, n // 2, start2, 0, unroll=4)
    else:
        def start(r, c):
            copy(r).start()
            return c
        lax.fori_loop(0, n, start, 0, unroll=8)


def _wait_row_gather(n, src_ref, dst_ref, sem):
    pltpu.make_async_copy(src_ref.at[pl.ds(0, n)], dst_ref, sem).wait()


ROW_LOOKAHEAD = 2
ROW_SLOTS = ROW_LOOKAHEAD + 1


def _expert_kernel(tok_ref, be_ref, nv_ref, run_ref, nxt_ref, x_ref, w1_ref, w3_ref, w2_ref, o_ref,
                   xbuf, sem, w1s, w3s, w2s, wsem, w1b, w3b, w2b, *, tm, l):
    i = pl.program_id(0)
    n_valid = nv_ref[0]
    e = be_ref[i]
    prev = be_ref[jnp.maximum(i - 1, 0)]
    slot = lax.rem(i, ROW_SLOTS)
    wslot = lax.rem(run_ref[i], 2)
    run_start = jnp.logical_and(i < n_valid, jnp.logical_or(i == 0, e != prev))

    def weight_copies(expert, s):
        return (pltpu.make_async_copy(w1_ref.at[l, expert], w1s.at[s], wsem.at[s]),
                pltpu.make_async_copy(w3_ref.at[l, expert], w3s.at[s], wsem.at[s]),
                pltpu.make_async_copy(w2_ref.at[l, expert], w2s.at[s], wsem.at[s]))

    @pl.when(i == 0)
    def _():
        for c in weight_copies(e, 0):
            c.start(priority=1)
        for b in range(ROW_LOOKAHEAD):
            @pl.when(b < n_valid)
            def _(b=b):
                _start_row_gather(tok_ref, b * tm, 1, tm, x_ref, xbuf.at[b], sem.at[b])

    @pl.when(i + ROW_LOOKAHEAD < n_valid)
    def _():
        ahead = lax.rem(i + ROW_LOOKAHEAD, ROW_SLOTS)
        _start_row_gather(tok_ref, (i + ROW_LOOKAHEAD) * tm, 1, tm, x_ref, xbuf.at[ahead], sem.at[ahead])

    @pl.when(run_start)
    def _():
        for c in weight_copies(e, wslot):
            c.wait()
        nxt = nxt_ref[i]

        @pl.when(nxt >= 0)
        def _():
            for c in weight_copies(nxt, 1 - wslot):
                c.start(priority=1)

        w1b[...] = w1s[wslot].astype(BF16)
        w3b[...] = w3s[wslot].astype(BF16)
        w2b[...] = w2s[wslot].astype(BF16)

    @pl.when(i < n_valid)
    def _():
        _wait_row_gather(tm, x_ref, xbuf.at[slot], sem.at[slot])
        x = xbuf[slot].reshape(tm, D_MODEL)
        a = _dot(x, w1b[...])
        h = (a * jax.nn.sigmoid(a)) * _dot(x, w3b[...])
        y = _dot(h.astype(BF16), w2b[...])
        o_ref[...] = y.astype(BF16).reshape(o_ref.shape)

    @pl.when(i >= n_valid)
    def _():
        o_ref[...] = jnp.zeros_like(o_ref)


def _experts(plan, x3, w1, w3, w2, l, tm):
    n = plan["row_tok"].shape[0]
    any_spec = pl.BlockSpec(memory_space=pl.ANY)
    grid_spec = pltpu.PrefetchScalarGridSpec(
        num_scalar_prefetch=5,
        grid=(n // tm,),
        in_specs=[any_spec, any_spec, any_spec, any_spec],
        out_specs=pl.BlockSpec((tm, ROW_TILES, LANES), lambda i, *_: (i, 0, 0)),
        scratch_shapes=[pltpu.VMEM((ROW_SLOTS, tm, ROW_TILES, LANES), BF16),
                        pltpu.SemaphoreType.DMA((ROW_SLOTS,)),
                        pltpu.VMEM((2, D_MODEL, D_EXPERT), F32),
                        pltpu.VMEM((2, D_MODEL, D_EXPERT), F32),
                        pltpu.VMEM((2, D_EXPERT, D_MODEL), F32),
                        pltpu.SemaphoreType.DMA((2,)),
                        pltpu.VMEM((D_MODEL, D_EXPERT), BF16),
                        pltpu.VMEM((D_MODEL, D_EXPERT), BF16),
                        pltpu.VMEM((D_EXPERT, D_MODEL), BF16)],
    )
    return pl.pallas_call(
        functools.partial(_expert_kernel, tm=tm, l=l),
        out_shape=jax.ShapeDtypeStruct((n, ROW_TILES, LANES), BF16),
        grid_spec=grid_spec,
        compiler_params=_cparams(("arbitrary",), 52),
        name="experts",
    )(plan["row_tok"], plan["blk_e"], plan["n_valid"], plan["run_id"], plan["next_e"], x3, w1, w3, w2)


def _combine_kernel(pos_ref, yb_ref, x_ref, wt_ref, g_ref, b_ref, o_ref, ob_ref, buf, sem, *, tm, alpha):
    i = pl.program_id(0)
    slot = lax.rem(i, 2)

    def start_tile(tile, s):
        for k in range(TOP_K):
            _start_row_gather(pos_ref, TOP_K * tile * tm + k, TOP_K, tm, yb_ref, buf.at[s, k], sem.at[s],
                              both_queues=True)

    @pl.when(i == 0)
    def _():
        start_tile(0, 0)

    @pl.when(i + 1 < pl.num_programs(0))
    def _():
        start_tile(i + 1, 1 - slot)

    for k in range(TOP_K):
        _wait_row_gather(tm, yb_ref, buf.at[slot, k], sem.at[slot])
    wt = wt_ref[...]
    h = None
    for k in range(TOP_K):
        term = buf[slot, k].reshape(tm, D_MODEL).astype(F32) * wt[:, k:k + 1]
        h = term if h is None else h + term
    y = _layer_norm_rows(alpha * x_ref[...] + h, g_ref[...], b_ref[...])
    o_ref[...] = y
    ob_ref[...] = y.astype(BF16)


def _combine_ln(pos, yb, x1, wts, g, b, alpha, tm):
    m = x1.shape[0]
    tm = min(tm, m)
    row = lambda i, pos: (i, 0)
    grid_spec = pltpu.PrefetchScalarGridSpec(
        num_scalar_prefetch=1,
        grid=(m // tm,),
        in_specs=[pl.BlockSpec(memory_space=pl.ANY),
                  pl.BlockSpec((tm, D_MODEL), row),
                  pl.BlockSpec((tm, LANES), row),
                  pl.BlockSpec((1, D_MODEL), lambda i, pos: (0, 0)),
                  pl.BlockSpec((1, D_MODEL), lambda i, pos: (0, 0))],
        out_specs=(pl.BlockSpec((tm, D_MODEL), row), pl.BlockSpec((tm, D_MODEL), row)),
        scratch_shapes=[pltpu.VMEM((2, TOP_K, tm, ROW_TILES, LANES), BF16), pltpu.SemaphoreType.DMA((2,))],
    )
    return pl.pallas_call(
        functools.partial(_combine_kernel, tm=tm, alpha=alpha),
        out_shape=(jax.ShapeDtypeStruct((m, D_MODEL), F32), jax.ShapeDtypeStruct((m, D_MODEL), BF16)),
        grid_spec=grid_spec,
        compiler_params=_cparams(("arbitrary",)),
        name="combine_ln",
    )(pos, yb, x1, wts, g, b)


def _dispatch_plan(ids, tm):
    t = ids.shape[0]
    flat_e = ids.reshape(-1)
    n_assign = flat_e.shape[0]
    onehot = (flat_e[:, None] == jnp.arange(N_EXPERTS, dtype=jnp.int32)[None, :]).astype(jnp.int32)
    csum = jnp.cumsum(onehot, axis=0)
    rank = jnp.sum((csum - onehot) * onehot, axis=1)
    counts = csum[-1]
    padded = (counts + tm - 1) // tm * tm
    pend = jnp.cumsum(padded)
    pstart = pend - padded
    dest = jnp.sum(onehot * pstart[None, :], axis=1) + rank
    n_rows = n_assign + N_EXPERTS * tm
    n_blocks = n_rows // tm
    flat_tok = jnp.arange(n_assign, dtype=jnp.int32) // TOP_K
    row_tok = jnp.zeros((n_rows,), jnp.int32).at[dest].set(flat_tok)
    blk_start = jnp.arange(n_blocks, dtype=jnp.int32) * tm
    blk_e = jnp.minimum(jnp.sum((pend[None, :] <= blk_start[:, None]).astype(jnp.int32), axis=1),
                        N_EXPERTS - 1).astype(jnp.int32)
    n_valid = (pend[-1] // tm).astype(jnp.int32)
    blk = jnp.arange(n_blocks, dtype=jnp.int32)
    is_start = jnp.logical_and(blk < n_valid, jnp.logical_or(blk == 0, blk_e != jnp.roll(blk_e, 1)))
    run_id = jnp.cumsum(is_start.astype(jnp.int32)) - 1
    start_pos = jnp.where(is_start, blk, n_blocks)
    next_start = lax.cummin(jnp.concatenate([start_pos[1:], jnp.full((1,), n_blocks, jnp.int32)]), reverse=True)
    next_e = jnp.where(next_start < n_blocks, blk_e[jnp.minimum(next_start, n_blocks - 1)], -1)
    return dict(row_tok=row_tok, blk_e=blk_e, n_valid=n_valid.reshape(1), run_id=run_id.astype(jnp.int32),
                next_e=next_e.astype(jnp.int32), dest=dest.astype(jnp.int32))


def _rope_tables(s):
    half = HEAD_DIM // 2
    inv = ROPE_BASE ** (-jnp.arange(half, dtype=F32) / half)
    ang = jnp.arange(s, dtype=F32)[:, None] * inv[None, :]
    cos, sin = jnp.cos(ang), jnp.sin(ang)
    return jnp.concatenate([cos, cos], axis=1), jnp.concatenate([-sin, sin], axis=1)


def kernel(x, w_in, fox_forget_bias, gla_gate_up, gla_gate_bias, w_branch, w_gate, b_gate, w_out, ln1_g, ln1_b, w_group, b_group, w_expert_router, b_expert_router, w1, w3, w2, ln2_g, ln2_b):
    bsz, s, d = x.shape
    depth = w_in.shape[0]
    alpha = DEEPNORM_ALPHA
    assert bsz == 1 and d == D_MODEL and w_in.shape[2] == IN_WIDTH
    t = bsz * s
    tm_moe = 256

    w_in_b = w_in.astype(BF16)
    wo_b = w_out.astype(BF16)
    fbias = jnp.pad(fox_forget_bias, ((0, 0), (0, LANES - N_HEADS)))[:, None, :]
    up_pad = jnp.pad(gla_gate_up, ((0, 0), (S_GA, LANES - S_GA - GLA_GATE_RANK), (0, 0)))
    w_r = jnp.pad(jnp.concatenate([w_group, w_expert_router], axis=2),
                  ((0, 0), (0, 0), (0, LANES - N_GROUPS - N_EXPERTS)))
    b_r = jnp.pad(jnp.concatenate([b_group, b_expert_router], axis=1),
                  ((0, 0), (0, LANES - N_GROUPS - N_EXPERTS)))[:, None, :]
    cos_t, sin_t = _rope_tables(s)

    xf = x.reshape(t, d)
    xb = xf.astype(BF16)
    for l in range(depth):
        pm = _proj_main(xb, w_in_b, l, 2048)
        ps = _proj_small(xb, w_in_b, l, 1024)
        fcol = _forget_cumsum(ps, fbias[l], 512)
        y_fox = _fox_attention(pm, fcol, 512, 1024)
        y_ret = _retention(pm, cos_t, sin_t, 256)
        y_sb = _sb_attention(pm, 256)
        y_gla = _gla(pm, ps, up_pad[l], gla_gate_bias[l][None, :], 256)
        merged = _merge(xb, (y_fox, y_ret, y_sb, y_gla), w_gate, b_gate[l][:, None, :], w_branch, l, 1024, 256)
        x1, x1b3 = _out_ln(merged, wo_b, l, xf, ln1_g[l][None, :], ln1_b[l][None, :], alpha, 512)

        ids, wts = _router(x1, w_r[l], b_r[l], 512)
        plan = _dispatch_plan(ids[:, :TOP_K], tm_moe)
        yb3 = _experts(plan, x1b3, w1, w3, w2, l, tm_moe)
        xf, xb = _combine_ln(plan["dest"], yb3, x1, wts, ln2_g[l][None, :], ln2_b[l][None, :], alpha, 256)
    return xf.reshape(bsz, s, d)
```

```python
import functools
import math

import jax
import jax.numpy as jnp
from jax import lax
from jax.experimental import pallas as pl
from jax.experimental.pallas import tpu as pltpu

F32 = jnp.float32
BF16 = jnp.bfloat16

D_MODEL = 2048
CHUNK = 64
N_HEADS = 4
HEAD_DIM = 128
BRANCH_W = N_HEADS * HEAD_DIM
GLA_DK = 64
GLA_KW = N_HEADS * GLA_DK
GLA_GATE_RANK = 16
GLA_GATE_NORM = 16.0
N_BRANCH = 4
N_GROUPS = 4
EXPERTS_PER_GROUP = 8
N_EXPERTS = N_GROUPS * EXPERTS_PER_GROUP
TOP_K = 2
D_EXPERT = D_MODEL // 4
ROPE_BASE = 10000.0
LN_EPS = 1e-5
DEPTH = 4
DEEPNORM_ALPHA = (2.0 * DEPTH) ** 0.25

LANES = 128
ROW_TILES = D_MODEL // LANES
NEG_BIG = -1e30
SB_EXIT = -104.0

C_FQ, C_FK, C_FV = 0, 512, 1024
C_RQ, C_RK, C_RV, C_RG = 1536, 2048, 2560, 3072
C_SQ, C_SK, C_SV = 3584, 4096, 4608
C_GQ, C_GK, C_GV, C_GR = 5120, 5376, 5632, 6144
MAIN_W = 6656
O_FF, O_GA = 1536, 6660
IN_WIDTH = 6676
S_FF, S_GA = 0, 4


def _cparams(sem, vmem_mb=None):
    kw = dict(dimension_semantics=sem)
    if vmem_mb is not None:
        kw["vmem_limit_bytes"] = vmem_mb * 1024 * 1024
    return pltpu.CompilerParams(**kw)


def _log_sigmoid_parts(z):
    t = jnp.log(1.0 + jnp.exp(-jnp.abs(z)))
    return jnp.minimum(z, 0.0) - t, -(jnp.maximum(z, 0.0) + t)


def _split3(x):
    h1 = x.astype(BF16)
    r1 = x - h1.astype(F32)
    h2 = r1.astype(BF16)
    h3 = (r1 - h2.astype(F32)).astype(BF16)
    return h1, h2, h3


def _dot(a, b):
    return jnp.dot(a, b, preferred_element_type=F32)


def _dot_nt(a, b):
    return lax.dot_general(a, b, (((1,), (1,)), ((), ())), preferred_element_type=F32)


def _dot_tn(a, b):
    return lax.dot_general(a, b, (((0,), (0,)), ((), ())), preferred_element_type=F32)


PROJ_BN = 512
N_ALIGNED_TILES = O_FF // PROJ_BN


def _proj_kernel(x_ref, wa_ref, wb_ref, o_ref, w_scr):
    j = pl.program_id(0)
    first_row_tile = pl.program_id(1) == 0

    @pl.when(jnp.logical_and(first_row_tile, j < N_ALIGNED_TILES))
    def _():
        w_scr[...] = wa_ref[0]

    @pl.when(jnp.logical_and(first_row_tile, j >= N_ALIGNED_TILES))
    def _():
        ab = jnp.concatenate([wa_ref[0], wb_ref[0]], axis=1).astype(F32)
        width = PROJ_BN + LANES
        w_scr[...] = pltpu.roll(ab, width - N_HEADS, 1)[:, :PROJ_BN].astype(BF16)

    o_ref[...] = _dot(x_ref[...], w_scr[...]).astype(o_ref.dtype)


def _proj_main(xb, w_in, l, bm):
    m = xb.shape[0]
    bm = min(bm, m)
    lanes_per_tile = PROJ_BN // LANES
    return pl.pallas_call(
        _proj_kernel,
        out_shape=jax.ShapeDtypeStruct((m, MAIN_W), BF16),
        grid=(MAIN_W // PROJ_BN, m // bm),
        in_specs=[pl.BlockSpec((bm, D_MODEL), lambda j, i: (i, 0)),
                  pl.BlockSpec((1, D_MODEL, PROJ_BN), lambda j, i: (l, 0, j)),
                  pl.BlockSpec((1, D_MODEL, LANES), lambda j, i: (l, 0, lanes_per_tile * (j + 1)))],
        out_specs=pl.BlockSpec((bm, PROJ_BN), lambda j, i: (i, j)),
        scratch_shapes=[pltpu.VMEM((D_MODEL, PROJ_BN), BF16)],
        compiler_params=_cparams(("arbitrary", "arbitrary"), 48),
        name="proj_main",
    )(xb, w_in, w_in)


def _proj_small_kernel(x_ref, wf_ref, wg_ref, o_ref):
    lane = lax.broadcasted_iota(jnp.int32, (1, LANES), 1)
    w = jnp.where(lane < S_GA, wf_ref[0].astype(F32),
                  jnp.where(lane < S_GA + GLA_GATE_RANK, wg_ref[0].astype(F32), 0.0))
    o_ref[...] = _dot(x_ref[...], w.astype(BF16))


def _proj_small(xb, w_in, l, bm):
    m = xb.shape[0]
    bm = min(bm, m)
    assert O_FF % LANES == 0 and O_GA % LANES == S_GA
    return pl.pallas_call(
        _proj_small_kernel,
        out_shape=jax.ShapeDtypeStruct((m, LANES), F32),
        grid=(m // bm,),
        in_specs=[pl.BlockSpec((bm, D_MODEL), lambda i: (i, 0)),
                  pl.BlockSpec((1, D_MODEL, LANES), lambda i: (l, 0, O_FF // LANES)),
                  pl.BlockSpec((1, D_MODEL, LANES), lambda i: (l, 0, O_GA // LANES))],
        out_specs=pl.BlockSpec((bm, LANES), lambda i: (i, 0)),
        compiler_params=_cparams(("arbitrary",)),
        name="proj_small",
    )(xb, w_in, w_in)


def _fcum_kernel(ps_ref, bias_ref, tri_ref, o_ref, carry_ref):
    @pl.when(pl.program_id(0) == 0)
    def _():
        carry_ref[...] = jnp.zeros_like(carry_ref)

    lf, _ = _log_sigmoid_parts(ps_ref[...] + bias_ref[...])
    tri = tri_ref[...]
    h1, h2, h3 = _split3(lf)
    c = _dot(tri, h1) + _dot(tri, h2) + _dot(tri, h3) + carry_ref[0:1, :]
    o_ref[...] = c
    carry_ref[...] = jnp.broadcast_to(c[-1:, :], carry_ref.shape)


def _forget_cumsum(ps, bias_row, tb):
    s = ps.shape[0]
    tb = min(tb, s)
    r = lax.broadcasted_iota(jnp.int32, (tb, tb), 0)
    c = lax.broadcasted_iota(jnp.int32, (tb, tb), 1)
    tri = (c <= r).astype(BF16)
    return pl.pallas_call(
        _fcum_kernel,
        out_shape=jax.ShapeDtypeStruct((s, LANES), F32),
        grid=(s // tb,),
        in_specs=[pl.BlockSpec((tb, LANES), lambda i: (i, 0)),
                  pl.BlockSpec((1, LANES), lambda i: (0, 0)),
                  pl.BlockSpec((tb, tb), lambda i: (0, 0))],
        out_specs=pl.BlockSpec((tb, LANES), lambda i: (i, 0)),
        scratch_shapes=[pltpu.VMEM((8, LANES), F32)],
        compiler_params=_cparams(("arbitrary",)),
        name="forget_cumsum",
    )(ps, bias_row, tri)


FOX_AUG = 2 * HEAD_DIM
LOG2E = 1.4426950408889634


def _fox_prep_kernel(q_ref, k_ref, v_ref, f_ref, qa_ref, ka_ref, va_ref, st_ref):
    tb = q_ref.shape[0]
    lane = lax.broadcasted_iota(jnp.int32, (tb, HEAD_DIM), 1)
    lane1 = lax.broadcasted_iota(jnp.int32, (1, LANES), 1)
    inv_scale = HEAD_DIM ** 0.5
    c1 = HEAD_DIM ** -0.5 * LOG2E
    stats = jnp.zeros((1, LANES), F32)

    def max_norm(x_ref, sl):
        x = x_ref[:, sl].astype(F32)
        n2 = jnp.sum(x * x, axis=-1, keepdims=True)
        return jnp.sqrt(jnp.max(n2, axis=0, keepdims=True))

    for h in range(N_HEADS):
        sl = slice(h * HEAD_DIM, (h + 1) * HEAD_DIM)
        stats = jnp.where(lane1 == h, c1 * max_norm(q_ref, sl), stats)
        stats = jnp.where(lane1 == N_HEADS + h, max_norm(k_ref, sl), stats)
        stats = jnp.where(lane1 == 2 * N_HEADS + h, f_ref[0:1, h:h + 1] * LOG2E, stats)
        stats = jnp.where(lane1 == 3 * N_HEADS + h, f_ref[tb - 1:tb, h:h + 1] * LOG2E, stats)
        f = f_ref[:, h:h + 1] * inv_scale
        h1, h2, h3 = (p.astype(F32) for p in _split3(f))
        aq = jnp.where(lane == 0, h1, jnp.where(lane == 1, h2, jnp.where(lane == 2, h3,
                       jnp.where(lane < 6, 1.0, 0.0)))).astype(BF16)
        ak = jnp.where(lane < 3, 1.0, jnp.where(lane == 3, -h1, jnp.where(lane == 4, -h2,
                       jnp.where(lane == 5, -h3, 0.0)))).astype(BF16)
        qa_ref[:, h * FOX_AUG:h * FOX_AUG + HEAD_DIM] = q_ref[:, sl]
        qa_ref[:, h * FOX_AUG + HEAD_DIM:(h + 1) * FOX_AUG] = aq
        ka_ref[:, h * FOX_AUG:h * FOX_AUG + HEAD_DIM] = k_ref[:, sl]
        ka_ref[:, h * FOX_AUG + HEAD_DIM:(h + 1) * FOX_AUG] = ak
        va_ref[:, h * FOX_AUG:h * FOX_AUG + HEAD_DIM] = v_ref[:, sl]
        va_ref[:, h * FOX_AUG + HEAD_DIM:(h + 1) * FOX_AUG] = jnp.where(lane == 0, 1.0, 0.0).astype(BF16)
    st_ref[...] = jnp.broadcast_to(stats, st_ref.shape)


def _fox_prep(pm, fcol, tb):
    s = pm.shape[0]
    tb = min(tb, s)
    blk = lambda c: pl.BlockSpec((tb, BRANCH_W), lambda i, c=c: (i, c // BRANCH_W))
    out = pl.BlockSpec((tb, N_HEADS * FOX_AUG), lambda i: (i, 0))
    shp = jax.ShapeDtypeStruct((s, N_HEADS * FOX_AUG), BF16)
    return pl.pallas_call(
        _fox_prep_kernel,
        out_shape=(shp, shp, shp, jax.ShapeDtypeStruct((s // tb * 8, LANES), F32)),
        grid=(s // tb,),
        in_specs=[blk(C_FQ), blk(C_FK), blk(C_FV), pl.BlockSpec((tb, LANES), lambda i: (i, 0))],
        out_specs=(out, out, out, pl.BlockSpec((8, LANES), lambda i: (i, 0))),
        compiler_params=_cparams(("arbitrary",)),
        name="fox_prep",
    )(pm, pm, pm, fcol)


FOX_STATS = 4 * N_HEADS
FOX_SKIP = -160.0


def _fox_kernel(qi_ref, kj_ref, st_ref, q_ref, k_ref, v_ref, o_ref, m_ref, acc_ref, *, tq, tk):
    r = tk // tq
    p_id = pl.program_id(0)
    qi = qi_ref[p_id]
    kj = kj_ref[p_id]
    c1 = HEAD_DIM ** -0.5 * LOG2E
    on_diag = kj == lax.div(qi, r)
    sub = lax.rem(qi, r)

    def tile_is_dead():
        bound = jnp.float32(-jnp.inf)
        m_min = jnp.full((1, 1), jnp.inf, F32)
        last = (kj * r + r - 1) * FOX_STATS
        for h in range(N_HEADS):
            qb = qi * FOX_STATS + h
            kn = st_ref[kj * r * FOX_STATS + N_HEADS + h]
            for b in range(1, r):
                kn = jnp.maximum(kn, st_ref[(kj * r + b) * FOX_STATS + N_HEADS + h])
            bound = jnp.maximum(bound, st_ref[qb] * kn + st_ref[qb + 2 * N_HEADS]
                                - st_ref[last + 3 * N_HEADS + h])
            m_min = jnp.minimum(m_min, jnp.min(m_ref[h], axis=0, keepdims=True))
        return bound - m_min[0, 0] < FOX_SKIP

    def tile(width, keep):
        for h in range(N_HEADS):
            sl = slice(h * FOX_AUG, (h + 1) * FOX_AUG)
            s = _dot_nt(q_ref[:, sl], k_ref[0:width, sl]) * c1
            if keep is not None:
                s = jnp.where(keep, s, NEG_BIG)
            m_prev = m_ref[h]
            m_new = jnp.maximum(m_prev, jnp.max(s, axis=-1, keepdims=True))
            alpha = jnp.exp2(m_prev - m_new)
            p = jnp.exp2(s - m_new)
            acc_ref[h] = alpha * acc_ref[h] + _dot(p.astype(BF16), v_ref[0:width, sl])
            m_ref[h] = m_new

    @pl.when(on_diag)
    def _():
        m_ref[...] = jnp.full_like(m_ref, NEG_BIG)
        acc_ref[...] = jnp.zeros_like(acc_ref)

    for v in range(r):
        @pl.when(jnp.logical_and(on_diag, sub == v))
        def _(v=v):
            width = (v + 1) * tq
            row = lax.broadcasted_iota(jnp.int32, (tq, width), 0)
            col = lax.broadcasted_iota(jnp.int32, (tq, width), 1)
            tile(width, col - v * tq <= row)

    @pl.when(jnp.logical_not(on_diag))
    def _():
        @pl.when(jnp.logical_not(tile_is_dead()))
        def _():
            tile(tk, None)

    @pl.when(kj == 0)
    def _():
        for h in range(N_HEADS):
            acc = acc_ref[h]
            o_ref[:, h * HEAD_DIM:(h + 1) * HEAD_DIM] = (
                acc[:, :HEAD_DIM] / acc[:, HEAD_DIM:HEAD_DIM + 1]).astype(o_ref.dtype)


def _fox_attention(pm, fcol, tq, tk):
    s = pm.shape[0]
    tq = min(tq, s)
    tk = min(tk, s)
    assert tk % tq == 0 and s % tk == 0
    r = tk // tq
    nb = s // tq
    qa, ka, va, st = _fox_prep(pm, fcol, tq)
    stats = st.reshape(nb, 8, LANES)[:, 0, :FOX_STATS].reshape(nb * FOX_STATS)
    pairs = [(i, j) for i in range(nb) for j in range(i // r, -1, -1)]
    qi_tab = jnp.asarray([p[0] for p in pairs], jnp.int32)
    kj_tab = jnp.asarray([p[1] for p in pairs], jnp.int32)
    w = N_HEADS * FOX_AUG
    grid_spec = pltpu.PrefetchScalarGridSpec(
        num_scalar_prefetch=3,
        grid=(len(pairs),),
        in_specs=[pl.BlockSpec((tq, w), lambda p, qi, kj, st: (qi[p], 0)),
                  pl.BlockSpec((tk, w), lambda p, qi, kj, st: (kj[p], 0)),
                  pl.BlockSpec((tk, w), lambda p, qi, kj, st: (kj[p], 0))],
        out_specs=pl.BlockSpec((tq, BRANCH_W), lambda p, qi, kj, st: (qi[p], 0)),
        scratch_shapes=[pltpu.VMEM((N_HEADS, tq, 1), F32),
                        pltpu.VMEM((N_HEADS, tq, FOX_AUG), F32)],
    )
    return pl.pallas_call(
        functools.partial(_fox_kernel, tq=tq, tk=tk),
        out_shape=jax.ShapeDtypeStruct((s, BRANCH_W), BF16),
        grid_spec=grid_spec,
        compiler_params=_cparams(("arbitrary",), 48),
        name="fox_attention",
    )(qi_tab, kj_tab, stats, qa, ka, va)


def _sb_kernel(q_ref, k_ref, v_ref, gt_ref, o_ref, r_ref, acc_ref, *, t):
    i = pl.program_id(0)
    r_ref[...] = jnp.zeros_like(r_ref)
    acc_ref[...] = jnp.zeros_like(acc_ref)
    scale = HEAD_DIM ** -0.5
    gt = gt_ref[...]

    def block(kb, diagonal):
        k0 = pl.multiple_of(kb * t, t)
        if diagonal:
            keep = (lax.broadcasted_iota(jnp.int32, (t, t), 1)
                    < lax.broadcasted_iota(jnp.int32, (t, t), 0))
        rmax = jnp.full((1, 1), -jnp.inf, F32)
        for h in range(N_HEADS):
            sl = slice(h * HEAD_DIM, (h + 1) * HEAD_DIM)
            z = _dot_nt(q_ref[:, sl], k_ref[pl.ds(k0, t), sl]) * scale
            lsg, lom = _log_sigmoid_parts(z)
            if diagonal:
                lom = jnp.where(keep, lom, 0.0)
            r_prev = r_ref[h]
            after = _dot(lom.astype(BF16), gt) + r_prev
            a = jnp.exp(lsg + after)
            if diagonal:
                a = jnp.where(keep, a, 0.0)
            acc_ref[h] += _dot(a.astype(BF16), v_ref[pl.ds(k0, t), sl])
            r_new = r_prev + jnp.sum(lom, axis=-1, keepdims=True)
            r_ref[h] = r_new
            rmax = jnp.maximum(rmax, jnp.max(r_new, axis=0, keepdims=True))
        return rmax[0, 0]

    def cond(c):
        jj, rmax = c
        return jnp.logical_and(jj <= i, rmax > SB_EXIT)

    def body(c):
        jj, _ = c
        return jj + 1, block(i - jj, False)

    lax.while_loop(cond, body, (jnp.int32(1), block(i, True)))
    for h in range(N_HEADS):
        sl = slice(h * HEAD_DIM, (h + 1) * HEAD_DIM)
        o_ref[:, sl] = acc_ref[h].astype(o_ref.dtype)


def _sb_attention(pm, t):
    s = pm.shape[0]
    t = min(t, s)
    r = lax.broadcasted_iota(jnp.int32, (t, t), 0)
    c = lax.broadcasted_iota(jnp.int32, (t, t), 1)
    gt = (r > c).astype(BF16)
    return pl.pallas_call(
        functools.partial(_sb_kernel, t=t),
        out_shape=jax.ShapeDtypeStruct((s, BRANCH_W), BF16),
        grid=(s // t,),
        in_specs=[pl.BlockSpec((t, BRANCH_W), lambda i: (i, C_SQ // BRANCH_W)),
                  pl.BlockSpec((s, BRANCH_W), lambda i: (0, C_SK // BRANCH_W)),
                  pl.BlockSpec((s, BRANCH_W), lambda i: (0, C_SV // BRANCH_W)),
                  pl.BlockSpec((t, t), lambda i: (0, 0))],
        out_specs=pl.BlockSpec((t, BRANCH_W), lambda i: (i, 0)),
        scratch_shapes=[pltpu.VMEM((N_HEADS, t, 1), F32),
                        pltpu.VMEM((N_HEADS, t, HEAD_DIM), F32)],
        compiler_params=_cparams(("arbitrary",), 48),
        name="sb_attention",
    )(pm, pm, pm, gt)


def _ret_kernel(q_ref, k_ref, v_ref, g_ref, cos_ref, sin_ref, o_ref, st_ref, *, tb):
    @pl.when(pl.program_id(0) == 0)
    def _():
        st_ref[...] = jnp.zeros_like(st_ref)

    cos = cos_ref[...]
    sin = sin_ref[...]
    ri = lax.broadcasted_iota(jnp.int32, (tb, tb), 0)
    ci = lax.broadcasted_iota(jnp.int32, (tb, tb), 1)
    chunk_ok = (ci // CHUNK) <= (ri // CHUNK)
    dist = jnp.abs(ri - ci).astype(F32)
    idx = lax.broadcasted_iota(jnp.int32, (tb, 1), 0).astype(F32)
    scale = HEAD_DIM ** -0.5
    for h in range(N_HEADS):
        sl = slice(h * HEAD_DIM, (h + 1) * HEAD_DIM)
        lg = math.log(1.0 - 2.0 ** (-5.0 - h))
        q = q_ref[:, sl].astype(F32)
        k = k_ref[:, sl].astype(F32)
        v = v_ref[:, sl]
        qr = q * cos + pltpu.roll(q, HEAD_DIM // 2, 1) * sin
        kr = (k * cos + pltpu.roll(k, HEAD_DIM // 2, 1) * sin) * scale
        decay = jnp.where(chunk_ok, jnp.exp(lg * dist), 0.0)
        scores = _dot_nt(qr.astype(BF16), kr.astype(BF16)) * decay
        intra = _dot(scores.astype(BF16), v)
        q_dec = jnp.exp(lg * (idx + 1.0))
        k_dec = jnp.exp(lg * (tb - 1.0 - idx))
        state = st_ref[h]
        inter = _dot((qr * q_dec).astype(BF16), state.astype(BF16))
        kv = _dot_tn((kr * k_dec).astype(BF16), v)
        st_ref[h] = state * math.exp(lg * tb) + kv
        o = intra + inter
        mu = jnp.mean(o, axis=-1, keepdims=True)
        d = o - mu
        var = jnp.mean(d * d, axis=-1, keepdims=True)
        on = d * lax.rsqrt(var + LN_EPS)
        g = g_ref[:, sl].astype(F32)
        o_ref[:, sl] = (on * (g * jax.nn.sigmoid(g))).astype(o_ref.dtype)


def _retention(pm, cos_t, sin_t, tb):
    s = pm.shape[0]
    tb = min(tb, s)
    blk = lambda c: pl.BlockSpec((tb, BRANCH_W), lambda i, c=c: (i, c // BRANCH_W))
    return pl.pallas_call(
        functools.partial(_ret_kernel, tb=tb),
        out_shape=jax.ShapeDtypeStruct((s, BRANCH_W), BF16),
        grid=(s // tb,),
        in_specs=[blk(C_RQ), blk(C_RK), blk(C_RV), blk(C_RG),
                  pl.BlockSpec((tb, HEAD_DIM), lambda i: (i, 0)),
                  pl.BlockSpec((tb, HEAD_DIM), lambda i: (i, 0))],
        out_specs=pl.BlockSpec((tb, BRANCH_W), lambda i: (i, 0)),
        scratch_shapes=[pltpu.VMEM((N_HEADS, HEAD_DIM, HEAD_DIM), F32)],
        compiler_params=_cparams(("arbitrary",)),
        name="retention",
    )(pm, pm, pm, pm, cos_t, sin_t)


def _gla_kernel(q_ref, k_ref, v_ref, g_ref, ps_ref, up_ref, gb_ref, bd_ref, o_ref, st_ref, *, tb):
    @pl.when(pl.program_id(0) == 0)
    def _():
        st_ref[...] = jnp.zeros_like(st_ref)

    pre = _dot(ps_ref[...].astype(BF16), up_ref[...].astype(BF16)) + gb_ref[...]
    la, _ = _log_sigmoid_parts(pre)
    la = la / GLA_GATE_NORM
    bd = bd_ref[...]
    h1, h2, h3 = _split3(la)
    b = _dot(bd, h1) + _dot(bd, h2) + _dot(bd, h3)
    eb = jnp.exp(b)
    ieb = jnp.exp(-b)
    scale = GLA_DK ** -0.5
    q = q_ref[...].astype(F32) * scale
    k = k_ref[...].astype(F32)
    qe = q * eb
    qi = q * ieb
    ke = (k * ieb).astype(BF16)
    kf = (k * eb).astype(BF16)
    lane = lax.broadcasted_iota(jnp.int32, (1, GLA_KW), 1)
    ri = lax.broadcasted_iota(jnp.int32, (tb, tb), 0)
    ci = lax.broadcasted_iota(jnp.int32, (tb, tb), 1)
    same_chunk = (ri // CHUNK) == (ci // CHUNK)
    past = jnp.logical_and(same_chunk, ri >= ci)
    n_chunks = tb // CHUNK
    kd_parts, dec_parts = [], []
    for c in range(n_chunks):
        rs = slice(c * CHUNK, (c + 1) * CHUNK)
        b_last = b[(c + 1) * CHUNK - 1:(c + 1) * CHUNK, :]
        kd_parts.append(k[rs] * jnp.exp(b_last - b[rs]))
        dec_parts.append(jnp.exp(b_last))
    for h in range(N_HEADS):
        hm = jnp.logical_and(lane >= h * GLA_DK, lane < (h + 1) * GLA_DK)
        vs = slice(h * HEAD_DIM, (h + 1) * HEAD_DIM)
        qe_h = jnp.where(hm, qe, 0.0).astype(BF16)
        qi_h = jnp.where(hm, qi, 0.0).astype(BF16)
        scores = jnp.where(past, _dot_nt(qe_h, ke), jnp.where(same_chunk, _dot_nt(qi_h, kf), 0.0))
        intra = _dot(scores.astype(BF16), v_ref[:, vs])
        for c in range(n_chunks):
            rs = slice(c * CHUNK, (c + 1) * CHUNK)
            st = st_ref[h]
            o = intra[rs] + _dot_nt(qe_h[rs], st.astype(BF16))
            kd_h = jnp.where(hm, kd_parts[c], 0.0).astype(BF16)
            st_ref[h] = st * dec_parts[c] + _dot_tn(v_ref[rs, vs], kd_h)
            on = o * lax.rsqrt(jnp.mean(o * o, axis=-1, keepdims=True) + LN_EPS)
            g = g_ref[rs, vs].astype(F32)
            o_ref[rs, vs] = (on * (g * jax.nn.sigmoid(g))).astype(o_ref.dtype)


def _gla(pm, ps, up_pad, gbias, tb):
    s = pm.shape[0]
    tb = min(tb, s)
    r = lax.broadcasted_iota(jnp.int32, (tb, tb), 0)
    c = lax.broadcasted_iota(jnp.int32, (tb, tb), 1)
    bd = jnp.logical_and(c <= r, (c // CHUNK) == (r // CHUNK)).astype(BF16)
    return pl.pallas_call(
        functools.partial(_gla_kernel, tb=tb),
        out_shape=jax.ShapeDtypeStruct((s, BRANCH_W), BF16),
        grid=(s // tb,),
        in_specs=[pl.BlockSpec((tb, GLA_KW), lambda i: (i, C_GQ // GLA_KW)),
                  pl.BlockSpec((tb, GLA_KW), lambda i: (i, C_GK // GLA_KW)),
                  pl.BlockSpec((tb, BRANCH_W), lambda i: (i, C_GV // BRANCH_W)),
                  pl.BlockSpec((tb, BRANCH_W), lambda i: (i, C_GR // BRANCH_W)),
                  pl.BlockSpec((tb, LANES), lambda i: (i, 0)),
                  pl.BlockSpec((LANES, GLA_KW), lambda i: (0, 0)),
                  pl.BlockSpec((1, GLA_KW), lambda i: (0, 0)),
                  pl.BlockSpec((tb, tb), lambda i: (0, 0))],
        out_specs=pl.BlockSpec((tb, BRANCH_W), lambda i: (i, 0)),
        scratch_shapes=[pltpu.VMEM((N_HEADS, HEAD_DIM, GLA_KW), F32)],
        compiler_params=_cparams(("arbitrary",)),
        name="gla",
    )(pm, pm, pm, pm, ps, up_pad, gbias, bd)


def _merge_kernel(x_ref, y0_ref, y1_ref, y2_ref, y3_ref, wg_ref, bg_ref, wb_ref, o_ref, wg_s, wb_s):
    @pl.when(pl.program_id(1) == 0)
    def _():
        for n in range(N_BRANCH):
            wg_s[n] = wg_ref[0, n].astype(BF16)
            wb_s[n] = wb_ref[0, n].astype(BF16)

    x = x_ref[...]
    acc = None
    for n, y_ref in enumerate((y0_ref, y1_ref, y2_ref, y3_ref)):
        gate = jax.nn.sigmoid(_dot(x, wg_s[n]) + bg_ref[n])
        term = gate * _dot(y_ref[...], wb_s[n])
        acc = term if acc is None else acc + term
    o_ref[...] = acc.astype(o_ref.dtype)


def _merge(xb, ys, wg, bg, wb, l, bm, bn):
    m = xb.shape[0]
    bm = min(bm, m)
    yspec = pl.BlockSpec((bm, BRANCH_W), lambda j, i: (i, 0))
    return pl.pallas_call(
        _merge_kernel,
        out_shape=jax.ShapeDtypeStruct((m, D_MODEL), BF16),
        grid=(D_MODEL // bn, m // bm),
        in_specs=[pl.BlockSpec((bm, D_MODEL), lambda j, i: (i, 0)),
                  yspec, yspec, yspec, yspec,
                  pl.BlockSpec((1, N_BRANCH, D_MODEL, bn), lambda j, i: (l, 0, 0, j)),
                  pl.BlockSpec((N_BRANCH, 1, bn), lambda j, i: (0, 0, j)),
                  pl.BlockSpec((1, N_BRANCH, BRANCH_W, bn), lambda j, i: (l, 0, 0, j))],
        out_specs=pl.BlockSpec((bm, bn), lambda j, i: (i, j)),
        scratch_shapes=[pltpu.VMEM((N_BRANCH, D_MODEL, bn), BF16),
                        pltpu.VMEM((N_BRANCH, BRANCH_W, bn), BF16)],
        compiler_params=_cparams(("arbitrary", "arbitrary"), 52),
        name="merge",
    )(xb, *ys, wg, bg, wb)


def _layer_norm_rows(z, g, b):
    mu = jnp.mean(z, axis=-1, keepdims=True)
    d = z - mu
    var = jnp.mean(d * d, axis=-1, keepdims=True)
    return d * lax.rsqrt(var + LN_EPS) * g + b


def _outln_kernel(m_ref, w_ref, x_ref, g_ref, b_ref, o_ref, ob_ref, *, alpha):
    h = _dot(m_ref[...], w_ref[0])
    y = _layer_norm_rows(alpha * x_ref[...] + h, g_ref[...], b_ref[...])
    o_ref[...] = y
    ob_ref[...] = y.astype(BF16).reshape(ob_ref.shape)


def _out_ln(merged, w_out, l, x, g, b, alpha, bm):
    m = x.shape[0]
    bm = min(bm, m)
    row = pl.BlockSpec((bm, D_MODEL), lambda i: (i, 0))
    row3 = pl.BlockSpec((bm, ROW_TILES, LANES), lambda i: (i, 0, 0))
    vec = pl.BlockSpec((1, D_MODEL), lambda i: (0, 0))
    return pl.pallas_call(
        functools.partial(_outln_kernel, alpha=alpha),
        out_shape=(jax.ShapeDtypeStruct((m, D_MODEL), F32),
                   jax.ShapeDtypeStruct((m, ROW_TILES, LANES), BF16)),
        grid=(m // bm,),
        in_specs=[row, pl.BlockSpec((1, D_MODEL, D_MODEL), lambda i: (l, 0, 0)), row, vec, vec],
        out_specs=(row, row3),
        compiler_params=_cparams(("arbitrary",), 48),
        name="out_ln",
    )(merged, w_out, x, g, b)


def _router_kernel(x_ref, w_ref, b_ref, id_ref, wt_ref):
    x = x_ref[...]
    xh = x.astype(BF16)
    xl = (x - xh.astype(F32)).astype(BF16)
    w = w_ref[...]
    wh = w.astype(BF16)
    wl = (w - wh.astype(F32)).astype(BF16)
    logits = _dot(xh, wh) + _dot(xh, wl) + _dot(xl, wh) + b_ref[...]
    lane = lax.broadcasted_iota(jnp.int32, logits.shape, 1)
    neg = -jnp.inf
    gl = jnp.where(lane < N_GROUPS, logits, neg)
    gmax = jnp.max(gl, axis=-1, keepdims=True)
    g_sel = jnp.min(jnp.where(gl == gmax, lane, LANES), axis=-1, keepdims=True)
    g_w = 1.0 / jnp.sum(jnp.where(lane < N_GROUPS, jnp.exp(logits - gmax), 0.0), axis=-1, keepdims=True)
    lo = N_GROUPS + g_sel * EXPERTS_PER_GROUP
    el = jnp.where(jnp.logical_and(lane >= lo, lane < lo + EXPERTS_PER_GROUP), logits, neg)
    v1 = jnp.max(el, axis=-1, keepdims=True)
    i1 = jnp.min(jnp.where(el == v1, lane, LANES), axis=-1, keepdims=True)
    el2 = jnp.where(lane == i1, neg, el)
    v2 = jnp.max(el2, axis=-1, keepdims=True)
    i2 = jnp.min(jnp.where(el2 == v2, lane, LANES), axis=-1, keepdims=True)
    e2 = jnp.exp(v2 - v1)
    p1 = 1.0 / (1.0 + e2)
    p2 = e2 / (1.0 + e2)
    id_ref[...] = jnp.where(lane == 0, i1 - N_GROUPS, jnp.where(lane == 1, i2 - N_GROUPS, 0))
    wt_ref[...] = jnp.where(lane == 0, p1 * g_w, jnp.where(lane == 1, p2 * g_w, 0.0))


def _router(x1, w_r, b_r, bm):
    m = x1.shape[0]
    bm = min(bm, m)
    return pl.pallas_call(
        _router_kernel,
        out_shape=(jax.ShapeDtypeStruct((m, LANES), jnp.int32), jax.ShapeDtypeStruct((m, LANES), F32)),
        grid=(m // bm,),
        in_specs=[pl.BlockSpec((bm, D_MODEL), lambda i: (i, 0)),
                  pl.BlockSpec((D_MODEL, LANES), lambda i: (0, 0)),
                  pl.BlockSpec((1, LANES), lambda i: (0, 0))],
        out_specs=(pl.BlockSpec((bm, LANES), lambda i: (i, 0)), pl.BlockSpec((bm, LANES), lambda i: (i, 0))),
        compiler_params=_cparams(("arbitrary",)),
        name="router",
    )(x1, w_r, b_r)


def _start_row_gather(idx_ref, first, stride, n, src_ref, dst_ref, sem, both_queues=False, dst_first=0):
    def copy(r):
        return pltpu.make_async_copy(src_ref.at[idx_ref[first + stride * r]], dst_ref.at[dst_first + r], sem)

    if both_queues:
        def start2(r2, c):
            copy(2 * r2).start(priority=0)
            copy(2 * r2 + 1).start(priority=1)
            return c
        lax.fori_loop(0, n // 2, start2, 0, unroll=4)
    else:
        def start(r, c):
            copy(r).start()
            return c
        lax.fori_loop(0, n, start, 0, unroll=8)


def _wait_row_gather(n, src_ref, dst_ref, sem):
    pltpu.make_async_copy(src_ref.at[pl.ds(0, n)], dst_ref, sem).wait()


ROW_LOOKAHEAD = 3
ROW_SLOTS = ROW_LOOKAHEAD + 1


def _expert_kernel(tok_ref, be_ref, nv_ref, run_ref, nxt_ref, x_ref, w1_ref, w3_ref, w2_ref, o_ref,
                   xbuf, sem, w1s, w3s, w2s, wsem, w1b, w3b, w2b, *, tm, l):
    i = pl.program_id(0)
    n_valid = nv_ref[0]
    e = be_ref[i]
    prev = be_ref[jnp.maximum(i - 1, 0)]
    slot = lax.rem(i, ROW_SLOTS)
    wslot = lax.rem(run_ref[i], 2)
    run_start = jnp.logical_and(i < n_valid, jnp.logical_or(i == 0, e != prev))

    def weight_copies(expert, s):
        return (pltpu.make_async_copy(w1_ref.at[l, expert], w1s.at[s], wsem.at[s]),
                pltpu.make_async_copy(w3_ref.at[l, expert], w3s.at[s], wsem.at[s]),
                pltpu.make_async_copy(w2_ref.at[l, expert], w2s.at[s], wsem.at[s]))

    @pl.when(i == 0)
    def _():
        for c in weight_copies(e, 0):
            c.start(priority=1)
        for b in range(ROW_LOOKAHEAD):
            @pl.when(b < n_valid)
            def _(b=b):
                _start_row_gather(tok_ref, b * tm, 1, tm, x_ref, xbuf.at[b], sem.at[b])

    @pl.when(i + ROW_LOOKAHEAD < n_valid)
    def _():
        ahead = lax.rem(i + ROW_LOOKAHEAD, ROW_SLOTS)
        _start_row_gather(tok_ref, (i + ROW_LOOKAHEAD) * tm, 1, tm, x_ref, xbuf.at[ahead], sem.at[ahead])

    @pl.when(run_start)
    def _():
        for c in weight_copies(e, wslot):
            c.wait()
        nxt = nxt_ref[i]

        @pl.when(nxt >= 0)
        def _():
            for c in weight_copies(nxt, 1 - wslot):
                c.start(priority=1)

        w1b[...] = w1s[wslot].astype(BF16)
        w3b[...] = w3s[wslot].astype(BF16)
        w2b[...] = w2s[wslot].astype(BF16)

    @pl.when(i < n_valid)
    def _():
        _wait_row_gather(tm, x_ref, xbuf.at[slot], sem.at[slot])
        x = xbuf[slot].reshape(tm, D_MODEL)
        a = _dot(x, w1b[...])
        h = (a * jax.nn.sigmoid(a)) * _dot(x, w3b[...])
        y = _dot(h.astype(BF16), w2b[...])
        o_ref[...] = y.astype(BF16).reshape(o_ref.shape)

    @pl.when(i >= n_valid)
    def _():
        o_ref[...] = jnp.zeros_like(o_ref)


def _experts(plan, x3, w1, w3, w2, l, tm):
    n = plan["row_tok"].shape[0]
    any_spec = pl.BlockSpec(memory_space=pl.ANY)
    grid_spec = pltpu.PrefetchScalarGridSpec(
        num_scalar_prefetch=5,
        grid=(n // tm,),
        in_specs=[any_spec, any_spec, any_spec, any_spec],
        out_specs=pl.BlockSpec((tm, ROW_TILES, LANES), lambda i, *_: (i, 0, 0)),
        scratch_shapes=[pltpu.VMEM((ROW_SLOTS, tm, ROW_TILES, LANES), BF16),
                        pltpu.SemaphoreType.DMA((ROW_SLOTS,)),
                        pltpu.VMEM((2, D_MODEL, D_EXPERT), F32),
                        pltpu.VMEM((2, D_MODEL, D_EXPERT), F32),
                        pltpu.VMEM((2, D_EXPERT, D_MODEL), F32),
                        pltpu.SemaphoreType.DMA((2,)),
                        pltpu.VMEM((D_MODEL, D_EXPERT), BF16),
                        pltpu.VMEM((D_MODEL, D_EXPERT), BF16),
                        pltpu.VMEM((D_EXPERT, D_MODEL), BF16)],
    )
    return pl.pallas_call(
        functools.partial(_expert_kernel, tm=tm, l=l),
        out_shape=jax.ShapeDtypeStruct((n, ROW_TILES, LANES), BF16),
        grid_spec=grid_spec,
        compiler_params=_cparams(("arbitrary",), 52),
        name="experts",
    )(plan["row_tok"], plan["blk_e"], plan["n_valid"], plan["run_id"], plan["next_e"], x3, w1, w3, w2)


def _combine_kernel(pos_ref, yb_ref, x_ref, wt_ref, g_ref, b_ref, o_ref, ob_ref, buf, sem, *, tm, alpha):
    i = pl.program_id(0)
    slot = lax.rem(i, 2)

    def start_tile(tile, s):
        for k in range(TOP_K):
            _start_row_gather(pos_ref, TOP_K * tile * tm + k, TOP_K, tm, yb_ref, buf.at[s, k], sem.at[s],
                              both_queues=True)

    @pl.when(i == 0)
    def _():
        start_tile(0, 0)

    @pl.when(i + 1 < pl.num_programs(0))
    def _():
        start_tile(i + 1, 1 - slot)

    for k in range(TOP_K):
        _wait_row_gather(tm, yb_ref, buf.at[slot, k], sem.at[slot])
    wt = wt_ref[...]
    h = None
    for k in range(TOP_K):
        term = buf[slot, k].reshape(tm, D_MODEL).astype(F32) * wt[:, k:k + 1]
        h = term if h is None else h + term
    y = _layer_norm_rows(alpha * x_ref[...] + h, g_ref[...], b_ref[...])
    o_ref[...] = y
    ob_ref[...] = y.astype(BF16)


def _combine_ln(pos, yb, x1, wts, g, b, alpha, tm):
    m = x1.shape[0]
    tm = min(tm, m)
    row = lambda i, pos: (i, 0)
    grid_spec = pltpu.PrefetchScalarGridSpec(
        num_scalar_prefetch=1,
        grid=(m // tm,),
        in_specs=[pl.BlockSpec(memory_space=pl.ANY),
                  pl.BlockSpec((tm, D_MODEL), row),
                  pl.BlockSpec((tm, LANES), row),
                  pl.BlockSpec((1, D_MODEL), lambda i, pos: (0, 0)),
                  pl.BlockSpec((1, D_MODEL), lambda i, pos: (0, 0))],
        out_specs=(pl.BlockSpec((tm, D_MODEL), row), pl.BlockSpec((tm, D_MODEL), row)),
        scratch_shapes=[pltpu.VMEM((2, TOP_K, tm, ROW_TILES, LANES), BF16), pltpu.SemaphoreType.DMA((2,))],
    )
    return pl.pallas_call(
        functools.partial(_combine_kernel, tm=tm, alpha=alpha),
        out_shape=(jax.ShapeDtypeStruct((m, D_MODEL), F32), jax.ShapeDtypeStruct((m, D_MODEL), BF16)),
        grid_spec=grid_spec,
        compiler_params=_cparams(("arbitrary",)),
        name="combine_ln",
    )(pos, yb, x1, wts, g, b)


def _dispatch_plan(ids, tm):
    t = ids.shape[0]
    flat_e = ids.reshape(-1)
    n_assign = flat_e.shape[0]
    onehot = (flat_e[:, None] == jnp.arange(N_EXPERTS, dtype=jnp.int32)[None, :]).astype(jnp.int32)
    csum = jnp.cumsum(onehot, axis=0)
    rank = jnp.sum((csum - onehot) * onehot, axis=1)
    counts = csum[-1]
    padded = (counts + tm - 1) // tm * tm
    pend = jnp.cumsum(padded)
    pstart = pend - padded
    dest = jnp.sum(onehot * pstart[None, :], axis=1) + rank
    n_rows = n_assign + N_EXPERTS * tm
    n_blocks = n_rows // tm
    flat_tok = jnp.arange(n_assign, dtype=jnp.int32) // TOP_K
    row_tok = (jnp.arange(n_rows, dtype=jnp.int32) % t).at[dest].set(flat_tok)
    blk_start = jnp.arange(n_blocks, dtype=jnp.int32) * tm
    blk_e = jnp.minimum(jnp.sum((pend[None, :] <= blk_start[:, None]).astype(jnp.int32), axis=1),
                        N_EXPERTS - 1).astype(jnp.int32)
    n_valid = (pend[-1] // tm).astype(jnp.int32)
    blk = jnp.arange(n_blocks, dtype=jnp.int32)
    is_start = jnp.logical_and(blk < n_valid, jnp.logical_or(blk == 0, blk_e != jnp.roll(blk_e, 1)))
    run_id = jnp.cumsum(is_start.astype(jnp.int32)) - 1
    start_pos = jnp.where(is_start, blk, n_blocks)
    next_start = lax.cummin(jnp.concatenate([start_pos[1:], jnp.full((1,), n_blocks, jnp.int32)]), reverse=True)
    next_e = jnp.where(next_start < n_blocks, blk_e[jnp.minimum(next_start, n_blocks - 1)], -1)
    return dict(row_tok=row_tok, blk_e=blk_e, n_valid=n_valid.reshape(1), run_id=run_id.astype(jnp.int32),
                next_e=next_e.astype(jnp.int32), dest=dest.astype(jnp.int32))


def _rope_tables(s):
    half = HEAD_DIM // 2
    inv = ROPE_BASE ** (-jnp.arange(half, dtype=F32) / half)
    ang = jnp.arange(s, dtype=F32)[:, None] * inv[None, :]
    cos, sin = jnp.cos(ang), jnp.sin(ang)
    return jnp.concatenate([cos, cos], axis=1), jnp.concatenate([-sin, sin], axis=1)


def kernel(x, w_in, fox_forget_bias, gla_gate_up, gla_gate_bias, w_branch, w_gate, b_gate, w_out, ln1_g, ln1_b, w_group, b_group, w_expert_router, b_expert_router, w1, w3, w2, ln2_g, ln2_b):
    bsz, s, d = x.shape
    depth = w_in.shape[0]
    alpha = DEEPNORM_ALPHA
    assert bsz == 1 and d == D_MODEL and w_in.shape[2] == IN_WIDTH
    t = bsz * s
    tm_moe = 256

    w_in_b = w_in.astype(BF16)
    wo_b = w_out.astype(BF16)
    fbias = jnp.pad(fox_forget_bias, ((0, 0), (0, LANES - N_HEADS)))[:, None, :]
    up_pad = jnp.pad(gla_gate_up, ((0, 0), (S_GA, LANES - S_GA - GLA_GATE_RANK), (0, 0)))
    w_r = jnp.pad(jnp.concatenate([w_group, w_expert_router], axis=2),
                  ((0, 0), (0, 0), (0, LANES - N_GROUPS - N_EXPERTS)))
    b_r = jnp.pad(jnp.concatenate([b_group, b_expert_router], axis=1),
                  ((0, 0), (0, LANES - N_GROUPS - N_EXPERTS)))[:, None, :]
    cos_t, sin_t = _rope_tables(s)

    xf = x.reshape(t, d)
    xb = xf.astype(BF16)
    for l in range(depth):
        pm = _proj_main(xb, w_in_b, l, 2048)
        ps = _proj_small(xb, w_in_b, l, 1024)
        fcol = _forget_cumsum(ps, fbias[l], 512)
        y_fox = _fox_attention(pm, fcol, 512, 1024)
        y_ret = _retention(pm, cos_t, sin_t, 256)
        y_sb = _sb_attention(pm, 256)
        y_gla = _gla(pm, ps, up_pad[l], gla_gate_bias[l][None, :], 256)
        merged = _merge(xb, (y_fox, y_ret, y_sb, y_gla), w_gate, b_gate[l][:, None, :], w_branch, l, 1024, 256)
        x1, x1b3 = _out_ln(merged, wo_b, l, xf, ln1_g[l][None, :], ln1_b[l][None, :], alpha, 512)

        ids, wts = _router(x1, w_r[l], b_r[l], 512)
        plan = _dispatch_plan(ids[:, :TOP_K], tm_moe)
        yb3 = _experts(plan, x1b3, w1, w3, w2, l, tm_moe)
        xf, xb = _combine_ln(plan["dest"], yb3, x1, wts, ln2_g[l][None, :], ln2_b[l][None, :], alpha, 256)
    return xf.reshape(bsz, s, d)
```

```python
import functools
import math

import jax
import jax.numpy as jnp
import numpy as np
from jax import lax
from jax.experimental import pallas as pl
from jax.experimental.pallas import tpu as pltpu

F32 = jnp.float32
BF16 = jnp.bfloat16

D_MODEL = 2048
CHUNK = 64
N_HEADS = 4
HEAD_DIM = 128
BRANCH_W = N_HEADS * HEAD_DIM
GLA_DK = 64
GLA_KW = N_HEADS * GLA_DK
GLA_GATE_RANK = 16
GLA_GATE_NORM = 16.0
N_BRANCH = 4
N_GROUPS = 4
EXPERTS_PER_GROUP = 8
N_EXPERTS = N_GROUPS * EXPERTS_PER_GROUP
TOP_K = 2
D_EXPERT = D_MODEL // 4
ROPE_BASE = 10000.0
LN_EPS = 1e-5
DEPTH = 4
DEEPNORM_ALPHA = (2.0 * DEPTH) ** 0.25

LANES = 128
ROW_TILES = D_MODEL // LANES
NEG_BIG = -1e30
SB_EXIT = -104.0

C_FQ, C_FK, C_FV = 0, 512, 1024
C_RQ, C_RK, C_RV, C_RG = 1536, 2048, 2560, 3072
C_SQ, C_SK, C_SV = 3584, 4096, 4608
C_GQ, C_GK, C_GV, C_GR = 5120, 5376, 5632, 6144
MAIN_W = 6656
O_FF, O_GA = 1536, 6660
IN_WIDTH = 6676
S_FF, S_GA = 0, 4


def _cparams(sem, vmem_mb=None):
    kw = dict(dimension_semantics=sem)
    if vmem_mb is not None:
        kw["vmem_limit_bytes"] = vmem_mb * 1024 * 1024
    return pltpu.CompilerParams(**kw)


def _log_sigmoid_parts(z):
    t = jnp.log(1.0 + jnp.exp(-jnp.abs(z)))
    return jnp.minimum(z, 0.0) - t, -(jnp.maximum(z, 0.0) + t)


def _split3(x):
    h1 = x.astype(BF16)
    r1 = x - h1.astype(F32)
    h2 = r1.astype(BF16)
    h3 = (r1 - h2.astype(F32)).astype(BF16)
    return h1, h2, h3


def _dot(a, b):
    return jnp.dot(a, b, preferred_element_type=F32)


def _dot_nt(a, b):
    return lax.dot_general(a, b, (((1,), (1,)), ((), ())), preferred_element_type=F32)


def _dot_tn(a, b):
    return lax.dot_general(a, b, (((0,), (0,)), ((), ())), preferred_element_type=F32)


PROJ_BN = 512
N_ALIGNED_TILES = O_FF // PROJ_BN


def _proj_kernel(x_ref, wa_ref, wb_ref, o_ref, w_scr):
    j = pl.program_id(0)
    first_row_tile = pl.program_id(1) == 0

    @pl.when(jnp.logical_and(first_row_tile, j < N_ALIGNED_TILES))
    def _():
        w_scr[...] = wa_ref[0]

    @pl.when(jnp.logical_and(first_row_tile, j >= N_ALIGNED_TILES))
    def _():
        ab = jnp.concatenate([wa_ref[0], wb_ref[0]], axis=1).astype(F32)
        width = PROJ_BN + LANES
        w_scr[...] = pltpu.roll(ab, width - N_HEADS, 1)[:, :PROJ_BN].astype(BF16)

    o_ref[...] = _dot(x_ref[...], w_scr[...]).astype(o_ref.dtype)


def _proj_main(xb, w_in, l, bm):
    m = xb.shape[0]
    bm = min(bm, m)
    lanes_per_tile = PROJ_BN // LANES
    return pl.pallas_call(
        _proj_kernel,
        out_shape=jax.ShapeDtypeStruct((m, MAIN_W), BF16),
        grid=(MAIN_W // PROJ_BN, m // bm),
        in_specs=[pl.BlockSpec((bm, D_MODEL), lambda j, i: (i, 0)),
                  pl.BlockSpec((1, D_MODEL, PROJ_BN), lambda j, i: (l, 0, j)),
                  pl.BlockSpec((1, D_MODEL, LANES), lambda j, i: (l, 0, lanes_per_tile * (j + 1)))],
        out_specs=pl.BlockSpec((bm, PROJ_BN), lambda j, i: (i, j)),
        scratch_shapes=[pltpu.VMEM((D_MODEL, PROJ_BN), BF16)],
        compiler_params=_cparams(("arbitrary", "arbitrary"), 48),
        name="proj_main",
    )(xb, w_in, w_in)


FOX_AUG = 2 * HEAD_DIM
LOG2E = 1.4426950408889634


def _fox_prep_kernel(x_ref, wf_ref, wg_ref, bias_ref, tri_ref, q_ref, k_ref, v_ref,
                     ps_ref, qa_ref, ka_ref, va_ref, st_ref, carry_ref, f_ref):
    @pl.when(pl.program_id(0) == 0)
    def _():
        carry_ref[...] = jnp.zeros_like(carry_ref)

    wlane = lax.broadcasted_iota(jnp.int32, (1, LANES), 1)
    w = jnp.where(wlane < S_GA, wf_ref[0].astype(F32),
                  jnp.where(wlane < S_GA + GLA_GATE_RANK, wg_ref[0].astype(F32), 0.0))
    ps = _dot(x_ref[...], w.astype(BF16))
    ps_ref[...] = ps
    lf, _ = _log_sigmoid_parts(ps + bias_ref[...])
    tri = tri_ref[...]
    t1, t2, t3 = _split3(lf)
    f_all = _dot(tri, t1) + _dot(tri, t2) + _dot(tri, t3) + carry_ref[0:1, :]
    f_ref[...] = f_all
    carry_ref[...] = jnp.broadcast_to(f_all[-1:, :], carry_ref.shape)

    tb = q_ref.shape[0]
    lane = lax.broadcasted_iota(jnp.int32, (tb, HEAD_DIM), 1)
    lane1 = lax.broadcasted_iota(jnp.int32, (1, LANES), 1)
    inv_scale = HEAD_DIM ** 0.5
    c1 = HEAD_DIM ** -0.5 * LOG2E
    stats = jnp.zeros((1, LANES), F32)

    def max_norm(x_ref, sl):
        x = x_ref[:, sl].astype(F32)
        n2 = jnp.sum(x * x, axis=-1, keepdims=True)
        return jnp.sqrt(jnp.max(n2, axis=0, keepdims=True))

    for h in range(N_HEADS):
        sl = slice(h * HEAD_DIM, (h + 1) * HEAD_DIM)
        stats = jnp.where(lane1 == h, c1 * max_norm(q_ref, sl), stats)
        stats = jnp.where(lane1 == N_HEADS + h, max_norm(k_ref, sl), stats)
        stats = jnp.where(lane1 == 2 * N_HEADS + h, f_ref[0:1, h:h + 1] * LOG2E, stats)
        stats = jnp.where(lane1 == 3 * N_HEADS + h, f_ref[tb - 1:tb, h:h + 1] * LOG2E, stats)
        f = f_ref[:, h:h + 1] * inv_scale
        h1, h2, h3 = (p.astype(F32) for p in _split3(f))
        aq = jnp.where(lane == 0, h1, jnp.where(lane == 1, h2, jnp.where(lane == 2, h3,
                       jnp.where(lane < 6, 1.0, 0.0)))).astype(BF16)
        ak = jnp.where(lane < 3, 1.0, jnp.where(lane == 3, -h1, jnp.where(lane == 4, -h2,
                       jnp.where(lane == 5, -h3, 0.0)))).astype(BF16)
        qa_ref[:, h * FOX_AUG:h * FOX_AUG + HEAD_DIM] = q_ref[:, sl]
        qa_ref[:, h * FOX_AUG + HEAD_DIM:(h + 1) * FOX_AUG] = aq
        ka_ref[:, h * FOX_AUG:h * FOX_AUG + HEAD_DIM] = k_ref[:, sl]
        ka_ref[:, h * FOX_AUG + HEAD_DIM:(h + 1) * FOX_AUG] = ak
        va_ref[:, h * FOX_AUG:h * FOX_AUG + HEAD_DIM] = v_ref[:, sl]
        va_ref[:, h * FOX_AUG + HEAD_DIM:(h + 1) * FOX_AUG] = jnp.where(lane == 0, 1.0, 0.0).astype(BF16)
    st_ref[...] = jnp.broadcast_to(stats, st_ref.shape)


def _fox_prep(xb, w_in, l, bias_row, pm, tb):
    s = pm.shape[0]
    tb = min(tb, s)
    assert O_FF % LANES == 0 and O_GA % LANES == S_GA
    r = lax.broadcasted_iota(jnp.int32, (tb, tb), 0)
    c = lax.broadcasted_iota(jnp.int32, (tb, tb), 1)
    tri = (c <= r).astype(BF16)
    blk = lambda c: pl.BlockSpec((tb, BRANCH_W), lambda i, c=c: (i, c // BRANCH_W))
    out = pl.BlockSpec((tb, N_HEADS * FOX_AUG), lambda i: (i, 0))
    shp = jax.ShapeDtypeStruct((s, N_HEADS * FOX_AUG), BF16)
    return pl.pallas_call(
        _fox_prep_kernel,
        out_shape=(jax.ShapeDtypeStruct((s, LANES), F32), shp, shp, shp,
                   jax.ShapeDtypeStruct((s // tb * 8, LANES), F32)),
        grid=(s // tb,),
        in_specs=[pl.BlockSpec((tb, D_MODEL), lambda i: (i, 0)),
                  pl.BlockSpec((1, D_MODEL, LANES), lambda i: (l, 0, O_FF // LANES)),
                  pl.BlockSpec((1, D_MODEL, LANES), lambda i: (l, 0, O_GA // LANES)),
                  pl.BlockSpec((1, LANES), lambda i: (0, 0)),
                  pl.BlockSpec((tb, tb), lambda i: (0, 0)),
                  blk(C_FQ), blk(C_FK), blk(C_FV)],
        out_specs=(pl.BlockSpec((tb, LANES), lambda i: (i, 0)), out, out, out,
                   pl.BlockSpec((8, LANES), lambda i: (i, 0))),
        scratch_shapes=[pltpu.VMEM((8, LANES), F32), pltpu.VMEM((tb, LANES), F32)],
        compiler_params=_cparams(("arbitrary",)),
        name="fox_prep",
    )(xb, w_in, w_in, bias_row, tri, pm, pm, pm)


FOX_STATS = 4 * N_HEADS
FOX_SKIP = -160.0


def _fox_kernel(qi_ref, kj_ref, st_ref, q_ref, k_ref, v_ref, o_ref, m_ref, acc_ref, *, tq, tk):
    r = tk // tq
    p_id = pl.program_id(0)
    qi = qi_ref[p_id]
    kj = kj_ref[p_id]
    c1 = HEAD_DIM ** -0.5 * LOG2E
    on_diag = kj == lax.div(qi, r)
    sub = lax.rem(qi, r)

    def tile_is_dead():
        bound = jnp.float32(-jnp.inf)
        m_min = jnp.full((1, 1), jnp.inf, F32)
        last = (kj * r + r - 1) * FOX_STATS
        for h in range(N_HEADS):
            qb = qi * FOX_STATS + h
            kn = st_ref[kj * r * FOX_STATS + N_HEADS + h]
            for b in range(1, r):
                kn = jnp.maximum(kn, st_ref[(kj * r + b) * FOX_STATS + N_HEADS + h])
            bound = jnp.maximum(bound, st_ref[qb] * kn + st_ref[qb + 2 * N_HEADS]
                                - st_ref[last + 3 * N_HEADS + h])
            m_min = jnp.minimum(m_min, jnp.min(m_ref[h], axis=0, keepdims=True))
        return bound - m_min[0, 0] < FOX_SKIP

    def tile(width, keep):
        for h in range(N_HEADS):
            sl = slice(h * FOX_AUG, (h + 1) * FOX_AUG)
            s = _dot_nt(q_ref[:, sl], k_ref[0:width, sl]) * c1
            if keep is not None:
                s = jnp.where(keep, s, NEG_BIG)
            m_prev = m_ref[h]
            m_new = jnp.maximum(m_prev, jnp.max(s, axis=-1, keepdims=True))
            alpha = jnp.exp2(m_prev - m_new)
            p = jnp.exp2(s - m_new)
            acc_ref[h] = alpha * acc_ref[h] + _dot(p.astype(BF16), v_ref[0:width, sl])
            m_ref[h] = m_new

    @pl.when(on_diag)
    def _():
        m_ref[...] = jnp.full_like(m_ref, NEG_BIG)
        acc_ref[...] = jnp.zeros_like(acc_ref)

    for v in range(r):
        @pl.when(jnp.logical_and(on_diag, sub == v))
        def _(v=v):
            width = (v + 1) * tq
            row = lax.broadcasted_iota(jnp.int32, (tq, width), 0)
            col = lax.broadcasted_iota(jnp.int32, (tq, width), 1)
            tile(width, col - v * tq <= row)

    @pl.when(jnp.logical_not(on_diag))
    def _():
        @pl.when(jnp.logical_not(tile_is_dead()))
        def _():
            tile(tk, None)

    @pl.when(kj == 0)
    def _():
        for h in range(N_HEADS):
            acc = acc_ref[h]
            o_ref[:, h * HEAD_DIM:(h + 1) * HEAD_DIM] = (
                acc[:, :HEAD_DIM] / acc[:, HEAD_DIM:HEAD_DIM + 1]).astype(o_ref.dtype)


def _fox_attention(qa, ka, va, st, tq, tk):
    s = qa.shape[0]
    tq = min(tq, s)
    tk = min(tk, s)
    assert tk % tq == 0 and s % tk == 0
    r = tk // tq
    nb = s // tq
    stats = st.reshape(nb, 8, LANES)[:, 0, :FOX_STATS].reshape(nb * FOX_STATS)
    pairs = [(i, j) for i in range(nb) for j in range(i // r, -1, -1)]
    qi_tab = jnp.asarray([p[0] for p in pairs], jnp.int32)
    kj_tab = jnp.asarray([p[1] for p in pairs], jnp.int32)
    w = N_HEADS * FOX_AUG
    grid_spec = pltpu.PrefetchScalarGridSpec(
        num_scalar_prefetch=3,
        grid=(len(pairs),),
        in_specs=[pl.BlockSpec((tq, w), lambda p, qi, kj, st: (qi[p], 0)),
                  pl.BlockSpec((tk, w), lambda p, qi, kj, st: (kj[p], 0)),
                  pl.BlockSpec((tk, w), lambda p, qi, kj, st: (kj[p], 0))],
        out_specs=pl.BlockSpec((tq, BRANCH_W), lambda p, qi, kj, st: (qi[p], 0)),
        scratch_shapes=[pltpu.VMEM((N_HEADS, tq, 1), F32),
                        pltpu.VMEM((N_HEADS, tq, FOX_AUG), F32)],
    )
    return pl.pallas_call(
        functools.partial(_fox_kernel, tq=tq, tk=tk),
        out_shape=jax.ShapeDtypeStruct((s, BRANCH_W), BF16),
        grid_spec=grid_spec,
        compiler_params=_cparams(("arbitrary",), 48),
        name="fox_attention",
    )(qi_tab, kj_tab, stats, qa, ka, va)


def _sb_kernel(q_ref, k_ref, v_ref, gt_ref, o_ref, r_ref, acc_ref, *, t):
    i = pl.program_id(0)
    r_ref[...] = jnp.zeros_like(r_ref)
    acc_ref[...] = jnp.zeros_like(acc_ref)
    scale = HEAD_DIM ** -0.5
    gt = gt_ref[...]

    def block(kb, diagonal):
        k0 = pl.multiple_of(kb * t, t)
        if diagonal:
            keep = (lax.broadcasted_iota(jnp.int32, (t, t), 1)
                    < lax.broadcasted_iota(jnp.int32, (t, t), 0))
        rmax = jnp.full((1, 1), -jnp.inf, F32)
        for h in range(N_HEADS):
            sl = slice(h * HEAD_DIM, (h + 1) * HEAD_DIM)
            z = _dot_nt(q_ref[:, sl], k_ref[pl.ds(k0, t), sl]) * scale
            lsg, lom = _log_sigmoid_parts(z)
            if diagonal:
                lom = jnp.where(keep, lom, 0.0)
            r_prev = r_ref[h]
            after = _dot(lom.astype(BF16), gt) + r_prev
            a = jnp.exp(lsg + after)
            if diagonal:
                a = jnp.where(keep, a, 0.0)
            acc_ref[h] += _dot(a.astype(BF16), v_ref[pl.ds(k0, t), sl])
            r_new = r_prev + jnp.sum(lom, axis=-1, keepdims=True)
            r_ref[h] = r_new
            rmax = jnp.maximum(rmax, jnp.max(r_new, axis=0, keepdims=True))
        return rmax[0, 0]

    def cond(c):
        jj, rmax = c
        return jnp.logical_and(jj <= i, rmax > SB_EXIT)

    def body(c):
        jj, _ = c
        return jj + 1, block(i - jj, False)

    lax.while_loop(cond, body, (jnp.int32(1), block(i, True)))
    for h in range(N_HEADS):
        sl = slice(h * HEAD_DIM, (h + 1) * HEAD_DIM)
        o_ref[:, sl] = acc_ref[h].astype(o_ref.dtype)


def _sb_attention(pm, t):
    s = pm.shape[0]
    t = min(t, s)
    r = lax.broadcasted_iota(jnp.int32, (t, t), 0)
    c = lax.broadcasted_iota(jnp.int32, (t, t), 1)
    gt = (r > c).astype(BF16)
    return pl.pallas_call(
        functools.partial(_sb_kernel, t=t),
        out_shape=jax.ShapeDtypeStruct((s, BRANCH_W), BF16),
        grid=(s // t,),
        in_specs=[pl.BlockSpec((t, BRANCH_W), lambda i: (i, C_SQ // BRANCH_W)),
                  pl.BlockSpec((s, BRANCH_W), lambda i: (0, C_SK // BRANCH_W)),
                  pl.BlockSpec((s, BRANCH_W), lambda i: (0, C_SV // BRANCH_W)),
                  pl.BlockSpec((t, t), lambda i: (0, 0))],
        out_specs=pl.BlockSpec((t, BRANCH_W), lambda i: (i, 0)),
        scratch_shapes=[pltpu.VMEM((N_HEADS, t, 1), F32),
                        pltpu.VMEM((N_HEADS, t, HEAD_DIM), F32)],
        compiler_params=_cparams(("arbitrary",), 48),
        name="sb_attention",
    )(pm, pm, pm, gt)


def _ret_kernel(q_ref, k_ref, v_ref, g_ref, cos_ref, sin_ref, o_ref, st_ref, *, tb):
    @pl.when(pl.program_id(0) == 0)
    def _():
        st_ref[...] = jnp.zeros_like(st_ref)

    cos = cos_ref[...]
    sin = sin_ref[...]
    ri = lax.broadcasted_iota(jnp.int32, (tb, tb), 0)
    ci = lax.broadcasted_iota(jnp.int32, (tb, tb), 1)
    chunk_ok = (ci // CHUNK) <= (ri // CHUNK)
    dist = jnp.abs(ri - ci).astype(F32)
    idx = lax.broadcasted_iota(jnp.int32, (tb, 1), 0).astype(F32)
    scale = HEAD_DIM ** -0.5
    for h in range(N_HEADS):
        sl = slice(h * HEAD_DIM, (h + 1) * HEAD_DIM)
        lg = math.log(1.0 - 2.0 ** (-5.0 - h))
        q = q_ref[:, sl].astype(F32)
        k = k_ref[:, sl].astype(F32)
        v = v_ref[:, sl]
        qr = q * cos + pltpu.roll(q, HEAD_DIM // 2, 1) * sin
        kr = (k * cos + pltpu.roll(k, HEAD_DIM // 2, 1) * sin) * scale
        decay = jnp.where(chunk_ok, jnp.exp(lg * dist), 0.0)
        scores = _dot_nt(qr.astype(BF16), kr.astype(BF16)) * decay
        intra = _dot(scores.astype(BF16), v)
        q_dec = jnp.exp(lg * (idx + 1.0))
        k_dec = jnp.exp(lg * (tb - 1.0 - idx))
        state = st_ref[h]
        inter = _dot((qr * q_dec).astype(BF16), state.astype(BF16))
        kv = _dot_tn((kr * k_dec).astype(BF16), v)
        st_ref[h] = state * math.exp(lg * tb) + kv
        o = intra + inter
        mu = jnp.mean(o, axis=-1, keepdims=True)
        d = o - mu
        var = jnp.mean(d * d, axis=-1, keepdims=True)
        on = d * lax.rsqrt(var + LN_EPS)
        g = g_ref[:, sl].astype(F32)
        o_ref[:, sl] = (on * (g * jax.nn.sigmoid(g))).astype(o_ref.dtype)


def _retention(pm, cos_t, sin_t, tb):
    s = pm.shape[0]
    tb = min(tb, s)
    blk = lambda c: pl.BlockSpec((tb, BRANCH_W), lambda i, c=c: (i, c // BRANCH_W))
    return pl.pallas_call(
        functools.partial(_ret_kernel, tb=tb),
        out_shape=jax.ShapeDtypeStruct((s, BRANCH_W), BF16),
        grid=(s // tb,),
        in_specs=[blk(C_RQ), blk(C_RK), blk(C_RV), blk(C_RG),
                  pl.BlockSpec((tb, HEAD_DIM), lambda i: (i, 0)),
                  pl.BlockSpec((tb, HEAD_DIM), lambda i: (i, 0))],
        out_specs=pl.BlockSpec((tb, BRANCH_W), lambda i: (i, 0)),
        scratch_shapes=[pltpu.VMEM((N_HEADS, HEAD_DIM, HEAD_DIM), F32)],
        compiler_params=_cparams(("arbitrary",)),
        name="retention",
    )(pm, pm, pm, pm, cos_t, sin_t)


def _gla_kernel(q_ref, k_ref, v_ref, g_ref, ps_ref, up_ref, gb_ref, bd_ref, o_ref, st_ref, *, tb):
    @pl.when(pl.program_id(0) == 0)
    def _():
        st_ref[...] = jnp.zeros_like(st_ref)

    pre = _dot(ps_ref[...].astype(BF16), up_ref[...].astype(BF16)) + gb_ref[...]
    la, _ = _log_sigmoid_parts(pre)
    la = la / GLA_GATE_NORM
    bd = bd_ref[...]
    h1, h2, h3 = _split3(la)
    b = _dot(bd, h1) + _dot(bd, h2) + _dot(bd, h3)
    eb = jnp.exp(b)
    ieb = jnp.exp(-b)
    scale = GLA_DK ** -0.5
    q = q_ref[...].astype(F32) * scale
    k = k_ref[...].astype(F32)
    qe = q * eb
    qi = q * ieb
    ke = (k * ieb).astype(BF16)
    kf = (k * eb).astype(BF16)
    lane = lax.broadcasted_iota(jnp.int32, (1, GLA_KW), 1)
    ri = lax.broadcasted_iota(jnp.int32, (tb, tb), 0)
    ci = lax.broadcasted_iota(jnp.int32, (tb, tb), 1)
    same_chunk = (ri // CHUNK) == (ci // CHUNK)
    past = jnp.logical_and(same_chunk, ri >= ci)
    n_chunks = tb // CHUNK
    kd_parts, dec_parts = [], []
    for c in range(n_chunks):
        rs = slice(c * CHUNK, (c + 1) * CHUNK)
        b_last = b[(c + 1) * CHUNK - 1:(c + 1) * CHUNK, :]
        kd_parts.append(k[rs] * jnp.exp(b_last - b[rs]))
        dec_parts.append(jnp.exp(b_last))
    for h in range(N_HEADS):
        hm = jnp.logical_and(lane >= h * GLA_DK, lane < (h + 1) * GLA_DK)
        vs = slice(h * HEAD_DIM, (h + 1) * HEAD_DIM)
        qe_h = jnp.where(hm, qe, 0.0).astype(BF16)
        qi_h = jnp.where(hm, qi, 0.0).astype(BF16)
        scores = jnp.where(past, _dot_nt(qe_h, ke), jnp.where(same_chunk, _dot_nt(qi_h, kf), 0.0))
        intra = _dot(scores.astype(BF16), v_ref[:, vs])
        for c in range(n_chunks):
            rs = slice(c * CHUNK, (c + 1) * CHUNK)
            st = st_ref[h]
            o = intra[rs] + _dot_nt(qe_h[rs], st.astype(BF16))
            kd_h = jnp.where(hm, kd_parts[c], 0.0).astype(BF16)
            st_ref[h] = st * dec_parts[c] + _dot_tn(v_ref[rs, vs], kd_h)
            on = o * lax.rsqrt(jnp.mean(o * o, axis=-1, keepdims=True) + LN_EPS)
            g = g_ref[rs, vs].astype(F32)
            o_ref[rs, vs] = (on * (g * jax.nn.sigmoid(g))).astype(o_ref.dtype)


def _gla(pm, ps, up_pad, gbias, tb):
    s = pm.shape[0]
    tb = min(tb, s)
    r = lax.broadcasted_iota(jnp.int32, (tb, tb), 0)
    c = lax.broadcasted_iota(jnp.int32, (tb, tb), 1)
    bd = jnp.logical_and(c <= r, (c // CHUNK) == (r // CHUNK)).astype(BF16)
    return pl.pallas_call(
        functools.partial(_gla_kernel, tb=tb),
        out_shape=jax.ShapeDtypeStruct((s, BRANCH_W), BF16),
        grid=(s // tb,),
        in_specs=[pl.BlockSpec((tb, GLA_KW), lambda i: (i, C_GQ // GLA_KW)),
                  pl.BlockSpec((tb, GLA_KW), lambda i: (i, C_GK // GLA_KW)),
                  pl.BlockSpec((tb, BRANCH_W), lambda i: (i, C_GV // BRANCH_W)),
                  pl.BlockSpec((tb, BRANCH_W), lambda i: (i, C_GR // BRANCH_W)),
                  pl.BlockSpec((tb, LANES), lambda i: (i, 0)),
                  pl.BlockSpec((LANES, GLA_KW), lambda i: (0, 0)),
                  pl.BlockSpec((1, GLA_KW), lambda i: (0, 0)),
                  pl.BlockSpec((tb, tb), lambda i: (0, 0))],
        out_specs=pl.BlockSpec((tb, BRANCH_W), lambda i: (i, 0)),
        scratch_shapes=[pltpu.VMEM((N_HEADS, HEAD_DIM, GLA_KW), F32)],
        compiler_params=_cparams(("arbitrary",)),
        name="gla",
    )(pm, pm, pm, pm, ps, up_pad, gbias, bd)


def _merge_kernel(x_ref, y0_ref, y1_ref, y2_ref, y3_ref, wg_ref, bg_ref, wb_ref, o_ref, wg_s, wb_s):
    @pl.when(pl.program_id(1) == 0)
    def _():
        for n in range(N_BRANCH):
            wg_s[n] = wg_ref[0, n].astype(BF16)
            wb_s[n] = wb_ref[0, n].astype(BF16)

    x = x_ref[...]
    acc = None
    for n, y_ref in enumerate((y0_ref, y1_ref, y2_ref, y3_ref)):
        gate = jax.nn.sigmoid(_dot(x, wg_s[n]) + bg_ref[n])
        term = gate * _dot(y_ref[...], wb_s[n])
        acc = term if acc is None else acc + term
    o_ref[...] = acc.astype(o_ref.dtype)


def _merge(xb, ys, wg, bg, wb, l, bm, bn):
    m = xb.shape[0]
    bm = min(bm, m)
    yspec = pl.BlockSpec((bm, BRANCH_W), lambda j, i: (i, 0))
    return pl.pallas_call(
        _merge_kernel,
        out_shape=jax.ShapeDtypeStruct((m, D_MODEL), BF16),
        grid=(D_MODEL // bn, m // bm),
        in_specs=[pl.BlockSpec((bm, D_MODEL), lambda j, i: (i, 0)),
                  yspec, yspec, yspec, yspec,
                  pl.BlockSpec((1, N_BRANCH, D_MODEL, bn), lambda j, i: (l, 0, 0, j)),
                  pl.BlockSpec((N_BRANCH, 1, bn), lambda j, i: (0, 0, j)),
                  pl.BlockSpec((1, N_BRANCH, BRANCH_W, bn), lambda j, i: (l, 0, 0, j))],
        out_specs=pl.BlockSpec((bm, bn), lambda j, i: (i, j)),
        scratch_shapes=[pltpu.VMEM((N_BRANCH, D_MODEL, bn), BF16),
                        pltpu.VMEM((N_BRANCH, BRANCH_W, bn), BF16)],
        compiler_params=_cparams(("arbitrary", "arbitrary"), 52),
        name="merge",
    )(xb, *ys, wg, bg, wb)


def _layer_norm_rows(z, g, b):
    mu = jnp.mean(z, axis=-1, keepdims=True)
    d = z - mu
    var = jnp.mean(d * d, axis=-1, keepdims=True)
    return d * lax.rsqrt(var + LN_EPS) * g + b


def _outln_kernel(m_ref, w_ref, x_ref, g_ref, b_ref, o_ref, ob_ref, *, alpha):
    half = m_ref.shape[0] // 2
    for r in range(2):
        rs = slice(r * half, (r + 1) * half)
        h = _dot(m_ref[rs, :], w_ref[0])
        y = _layer_norm_rows(alpha * x_ref[rs, :] + h, g_ref[...], b_ref[...])
        o_ref[rs, :] = y
        ob_ref[rs] = y.astype(BF16).reshape(half, ROW_TILES, LANES)


def _out_ln(merged, w_out, l, x, g, b, alpha, bm):
    m = x.shape[0]
    bm = min(bm, m)
    row = pl.BlockSpec((bm, D_MODEL), lambda i: (i, 0))
    row3 = pl.BlockSpec((bm, ROW_TILES, LANES), lambda i: (i, 0, 0))
    vec = pl.BlockSpec((1, D_MODEL), lambda i: (0, 0))
    return pl.pallas_call(
        functools.partial(_outln_kernel, alpha=alpha),
        out_shape=(jax.ShapeDtypeStruct((m, D_MODEL), F32),
                   jax.ShapeDtypeStruct((m, ROW_TILES, LANES), BF16)),
        grid=(m // bm,),
        in_specs=[row, pl.BlockSpec((1, D_MODEL, D_MODEL), lambda i: (l, 0, 0)), row, vec, vec],
        out_specs=(row, row3),
        compiler_params=_cparams(("arbitrary",), 48),
        name="out_ln",
    )(merged, w_out, x, g, b)


def _router_kernel(x_ref, w_ref, b_ref, id_ref, wt_ref):
    x = x_ref[...]
    xh = x.astype(BF16)
    xl = (x - xh.astype(F32)).astype(BF16)
    w = w_ref[...]
    wh = w.astype(BF16)
    wl = (w - wh.astype(F32)).astype(BF16)
    logits = _dot(xh, wh) + _dot(xh, wl) + _dot(xl, wh) + b_ref[...]
    lane = lax.broadcasted_iota(jnp.int32, logits.shape, 1)
    neg = -jnp.inf
    gl = jnp.where(lane < N_GROUPS, logits, neg)
    gmax = jnp.max(gl, axis=-1, keepdims=True)
    g_sel = jnp.min(jnp.where(gl == gmax, lane, LANES), axis=-1, keepdims=True)
    g_w = 1.0 / jnp.sum(jnp.where(lane < N_GROUPS, jnp.exp(logits - gmax), 0.0), axis=-1, keepdims=True)
    lo = N_GROUPS + g_sel * EXPERTS_PER_GROUP
    el = jnp.where(jnp.logical_and(lane >= lo, lane < lo + EXPERTS_PER_GROUP), logits, neg)
    v1 = jnp.max(el, axis=-1, keepdims=True)
    i1 = jnp.min(jnp.where(el == v1, lane, LANES), axis=-1, keepdims=True)
    el2 = jnp.where(lane == i1, neg, el)
    v2 = jnp.max(el2, axis=-1, keepdims=True)
    i2 = jnp.min(jnp.where(el2 == v2, lane, LANES), axis=-1, keepdims=True)
    e2 = jnp.exp(v2 - v1)
    p1 = 1.0 / (1.0 + e2)
    p2 = e2 / (1.0 + e2)
    id_ref[...] = jnp.where(lane == 0, i1 - N_GROUPS, jnp.where(lane == 1, i2 - N_GROUPS, 0))
    wt_ref[...] = jnp.where(lane == 0, p1 * g_w, jnp.where(lane == 1, p2 * g_w, 0.0))


def _router(x1, w_r, b_r, bm):
    m = x1.shape[0]
    bm = min(bm, m)
    return pl.pallas_call(
        _router_kernel,
        out_shape=(jax.ShapeDtypeStruct((m, LANES), jnp.int32), jax.ShapeDtypeStruct((m, LANES), F32)),
        grid=(m // bm,),
        in_specs=[pl.BlockSpec((bm, D_MODEL), lambda i: (i, 0)),
                  pl.BlockSpec((D_MODEL, LANES), lambda i: (0, 0)),
                  pl.BlockSpec((1, LANES), lambda i: (0, 0))],
        out_specs=(pl.BlockSpec((bm, LANES), lambda i: (i, 0)), pl.BlockSpec((bm, LANES), lambda i: (i, 0))),
        compiler_params=_cparams(("arbitrary",)),
        name="router",
    )(x1, w_r, b_r)


def _start_row_gather(idx_ref, first, stride, n, src_ref, dst_ref, sem, both_queues=False, dst_first=0):
    def copy(r):
        return pltpu.make_async_copy(src_ref.at[idx_ref[first + stride * r]], dst_ref.at[dst_first + r], sem)

    if both_queues:
        def start2(r2, c):
            copy(2 * r2).start(priority=0)
            copy(2 * r2 + 1).start(priority=1)
            return c
        lax.fori_loop(0, n // 2, start2, 0, unroll=4)
    else:
        def start(r, c):
            copy(r).start()
            return c
        lax.fori_loop(0, n, start, 0, unroll=8)


def _wait_row_gather(n, src_ref, dst_ref, sem):
    pltpu.make_async_copy(src_ref.at[pl.ds(0, n)], dst_ref, sem).wait()


ROW_LOOKAHEAD = 3
ROW_SLOTS = ROW_LOOKAHEAD + 1


def _expert_kernel(tok_ref, be_ref, nv_ref, run_ref, nxt_ref, x_ref, w1_ref, w3_ref, w2_ref, o_ref,
                   xbuf, sem, w1s, w3s, w2s, wsem, w1b, w3b, w2b, *, tm, l):
    i = pl.program_id(0)
    n_valid = nv_ref[0]
    e = be_ref[i]
    prev = be_ref[jnp.maximum(i - 1, 0)]
    slot = lax.rem(i, ROW_SLOTS)
    wslot = lax.rem(run_ref[i], 2)
    run_start = jnp.logical_and(i < n_valid, jnp.logical_or(i == 0, e != prev))

    def weight_copies(expert, s):
        return (pltpu.make_async_copy(w1_ref.at[l, expert], w1s.at[s], wsem.at[s]),
                pltpu.make_async_copy(w3_ref.at[l, expert], w3s.at[s], wsem.at[s]),
                pltpu.make_async_copy(w2_ref.at[l, expert], w2s.at[s], wsem.at[s]))

    @pl.when(i == 0)
    def _():
        for c in weight_copies(e, 0):
            c.start(priority=1)
        for b in range(ROW_LOOKAHEAD):
            @pl.when(b < n_valid)
            def _(b=b):
                _start_row_gather(tok_ref, b * tm, 1, tm, x_ref, xbuf.at[b], sem.at[b])

    @pl.when(i + ROW_LOOKAHEAD < n_valid)
    def _():
        ahead = lax.rem(i + ROW_LOOKAHEAD, ROW_SLOTS)
        _start_row_gather(tok_ref, (i + ROW_LOOKAHEAD) * tm, 1, tm, x_ref, xbuf.at[ahead], sem.at[ahead])

    @pl.when(run_start)
    def _():
        for c in weight_copies(e, wslot):
            c.wait()
        nxt = nxt_ref[i]

        @pl.when(nxt >= 0)
        def _():
            for c in weight_copies(nxt, 1 - wslot):
                c.start(priority=1)

        w1b[...] = w1s[wslot].astype(BF16)
        w3b[...] = w3s[wslot].astype(BF16)
        w2b[...] = w2s[wslot].astype(BF16)

    @pl.when(i < n_valid)
    def _():
        _wait_row_gather(tm, x_ref, xbuf.at[slot], sem.at[slot])
        x = xbuf[slot].reshape(tm, D_MODEL)
        a = _dot(x, w1b[...])
        h = (a * jax.nn.sigmoid(a)) * _dot(x, w3b[...])
        y = _dot(h.astype(BF16), w2b[...])
        o_ref[...] = y.astype(BF16).reshape(o_ref.shape)

    @pl.when(i >= n_valid)
    def _():
        o_ref[...] = jnp.zeros_like(o_ref)


def _experts(plan, x3, w1, w3, w2, l, tm):
    n = plan["row_tok"].shape[0]
    any_spec = pl.BlockSpec(memory_space=pl.ANY)
    grid_spec = pltpu.PrefetchScalarGridSpec(
        num_scalar_prefetch=5,
        grid=(n // tm,),
        in_specs=[any_spec, any_spec, any_spec, any_spec],
        out_specs=pl.BlockSpec((tm, ROW_TILES, LANES), lambda i, *_: (i, 0, 0)),
        scratch_shapes=[pltpu.VMEM((ROW_SLOTS, tm, ROW_TILES, LANES), BF16),
                        pltpu.SemaphoreType.DMA((ROW_SLOTS,)),
                        pltpu.VMEM((2, D_MODEL, D_EXPERT), F32),
                        pltpu.VMEM((2, D_MODEL, D_EXPERT), F32),
                        pltpu.VMEM((2, D_EXPERT, D_MODEL), F32),
                        pltpu.SemaphoreType.DMA((2,)),
                        pltpu.VMEM((D_MODEL, D_EXPERT), BF16),
                        pltpu.VMEM((D_MODEL, D_EXPERT), BF16),
                        pltpu.VMEM((D_EXPERT, D_MODEL), BF16)],
    )
    return pl.pallas_call(
        functools.partial(_expert_kernel, tm=tm, l=l),
        out_shape=jax.ShapeDtypeStruct((n, ROW_TILES, LANES), BF16),
        grid_spec=grid_spec,
        compiler_params=_cparams(("arbitrary",), 52),
        name="experts",
    )(plan["row_tok"], plan["blk_e"], plan["n_valid"], plan["run_id"], plan["next_e"], x3, w1, w3, w2)


def _combine_kernel(pos_ref, yb_ref, x_ref, wt_ref, g_ref, b_ref, o_ref, ob_ref, buf, sem, *, tm, alpha):
    i = pl.program_id(0)
    slot = lax.rem(i, 2)

    def start_tile(tile, s):
        for k in range(TOP_K):
            _start_row_gather(pos_ref, TOP_K * tile * tm + k, TOP_K, tm, yb_ref, buf.at[s, k], sem.at[s])

    @pl.when(i == 0)
    def _():
        start_tile(0, 0)

    @pl.when(i + 1 < pl.num_programs(0))
    def _():
        start_tile(i + 1, 1 - slot)

    for k in range(TOP_K):
        _wait_row_gather(tm, yb_ref, buf.at[slot, k], sem.at[slot])
    wt = wt_ref[...]
    h = None
    for k in range(TOP_K):
        term = buf[slot, k].reshape(tm, D_MODEL).astype(F32) * wt[:, k:k + 1]
        h = term if h is None else h + term
    y = _layer_norm_rows(alpha * x_ref[...] + h, g_ref[...], b_ref[...])
    o_ref[...] = y
    ob_ref[...] = y.astype(BF16)


def _combine_ln(pos, yb, x1, wts, g, b, alpha, tm):
    m = x1.shape[0]
    tm = min(tm, m)
    row = lambda i, pos: (i, 0)
    grid_spec = pltpu.PrefetchScalarGridSpec(
        num_scalar_prefetch=1,
        grid=(m // tm,),
        in_specs=[pl.BlockSpec(memory_space=pl.ANY),
                  pl.BlockSpec((tm, D_MODEL), row),
                  pl.BlockSpec((tm, LANES), row),
                  pl.BlockSpec((1, D_MODEL), lambda i, pos: (0, 0)),
                  pl.BlockSpec((1, D_MODEL), lambda i, pos: (0, 0))],
        out_specs=(pl.BlockSpec((tm, D_MODEL), row), pl.BlockSpec((tm, D_MODEL), row)),
        scratch_shapes=[pltpu.VMEM((2, TOP_K, tm, ROW_TILES, LANES), BF16), pltpu.SemaphoreType.DMA((2,))],
    )
    return pl.pallas_call(
        functools.partial(_combine_kernel, tm=tm, alpha=alpha),
        out_shape=(jax.ShapeDtypeStruct((m, D_MODEL), F32), jax.ShapeDtypeStruct((m, D_MODEL), BF16)),
        grid_spec=grid_spec,
        compiler_params=_cparams(("arbitrary",)),
        name="combine_ln",
    )(pos, yb, x1, wts, g, b)


def _dispatch_plan(ids, tm):
    t = ids.shape[0]
    flat_e = ids.reshape(-1)
    n_assign = flat_e.shape[0]
    onehot = (flat_e[:, None] == jnp.arange(N_EXPERTS, dtype=jnp.int32)[None, :]).astype(jnp.int32)
    csum = jnp.cumsum(onehot, axis=0)
    rank = jnp.sum((csum - onehot) * onehot, axis=1)
    counts = csum[-1]
    padded = (counts + tm - 1) // tm * tm
    pend = jnp.cumsum(padded)
    pstart = pend - padded
    dest = jnp.sum(onehot * pstart[None, :], axis=1) + rank
    n_rows = n_assign + N_EXPERTS * tm
    n_blocks = n_rows // tm
    flat_tok = jnp.arange(n_assign, dtype=jnp.int32) // TOP_K
    row_tok = (jnp.arange(n_rows, dtype=jnp.int32) % t).at[dest].set(flat_tok, unique_indices=True)
    blk_start = jnp.arange(n_blocks, dtype=jnp.int32) * tm
    blk_e = jnp.minimum(jnp.sum((pend[None, :] <= blk_start[:, None]).astype(jnp.int32), axis=1),
                        N_EXPERTS - 1).astype(jnp.int32)
    n_valid = (pend[-1] // tm).astype(jnp.int32)
    blk = jnp.arange(n_blocks, dtype=jnp.int32)
    is_start = jnp.logical_and(blk < n_valid, jnp.logical_or(blk == 0, blk_e != jnp.roll(blk_e, 1)))
    run_id = jnp.cumsum(is_start.astype(jnp.int32)) - 1
    start_pos = jnp.where(is_start, blk, n_blocks)
    next_start = lax.cummin(jnp.concatenate([start_pos[1:], jnp.full((1,), n_blocks, jnp.int32)]), reverse=True)
    next_e = jnp.where(next_start < n_blocks, blk_e[jnp.minimum(next_start, n_blocks - 1)], -1)
    return dict(row_tok=row_tok, blk_e=blk_e, n_valid=n_valid.reshape(1), run_id=run_id.astype(jnp.int32),
                next_e=next_e.astype(jnp.int32), dest=dest.astype(jnp.int32))


def _rope_tables(s):
    half = HEAD_DIM // 2
    inv = (np.float32(ROPE_BASE) ** (-np.arange(half, dtype=np.float32) / np.float32(half))).astype(np.float32)
    ang = np.arange(s, dtype=np.float32)[:, None] * inv[None, :]
    cos, sin = np.cos(ang), np.sin(ang)
    return (jnp.asarray(np.concatenate([cos, cos], axis=1), F32),
            jnp.asarray(np.concatenate([-sin, sin], axis=1), F32))


TILES = dict(proj_rows=2048, fox_q=512, fox_k=1024,
             retention_rows=256, sb_rows=256, gla_rows=256, merge_rows=1024, merge_cols=256,
             out_ln_rows=512, router_rows=512, moe_rows=256, combine_rows=256)


def kernel(x, w_in, fox_forget_bias, gla_gate_up, gla_gate_bias, w_branch, w_gate, b_gate, w_out, ln1_g, ln1_b, w_group, b_group, w_expert_router, b_expert_router, w1, w3, w2, ln2_g, ln2_b):
    bsz, s, d = x.shape
    depth = w_in.shape[0]
    alpha = DEEPNORM_ALPHA
    assert bsz == 1 and d == D_MODEL and w_in.shape[2] == IN_WIDTH
    t = bsz * s
    tl = TILES

    w_in_b = w_in.astype(BF16)
    wo_b = w_out.astype(BF16)
    fbias = jnp.pad(fox_forget_bias, ((0, 0), (0, LANES - N_HEADS)))[:, None, :]
    up_pad = jnp.pad(gla_gate_up, ((0, 0), (S_GA, LANES - S_GA - GLA_GATE_RANK), (0, 0)))
    w_r = jnp.pad(jnp.concatenate([w_group, w_expert_router], axis=2),
                  ((0, 0), (0, 0), (0, LANES - N_GROUPS - N_EXPERTS)))
    b_r = jnp.pad(jnp.concatenate([b_group, b_expert_router], axis=1),
                  ((0, 0), (0, LANES - N_GROUPS - N_EXPERTS)))[:, None, :]
    cos_t, sin_t = _rope_tables(s)

    xf = x.reshape(t, d)
    xb = xf.astype(BF16)
    for l in range(depth):
        pm = _proj_main(xb, w_in_b, l, tl["proj_rows"])
        ps, qa, ka, va, st = _fox_prep(xb, w_in_b, l, fbias[l], pm, tl["fox_q"])
        y_fox = _fox_attention(qa, ka, va, st, tl["fox_q"], tl["fox_k"])
        y_ret = _retention(pm, cos_t, sin_t, tl["retention_rows"])
        y_sb = _sb_attention(pm, tl["sb_rows"])
        y_gla = _gla(pm, ps, up_pad[l], gla_gate_bias[l][None, :], tl["gla_rows"])
        merged = _merge(xb, (y_fox, y_ret, y_sb, y_gla), w_gate, b_gate[l][:, None, :], w_branch, l,
                        tl["merge_rows"], tl["merge_cols"])
        x1, x1b3 = _out_ln(merged, wo_b, l, xf, ln1_g[l][None, :], ln1_b[l][None, :], alpha, tl["out_ln_rows"])

        ids, wts = _router(x1, w_r[l], b_r[l], tl["router_rows"])
        plan = _dispatch_plan(ids[:, :TOP_K], tl["moe_rows"])
        yb3 = _experts(plan, x1b3, w1, w3, w2, l, tl["moe_rows"])
        xf, xb = _combine_ln(plan["dest"], yb3, x1, wts, ln2_g[l][None, :], ln2_b[l][None, :], alpha,
                             tl["combine_rows"])
    return xf.reshape(bsz, s, d)
```

```python
import functools
import math

import jax
import jax.numpy as jnp
import numpy as np
from jax import lax
from jax.experimental import pallas as pl
from jax.experimental.pallas import tpu as pltpu

F32 = jnp.float32
BF16 = jnp.bfloat16

D_MODEL = 2048
CHUNK = 64
N_HEADS = 4
HEAD_DIM = 128
BRANCH_W = N_HEADS * HEAD_DIM
GLA_DK = 64
GLA_KW = N_HEADS * GLA_DK
GLA_GATE_RANK = 16
GLA_GATE_NORM = 16.0
N_BRANCH = 4
N_GROUPS = 4
EXPERTS_PER_GROUP = 8
N_EXPERTS = N_GROUPS * EXPERTS_PER_GROUP
TOP_K = 2
D_EXPERT = D_MODEL // 4
ROPE_BASE = 10000.0
LN_EPS = 1e-5
DEPTH = 4
DEEPNORM_ALPHA = (2.0 * DEPTH) ** 0.25

LANES = 128
ROW_TILES = D_MODEL // LANES
NEG_BIG = -1e30
SB_EXIT = -104.0

C_FQ, C_FK, C_FV = 0, 512, 1024
C_RQ, C_RK, C_RV, C_RG = 1536, 2048, 2560, 3072
C_SQ, C_SK, C_SV = 3584, 4096, 4608
C_GQ, C_GK, C_GV, C_GR = 5120, 5376, 5632, 6144
MAIN_W = 6656
O_FF, O_GA = 1536, 6660
IN_WIDTH = 6676
S_FF, S_GA = 0, 4


def _cparams(sem, vmem_mb=None):
    kw = dict(dimension_semantics=sem)
    if vmem_mb is not None:
        kw["vmem_limit_bytes"] = vmem_mb * 1024 * 1024
    return pltpu.CompilerParams(**kw)


def _log_sigmoid_parts(z):
    t = jnp.log(1.0 + jnp.exp(-jnp.abs(z)))
    return jnp.minimum(z, 0.0) - t, -(jnp.maximum(z, 0.0) + t)


def _split3(x):
    h1 = x.astype(BF16)
    r1 = x - h1.astype(F32)
    h2 = r1.astype(BF16)
    h3 = (r1 - h2.astype(F32)).astype(BF16)
    return h1, h2, h3


def _dot(a, b):
    return jnp.dot(a, b, preferred_element_type=F32)


def _dot_nt(a, b):
    return lax.dot_general(a, b, (((1,), (1,)), ((), ())), preferred_element_type=F32)


def _dot_tn(a, b):
    return lax.dot_general(a, b, (((0,), (0,)), ((), ())), preferred_element_type=F32)


PROJ_BN = 512
N_ALIGNED_TILES = O_FF // PROJ_BN


def _proj_kernel(x_ref, wa_ref, wb_ref, o_ref, w_scr):
    j = pl.program_id(0)
    first_row_tile = pl.program_id(1) == 0

    @pl.when(jnp.logical_and(first_row_tile, j < N_ALIGNED_TILES))
    def _():
        w_scr[...] = wa_ref[0]

    @pl.when(jnp.logical_and(first_row_tile, j >= N_ALIGNED_TILES))
    def _():
        ab = jnp.concatenate([wa_ref[0], wb_ref[0]], axis=1).astype(F32)
        width = PROJ_BN + LANES
        w_scr[...] = pltpu.roll(ab, width - N_HEADS, 1)[:, :PROJ_BN].astype(BF16)

    o_ref[...] = _dot(x_ref[...], w_scr[...]).astype(o_ref.dtype)


def _proj_main(xb, w_in, l, bm):
    m = xb.shape[0]
    bm = min(bm, m)
    lanes_per_tile = PROJ_BN // LANES
    return pl.pallas_call(
        _proj_kernel,
        out_shape=jax.ShapeDtypeStruct((m, MAIN_W), BF16),
        grid=(MAIN_W // PROJ_BN, m // bm),
        in_specs=[pl.BlockSpec((bm, D_MODEL), lambda j, i: (i, 0)),
                  pl.BlockSpec((1, D_MODEL, PROJ_BN), lambda j, i: (l, 0, j)),
                  pl.BlockSpec((1, D_MODEL, LANES), lambda j, i: (l, 0, lanes_per_tile * (j + 1)))],
        out_specs=pl.BlockSpec((bm, PROJ_BN), lambda j, i: (i, j)),
        scratch_shapes=[pltpu.VMEM((D_MODEL, PROJ_BN), BF16)],
        compiler_params=_cparams(("arbitrary", "arbitrary"), 48),
        name="proj_main",
    )(xb, w_in, w_in)


FOX_AUG = 2 * HEAD_DIM
LOG2E = 1.4426950408889634


def _fox_prep_kernel(x_ref, wf_ref, wg_ref, bias_ref, tri_ref, q_ref, k_ref, v_ref,
                     ps_ref, qa_ref, ka_ref, va_ref, st_ref, carry_ref, f_ref):
    @pl.when(pl.program_id(0) == 0)
    def _():
        carry_ref[...] = jnp.zeros_like(carry_ref)

    wlane = lax.broadcasted_iota(jnp.int32, (1, LANES), 1)
    w = jnp.where(wlane < S_GA, wf_ref[0].astype(F32),
                  jnp.where(wlane < S_GA + GLA_GATE_RANK, wg_ref[0].astype(F32), 0.0))
    ps = _dot(x_ref[...], w.astype(BF16))
    ps_ref[...] = ps
    lf, _ = _log_sigmoid_parts(ps + bias_ref[...])
    tri = tri_ref[...]
    t1, t2, t3 = _split3(lf)
    f_all = _dot(tri, t1) + _dot(tri, t2) + _dot(tri, t3) + carry_ref[0:1, :]
    f_ref[...] = f_all
    carry_ref[...] = jnp.broadcast_to(f_all[-1:, :], carry_ref.shape)

    tb = q_ref.shape[0]
    lane = lax.broadcasted_iota(jnp.int32, (tb, HEAD_DIM), 1)
    lane1 = lax.broadcasted_iota(jnp.int32, (1, LANES), 1)
    inv_scale = HEAD_DIM ** 0.5
    c1 = HEAD_DIM ** -0.5 * LOG2E
    stats = jnp.zeros((1, LANES), F32)

    def max_norm(x_ref, sl):
        x = x_ref[:, sl].astype(F32)
        n2 = jnp.sum(x * x, axis=-1, keepdims=True)
        return jnp.sqrt(jnp.max(n2, axis=0, keepdims=True))

    for h in range(N_HEADS):
        sl = slice(h * HEAD_DIM, (h + 1) * HEAD_DIM)
        stats = jnp.where(lane1 == h, c1 * max_norm(q_ref, sl), stats)
        stats = jnp.where(lane1 == N_HEADS + h, max_norm(k_ref, sl), stats)
        stats = jnp.where(lane1 == 2 * N_HEADS + h, f_ref[0:1, h:h + 1] * LOG2E, stats)
        stats = jnp.where(lane1 == 3 * N_HEADS + h, f_ref[tb - 1:tb, h:h + 1] * LOG2E, stats)
        f = f_ref[:, h:h + 1] * inv_scale
        h1, h2, h3 = (p.astype(F32) for p in _split3(f))
        aq = jnp.where(lane == 0, h1, jnp.where(lane == 1, h2, jnp.where(lane == 2, h3,
                       jnp.where(lane < 6, 1.0, 0.0)))).astype(BF16)
        ak = jnp.where(lane < 3, 1.0, jnp.where(lane == 3, -h1, jnp.where(lane == 4, -h2,
                       jnp.where(lane == 5, -h3, 0.0)))).astype(BF16)
        qa_ref[:, h * FOX_AUG:h * FOX_AUG + HEAD_DIM] = q_ref[:, sl]
        qa_ref[:, h * FOX_AUG + HEAD_DIM:(h + 1) * FOX_AUG] = aq
        ka_ref[:, h * FOX_AUG:h * FOX_AUG + HEAD_DIM] = k_ref[:, sl]
        ka_ref[:, h * FOX_AUG + HEAD_DIM:(h + 1) * FOX_AUG] = ak
        va_ref[:, h * FOX_AUG:h * FOX_AUG + HEAD_DIM] = v_ref[:, sl]
        va_ref[:, h * FOX_AUG + HEAD_DIM:(h + 1) * FOX_AUG] = jnp.where(lane == 0, 1.0, 0.0).astype(BF16)
    st_ref[...] = jnp.broadcast_to(stats, st_ref.shape)


def _fox_prep(xb, w_in, l, bias_row, pm, tb):
    s = pm.shape[0]
    tb = min(tb, s)
    assert O_FF % LANES == 0 and O_GA % LANES == S_GA
    r = lax.broadcasted_iota(jnp.int32, (tb, tb), 0)
    c = lax.broadcasted_iota(jnp.int32, (tb, tb), 1)
    tri = (c <= r).astype(BF16)
    blk = lambda c: pl.BlockSpec((tb, BRANCH_W), lambda i, c=c: (i, c // BRANCH_W))
    out = pl.BlockSpec((tb, N_HEADS * FOX_AUG), lambda i: (i, 0))
    shp = jax.ShapeDtypeStruct((s, N_HEADS * FOX_AUG), BF16)
    return pl.pallas_call(
        _fox_prep_kernel,
        out_shape=(jax.ShapeDtypeStruct((s, LANES), F32), shp, shp, shp,
                   jax.ShapeDtypeStruct((s // tb * 8, LANES), F32)),
        grid=(s // tb,),
        in_specs=[pl.BlockSpec((tb, D_MODEL), lambda i: (i, 0)),
                  pl.BlockSpec((1, D_MODEL, LANES), lambda i: (l, 0, O_FF // LANES)),
                  pl.BlockSpec((1, D_MODEL, LANES), lambda i: (l, 0, O_GA // LANES)),
                  pl.BlockSpec((1, LANES), lambda i: (0, 0)),
                  pl.BlockSpec((tb, tb), lambda i: (0, 0)),
                  blk(C_FQ), blk(C_FK), blk(C_FV)],
        out_specs=(pl.BlockSpec((tb, LANES), lambda i: (i, 0)), out, out, out,
                   pl.BlockSpec((8, LANES), lambda i: (i, 0))),
        scratch_shapes=[pltpu.VMEM((8, LANES), F32), pltpu.VMEM((tb, LANES), F32)],
        compiler_params=_cparams(("arbitrary",)),
        name="fox_prep",
    )(xb, w_in, w_in, bias_row, tri, pm, pm, pm)


FOX_STATS = 4 * N_HEADS
FOX_SKIP = -160.0


def _fox_kernel(qi_ref, kj_ref, st_ref, q_ref, k_ref, v_ref, o_ref, m_ref, acc_ref, *, tq, tk):
    r = tk // tq
    p_id = pl.program_id(0)
    qi = qi_ref[p_id]
    kj = kj_ref[p_id]
    c1 = HEAD_DIM ** -0.5 * LOG2E
    on_diag = kj == lax.div(qi, r)
    sub = lax.rem(qi, r)

    def tile_is_dead():
        bound = jnp.float32(-jnp.inf)
        m_min = jnp.full((1, 1), jnp.inf, F32)
        last = (kj * r + r - 1) * FOX_STATS
        for h in range(N_HEADS):
            qb = qi * FOX_STATS + h
            kn = st_ref[kj * r * FOX_STATS + N_HEADS + h]
            for b in range(1, r):
                kn = jnp.maximum(kn, st_ref[(kj * r + b) * FOX_STATS + N_HEADS + h])
            bound = jnp.maximum(bound, st_ref[qb] * kn + st_ref[qb + 2 * N_HEADS]
                                - st_ref[last + 3 * N_HEADS + h])
            m_min = jnp.minimum(m_min, jnp.min(m_ref[h], axis=0, keepdims=True))
        return bound - m_min[0, 0] < FOX_SKIP

    def tile(width, keep):
        for h in range(N_HEADS):
            sl = slice(h * FOX_AUG, (h + 1) * FOX_AUG)
            s = _dot_nt(q_ref[:, sl], k_ref[0:width, sl]) * c1
            if keep is not None:
                s = jnp.where(keep, s, NEG_BIG)
            m_prev = m_ref[h]
            m_new = jnp.maximum(m_prev, jnp.max(s, axis=-1, keepdims=True))
            alpha = jnp.exp2(m_prev - m_new)
            p = jnp.exp2(s - m_new)
            acc_ref[h] = alpha * acc_ref[h] + _dot(p.astype(BF16), v_ref[0:width, sl])
            m_ref[h] = m_new

    @pl.when(on_diag)
    def _():
        m_ref[...] = jnp.full_like(m_ref, NEG_BIG)
        acc_ref[...] = jnp.zeros_like(acc_ref)

    for v in range(r):
        @pl.when(jnp.logical_and(on_diag, sub == v))
        def _(v=v):
            width = (v + 1) * tq
            row = lax.broadcasted_iota(jnp.int32, (tq, width), 0)
            col = lax.broadcasted_iota(jnp.int32, (tq, width), 1)
            tile(width, col - v * tq <= row)

    @pl.when(jnp.logical_not(on_diag))
    def _():
        @pl.when(jnp.logical_not(tile_is_dead()))
        def _():
            tile(tk, None)

    @pl.when(kj == 0)
    def _():
        for h in range(N_HEADS):
            acc = acc_ref[h]
            o_ref[:, h * HEAD_DIM:(h + 1) * HEAD_DIM] = (
                acc[:, :HEAD_DIM] / acc[:, HEAD_DIM:HEAD_DIM + 1]).astype(o_ref.dtype)


def _fox_attention(qa, ka, va, st, tq, tk):
    s = qa.shape[0]
    tq = min(tq, s)
    tk = min(tk, s)
    assert tk % tq == 0 and s % tk == 0
    r = tk // tq
    nb = s // tq
    stats = st.reshape(nb, 8, LANES)[:, 0, :FOX_STATS].reshape(nb * FOX_STATS)
    pairs = [(i, j) for i in range(nb) for j in range(i // r, -1, -1)]
    qi_tab = jnp.asarray([p[0] for p in pairs], jnp.int32)
    kj_tab = jnp.asarray([p[1] for p in pairs], jnp.int32)
    w = N_HEADS * FOX_AUG
    grid_spec = pltpu.PrefetchScalarGridSpec(
        num_scalar_prefetch=3,
        grid=(len(pairs),),
        in_specs=[pl.BlockSpec((tq, w), lambda p, qi, kj, st: (qi[p], 0)),
                  pl.BlockSpec((tk, w), lambda p, qi, kj, st: (kj[p], 0)),
                  pl.BlockSpec((tk, w), lambda p, qi, kj, st: (kj[p], 0))],
        out_specs=pl.BlockSpec((tq, BRANCH_W), lambda p, qi, kj, st: (qi[p], 0)),
        scratch_shapes=[pltpu.VMEM((N_HEADS, tq, 1), F32),
                        pltpu.VMEM((N_HEADS, tq, FOX_AUG), F32)],
    )
    return pl.pallas_call(
        functools.partial(_fox_kernel, tq=tq, tk=tk),
        out_shape=jax.ShapeDtypeStruct((s, BRANCH_W), BF16),
        grid_spec=grid_spec,
        compiler_params=_cparams(("arbitrary",), 48),
        name="fox_attention",
    )(qi_tab, kj_tab, stats, qa, ka, va)


def _sb_kernel(q_ref, k_ref, v_ref, gt_ref, o_ref, r_ref, acc_ref, *, t):
    i = pl.program_id(0)
    r_ref[...] = jnp.zeros_like(r_ref)
    acc_ref[...] = jnp.zeros_like(acc_ref)
    scale = HEAD_DIM ** -0.5
    gt = gt_ref[...]

    def block(kb, diagonal):
        k0 = pl.multiple_of(kb * t, t)
        if diagonal:
            keep = (lax.broadcasted_iota(jnp.int32, (t, t), 1)
                    < lax.broadcasted_iota(jnp.int32, (t, t), 0))
        rmax = jnp.full((1, 1), -jnp.inf, F32)
        for h in range(N_HEADS):
            sl = slice(h * HEAD_DIM, (h + 1) * HEAD_DIM)
            z = _dot_nt(q_ref[:, sl], k_ref[pl.ds(k0, t), sl]) * scale
            lsg, lom = _log_sigmoid_parts(z)
            if diagonal:
                lom = jnp.where(keep, lom, 0.0)
            r_prev = r_ref[h]
            after = _dot(lom.astype(BF16), gt) + r_prev
            a = jnp.exp(lsg + after)
            if diagonal:
                a = jnp.where(keep, a, 0.0)
            acc_ref[h] += _dot(a.astype(BF16), v_ref[pl.ds(k0, t), sl])
            r_new = r_prev + jnp.sum(lom, axis=-1, keepdims=True)
            r_ref[h] = r_new
            rmax = jnp.maximum(rmax, jnp.max(r_new, axis=0, keepdims=True))
        return rmax[0, 0]

    def cond(c):
        jj, rmax = c
        return jnp.logical_and(jj <= i, rmax > SB_EXIT)

    def body(c):
        jj, _ = c
        return jj + 1, block(i - jj, False)

    lax.while_loop(cond, body, (jnp.int32(1), block(i, True)))
    for h in range(N_HEADS):
        sl = slice(h * HEAD_DIM, (h + 1) * HEAD_DIM)
        o_ref[:, sl] = acc_ref[h].astype(o_ref.dtype)


def _sb_attention(pm, t):
    s = pm.shape[0]
    t = min(t, s)
    r = lax.broadcasted_iota(jnp.int32, (t, t), 0)
    c = lax.broadcasted_iota(jnp.int32, (t, t), 1)
    gt = (r > c).astype(BF16)
    return pl.pallas_call(
        functools.partial(_sb_kernel, t=t),
        out_shape=jax.ShapeDtypeStruct((s, BRANCH_W), BF16),
        grid=(s // t,),
        in_specs=[pl.BlockSpec((t, BRANCH_W), lambda i: (i, C_SQ // BRANCH_W)),
                  pl.BlockSpec((s, BRANCH_W), lambda i: (0, C_SK // BRANCH_W)),
                  pl.BlockSpec((s, BRANCH_W), lambda i: (0, C_SV // BRANCH_W)),
                  pl.BlockSpec((t, t), lambda i: (0, 0))],
        out_specs=pl.BlockSpec((t, BRANCH_W), lambda i: (i, 0)),
        scratch_shapes=[pltpu.VMEM((N_HEADS, t, 1), F32),
                        pltpu.VMEM((N_HEADS, t, HEAD_DIM), F32)],
        compiler_params=_cparams(("arbitrary",), 48),
        name="sb_attention",
    )(pm, pm, pm, gt)


def _ret_kernel(q_ref, k_ref, v_ref, g_ref, cos_ref, sin_ref, o_ref, st_ref, *, tb):
    @pl.when(pl.program_id(0) == 0)
    def _():
        st_ref[...] = jnp.zeros_like(st_ref)

    cos = cos_ref[...]
    sin = sin_ref[...]
    ri = lax.broadcasted_iota(jnp.int32, (tb, tb), 0)
    ci = lax.broadcasted_iota(jnp.int32, (tb, tb), 1)
    chunk_ok = (ci // CHUNK) <= (ri // CHUNK)
    dist = jnp.abs(ri - ci).astype(F32)
    idx = lax.broadcasted_iota(jnp.int32, (tb, 1), 0).astype(F32)
    scale = HEAD_DIM ** -0.5
    for h in range(N_HEADS):
        sl = slice(h * HEAD_DIM, (h + 1) * HEAD_DIM)
        lg = math.log(1.0 - 2.0 ** (-5.0 - h))
        q = q_ref[:, sl].astype(F32)
        k = k_ref[:, sl].astype(F32)
        v = v_ref[:, sl]
        qr = q * cos + pltpu.roll(q, HEAD_DIM // 2, 1) * sin
        kr = (k * cos + pltpu.roll(k, HEAD_DIM // 2, 1) * sin) * scale
        decay = jnp.where(chunk_ok, jnp.exp(lg * dist), 0.0)
        scores = _dot_nt(qr.astype(BF16), kr.astype(BF16)) * decay
        intra = _dot(scores.astype(BF16), v)
        q_dec = jnp.exp(lg * (idx + 1.0))
        k_dec = jnp.exp(lg * (tb - 1.0 - idx))
        state = st_ref[h]
        inter = _dot((qr * q_dec).astype(BF16), state.astype(BF16))
        kv = _dot_tn((kr * k_dec).astype(BF16), v)
        st_ref[h] = state * math.exp(lg * tb) + kv
        o = intra + inter
        mu = jnp.mean(o, axis=-1, keepdims=True)
        d = o - mu
        var = jnp.mean(d * d, axis=-1, keepdims=True)
        on = d * lax.rsqrt(var + LN_EPS)
        g = g_ref[:, sl].astype(F32)
        o_ref[:, sl] = (on * (g * jax.nn.sigmoid(g))).astype(o_ref.dtype)


def _retention(pm, cos_t, sin_t, tb):
    s = pm.shape[0]
    tb = min(tb, s)
    blk = lambda c: pl.BlockSpec((tb, BRANCH_W), lambda i, c=c: (i, c // BRANCH_W))
    return pl.pallas_call(
        functools.partial(_ret_kernel, tb=tb),
        out_shape=jax.ShapeDtypeStruct((s, BRANCH_W), BF16),
        grid=(s // tb,),
        in_specs=[blk(C_RQ), blk(C_RK), blk(C_RV), blk(C_RG),
                  pl.BlockSpec((tb, HEAD_DIM), lambda i: (i, 0)),
                  pl.BlockSpec((tb, HEAD_DIM), lambda i: (i, 0))],
        out_specs=pl.BlockSpec((tb, BRANCH_W), lambda i: (i, 0)),
        scratch_shapes=[pltpu.VMEM((N_HEADS, HEAD_DIM, HEAD_DIM), F32)],
        compiler_params=_cparams(("arbitrary",)),
        name="retention",
    )(pm, pm, pm, pm, cos_t, sin_t)


def _gla_kernel(q_ref, k_ref, v_ref, g_ref, ps_ref, up_ref, gb_ref, bd_ref, o_ref, st_ref, *, tb):
    @pl.when(pl.program_id(0) == 0)
    def _():
        st_ref[...] = jnp.zeros_like(st_ref)

    pre = _dot(ps_ref[...].astype(BF16), up_ref[...].astype(BF16)) + gb_ref[...]
    la, _ = _log_sigmoid_parts(pre)
    la = la / GLA_GATE_NORM
    bd = bd_ref[...]
    h1, h2, h3 = _split3(la)
    b = _dot(bd, h1) + _dot(bd, h2) + _dot(bd, h3)
    eb = jnp.exp(b)
    ieb = jnp.exp(-b)
    scale = GLA_DK ** -0.5
    q = q_ref[...].astype(F32) * scale
    k = k_ref[...].astype(F32)
    qe = q * eb
    qi = q * ieb
    ke = (k * ieb).astype(BF16)
    kf = (k * eb).astype(BF16)
    lane = lax.broadcasted_iota(jnp.int32, (1, GLA_KW), 1)
    ri = lax.broadcasted_iota(jnp.int32, (tb, tb), 0)
    ci = lax.broadcasted_iota(jnp.int32, (tb, tb), 1)
    same_chunk = (ri // CHUNK) == (ci // CHUNK)
    past = jnp.logical_and(same_chunk, ri >= ci)
    n_chunks = tb // CHUNK
    kd_parts, dec_parts = [], []
    for c in range(n_chunks):
        rs = slice(c * CHUNK, (c + 1) * CHUNK)
        b_last = b[(c + 1) * CHUNK - 1:(c + 1) * CHUNK, :]
        kd_parts.append(k[rs] * jnp.exp(b_last - b[rs]))
        dec_parts.append(jnp.exp(b_last))
    for h in range(N_HEADS):
        hm = jnp.logical_and(lane >= h * GLA_DK, lane < (h + 1) * GLA_DK)
        vs = slice(h * HEAD_DIM, (h + 1) * HEAD_DIM)
        qe_h = jnp.where(hm, qe, 0.0).astype(BF16)
        qi_h = jnp.where(hm, qi, 0.0).astype(BF16)
        scores = jnp.where(past, _dot_nt(qe_h, ke), jnp.where(same_chunk, _dot_nt(qi_h, kf), 0.0))
        intra = _dot(scores.astype(BF16), v_ref[:, vs])
        for c in range(n_chunks):
            rs = slice(c * CHUNK, (c + 1) * CHUNK)
            st = st_ref[h]
            o = intra[rs] + _dot_nt(qe_h[rs], st.astype(BF16))
            kd_h = jnp.where(hm, kd_parts[c], 0.0).astype(BF16)
            st_ref[h] = st * dec_parts[c] + _dot_tn(v_ref[rs, vs], kd_h)
            on = o * lax.rsqrt(jnp.mean(o * o, axis=-1, keepdims=True) + LN_EPS)
            g = g_ref[rs, vs].astype(F32)
            o_ref[rs, vs] = (on * (g * jax.nn.sigmoid(g))).astype(o_ref.dtype)


def _gla(pm, ps, up_pad, gbias, tb):
    s = pm.shape[0]
    tb = min(tb, s)
    r = lax.broadcasted_iota(jnp.int32, (tb, tb), 0)
    c = lax.broadcasted_iota(jnp.int32, (tb, tb), 1)
    bd = jnp.logical_and(c <= r, (c // CHUNK) == (r // CHUNK)).astype(BF16)
    return pl.pallas_call(
        functools.partial(_gla_kernel, tb=tb),
        out_shape=jax.ShapeDtypeStruct((s, BRANCH_W), BF16),
        grid=(s // tb,),
        in_specs=[pl.BlockSpec((tb, GLA_KW), lambda i: (i, C_GQ // GLA_KW)),
                  pl.BlockSpec((tb, GLA_KW), lambda i: (i, C_GK // GLA_KW)),
                  pl.BlockSpec((tb, BRANCH_W), lambda i: (i, C_GV // BRANCH_W)),
                  pl.BlockSpec((tb, BRANCH_W), lambda i: (i, C_GR // BRANCH_W)),
                  pl.BlockSpec((tb, LANES), lambda i: (i, 0)),
                  pl.BlockSpec((LANES, GLA_KW), lambda i: (0, 0)),
                  pl.BlockSpec((1, GLA_KW), lambda i: (0, 0)),
                  pl.BlockSpec((tb, tb), lambda i: (0, 0))],
        out_specs=pl.BlockSpec((tb, BRANCH_W), lambda i: (i, 0)),
        scratch_shapes=[pltpu.VMEM((N_HEADS, HEAD_DIM, GLA_KW), F32)],
        compiler_params=_cparams(("arbitrary",)),
        name="gla",
    )(pm, pm, pm, pm, ps, up_pad, gbias, bd)


def _merge_kernel(x_ref, y0_ref, y1_ref, y2_ref, y3_ref, wg_ref, bg_ref, wb_ref, o_ref, wg_s, wb_s):
    @pl.when(pl.program_id(1) == 0)
    def _():
        for n in range(N_BRANCH):
            wg_s[n] = wg_ref[0, n].astype(BF16)
            wb_s[n] = wb_ref[0, n].astype(BF16)

    x = x_ref[...]
    acc = None
    for n, y_ref in enumerate((y0_ref, y1_ref, y2_ref, y3_ref)):
        gate = jax.nn.sigmoid(_dot(x, wg_s[n]) + bg_ref[n])
        term = gate * _dot(y_ref[...], wb_s[n])
        acc = term if acc is None else acc + term
    o_ref[...] = acc.astype(o_ref.dtype)


def _merge(xb, ys, wg, bg, wb, l, bm, bn):
    m = xb.shape[0]
    bm = min(bm, m)
    yspec = pl.BlockSpec((bm, BRANCH_W), lambda j, i: (i, 0))
    return pl.pallas_call(
        _merge_kernel,
        out_shape=jax.ShapeDtypeStruct((m, D_MODEL), BF16),
        grid=(D_MODEL // bn, m // bm),
        in_specs=[pl.BlockSpec((bm, D_MODEL), lambda j, i: (i, 0)),
                  yspec, yspec, yspec, yspec,
                  pl.BlockSpec((1, N_BRANCH, D_MODEL, bn), lambda j, i: (l, 0, 0, j)),
                  pl.BlockSpec((N_BRANCH, 1, bn), lambda j, i: (0, 0, j)),
                  pl.BlockSpec((1, N_BRANCH, BRANCH_W, bn), lambda j, i: (l, 0, 0, j))],
        out_specs=pl.BlockSpec((bm, bn), lambda j, i: (i, j)),
        scratch_shapes=[pltpu.VMEM((N_BRANCH, D_MODEL, bn), BF16),
                        pltpu.VMEM((N_BRANCH, BRANCH_W, bn), BF16)],
        compiler_params=_cparams(("arbitrary", "arbitrary"), 52),
        name="merge",
    )(xb, *ys, wg, bg, wb)


def _layer_norm_rows(z, g, b):
    mu = jnp.mean(z, axis=-1, keepdims=True)
    d = z - mu
    var = jnp.mean(d * d, axis=-1, keepdims=True)
    return d * lax.rsqrt(var + LN_EPS) * g + b


def _outln_kernel(m_ref, w_ref, x_ref, g_ref, b_ref, o_ref, ob_ref, *, alpha):
    half = m_ref.shape[0] // 2
    for r in range(2):
        rs = slice(r * half, (r + 1) * half)
        h = _dot(m_ref[rs, :], w_ref[0])
        y = _layer_norm_rows(alpha * x_ref[rs, :] + h, g_ref[...], b_ref[...])
        o_ref[rs, :] = y
        ob_ref[rs] = y.astype(BF16).reshape(half, ROW_TILES, LANES)


def _out_ln(merged, w_out, l, x, g, b, alpha, bm):
    m = x.shape[0]
    bm = min(bm, m)
    row = pl.BlockSpec((bm, D_MODEL), lambda i: (i, 0))
    row3 = pl.BlockSpec((bm, ROW_TILES, LANES), lambda i: (i, 0, 0))
    vec = pl.BlockSpec((1, D_MODEL), lambda i: (0, 0))
    return pl.pallas_call(
        functools.partial(_outln_kernel, alpha=alpha),
        out_shape=(jax.ShapeDtypeStruct((m, D_MODEL), F32),
                   jax.ShapeDtypeStruct((m, ROW_TILES, LANES), BF16)),
        grid=(m // bm,),
        in_specs=[row, pl.BlockSpec((1, D_MODEL, D_MODEL), lambda i: (l, 0, 0)), row, vec, vec],
        out_specs=(row, row3),
        compiler_params=_cparams(("arbitrary",), 48),
        name="out_ln",
    )(merged, w_out, x, g, b)


def _router_kernel(x_ref, w_ref, b_ref, id_ref, wt_ref):
    x = x_ref[...]
    xh = x.astype(BF16)
    xl = (x - xh.astype(F32)).astype(BF16)
    w = w_ref[...]
    wh = w.astype(BF16)
    wl = (w - wh.astype(F32)).astype(BF16)
    logits = _dot(xh, wh) + _dot(xh, wl) + _dot(xl, wh) + b_ref[...]
    lane = lax.broadcasted_iota(jnp.int32, logits.shape, 1)
    neg = -jnp.inf
    gl = jnp.where(lane < N_GROUPS, logits, neg)
    gmax = jnp.max(gl, axis=-1, keepdims=True)
    g_sel = jnp.min(jnp.where(gl == gmax, lane, LANES), axis=-1, keepdims=True)
    g_w = 1.0 / jnp.sum(jnp.where(lane < N_GROUPS, jnp.exp(logits - gmax), 0.0), axis=-1, keepdims=True)
    lo = N_GROUPS + g_sel * EXPERTS_PER_GROUP
    el = jnp.where(jnp.logical_and(lane >= lo, lane < lo + EXPERTS_PER_GROUP), logits, neg)
    v1 = jnp.max(el, axis=-1, keepdims=True)
    i1 = jnp.min(jnp.where(el == v1, lane, LANES), axis=-1, keepdims=True)
    el2 = jnp.where(lane == i1, neg, el)
    v2 = jnp.max(el2, axis=-1, keepdims=True)
    i2 = jnp.min(jnp.where(el2 == v2, lane, LANES), axis=-1, keepdims=True)
    e2 = jnp.exp(v2 - v1)
    p1 = 1.0 / (1.0 + e2)
    p2 = e2 / (1.0 + e2)
    id_ref[...] = jnp.where(lane == 0, i1 - N_GROUPS, jnp.where(lane == 1, i2 - N_GROUPS, 0))
    wt_ref[...] = jnp.where(lane == 0, p1 * g_w, jnp.where(lane == 1, p2 * g_w, 0.0))


def _router(x1, w_r, b_r, bm):
    m = x1.shape[0]
    bm = min(bm, m)
    return pl.pallas_call(
        _router_kernel,
        out_shape=(jax.ShapeDtypeStruct((m, LANES), jnp.int32), jax.ShapeDtypeStruct((m, LANES), F32)),
        grid=(m // bm,),
        in_specs=[pl.BlockSpec((bm, D_MODEL), lambda i: (i, 0)),
                  pl.BlockSpec((D_MODEL, LANES), lambda i: (0, 0)),
                  pl.BlockSpec((1, LANES), lambda i: (0, 0))],
        out_specs=(pl.BlockSpec((bm, LANES), lambda i: (i, 0)), pl.BlockSpec((bm, LANES), lambda i: (i, 0))),
        compiler_params=_cparams(("arbitrary",)),
        name="router",
    )(x1, w_r, b_r)


def _start_row_gather(idx_ref, first, stride, n, src_ref, dst_ref, sem, both_queues=False, dst_first=0):
    def copy(r):
        return pltpu.make_async_copy(src_ref.at[idx_ref[first + stride * r]], dst_ref.at[dst_first + r], sem)

    if both_queues:
        def start2(r2, c):
            copy(2 * r2).start(priority=0)
            copy(2 * r2 + 1).start(priority=1)
            return c
        lax.fori_loop(0, n // 2, start2, 0, unroll=4)
    else:
        def start(r, c):
            copy(r).start()
            return c
        lax.fori_loop(0, n, start, 0, unroll=8)


def _wait_row_gather(n, src_ref, dst_ref, sem):
    pltpu.make_async_copy(src_ref.at[pl.ds(0, n)], dst_ref, sem).wait()


ROW_LOOKAHEAD = 3
ROW_SLOTS = ROW_LOOKAHEAD + 1


def _expert_kernel(tok_ref, be_ref, nv_ref, run_ref, nxt_ref, x_ref, w1_ref, w3_ref, w2_ref, o_ref,
                   xbuf, sem, w1s, w3s, w2s, wsem, w1b, w3b, w2b, *, tm, l):
    i = pl.program_id(0)
    n_valid = nv_ref[0]
    e = be_ref[i]
    prev = be_ref[jnp.maximum(i - 1, 0)]
    slot = lax.rem(i, ROW_SLOTS)
    wslot = lax.rem(run_ref[i], 2)
    run_start = jnp.logical_and(i < n_valid, jnp.logical_or(i == 0, e != prev))

    def weight_copies(expert, s):
        return (pltpu.make_async_copy(w1_ref.at[l, expert], w1s.at[s], wsem.at[s]),
                pltpu.make_async_copy(w3_ref.at[l, expert], w3s.at[s], wsem.at[s]),
                pltpu.make_async_copy(w2_ref.at[l, expert], w2s.at[s], wsem.at[s]))

    @pl.when(i == 0)
    def _():
        for c in weight_copies(e, 0):
            c.start(priority=1)
        for b in range(ROW_LOOKAHEAD):
            @pl.when(b < n_valid)
            def _(b=b):
                _start_row_gather(tok_ref, b * tm, 1, tm, x_ref, xbuf.at[b], sem.at[b])

    @pl.when(i + ROW_LOOKAHEAD < n_valid)
    def _():
        ahead = lax.rem(i + ROW_LOOKAHEAD, ROW_SLOTS)
        _start_row_gather(tok_ref, (i + ROW_LOOKAHEAD) * tm, 1, tm, x_ref, xbuf.at[ahead], sem.at[ahead])

    @pl.when(run_start)
    def _():
        for c in weight_copies(e, wslot):
            c.wait()
        nxt = nxt_ref[i]

        @pl.when(nxt >= 0)
        def _():
            for c in weight_copies(nxt, 1 - wslot):
                c.start(priority=1)

        w1b[...] = w1s[wslot].astype(BF16)
        w3b[...] = w3s[wslot].astype(BF16)
        w2b[...] = w2s[wslot].astype(BF16)

    @pl.when(i < n_valid)
    def _():
        _wait_row_gather(tm, x_ref, xbuf.at[slot], sem.at[slot])
        x = xbuf[slot].reshape(tm, D_MODEL)
        a = _dot(x, w1b[...])
        h = (a * jax.nn.sigmoid(a)) * _dot(x, w3b[...])
        y = _dot(h.astype(BF16), w2b[...])
        o_ref[...] = y.astype(BF16).reshape(o_ref.shape)

    @pl.when(i >= n_valid)
    def _():
        o_ref[...] = jnp.zeros_like(o_ref)


def _experts(plan, x3, w1, w3, w2, l, tm):
    n = plan["row_tok"].shape[0]
    any_spec = pl.BlockSpec(memory_space=pl.ANY)
    grid_spec = pltpu.PrefetchScalarGridSpec(
        num_scalar_prefetch=5,
        grid=(n // tm,),
        in_specs=[any_spec, any_spec, any_spec, any_spec],
        out_specs=pl.BlockSpec((tm, ROW_TILES, LANES), lambda i, *_: (i, 0, 0)),
        scratch_shapes=[pltpu.VMEM((ROW_SLOTS, tm, ROW_TILES, LANES), BF16),
                        pltpu.SemaphoreType.DMA((ROW_SLOTS,)),
                        pltpu.VMEM((2, D_MODEL, D_EXPERT), F32),
                        pltpu.VMEM((2, D_MODEL, D_EXPERT), F32),
                        pltpu.VMEM((2, D_EXPERT, D_MODEL), F32),
                        pltpu.SemaphoreType.DMA((2,)),
                        pltpu.VMEM((D_MODEL, D_EXPERT), BF16),
                        pltpu.VMEM((D_MODEL, D_EXPERT), BF16),
                        pltpu.VMEM((D_EXPERT, D_MODEL), BF16)],
    )
    return pl.pallas_call(
        functools.partial(_expert_kernel, tm=tm, l=l),
        out_shape=jax.ShapeDtypeStruct((n, ROW_TILES, LANES), BF16),
        grid_spec=grid_spec,
        compiler_params=_cparams(("arbitrary",), 52),
        name="experts",
    )(plan["row_tok"], plan["blk_e"], plan["n_valid"], plan["run_id"], plan["next_e"], x3, w1, w3, w2)


def _combine_kernel(pos_ref, yb_ref, x_ref, wt_ref, g_ref, b_ref, o_ref, ob_ref, buf, sem, *, tm, alpha):
    i = pl.program_id(0)
    slot = lax.rem(i, 2)

    def start_tile(tile, s):
        for k in range(TOP_K):
            _start_row_gather(pos_ref, TOP_K * tile * tm + k, TOP_K, tm, yb_ref, buf.at[s, k], sem.at[s])

    @pl.when(i == 0)
    def _():
        start_tile(0, 0)

    @pl.when(i + 1 < pl.num_programs(0))
    def _():
        start_tile(i + 1, 1 - slot)

    for k in range(TOP_K):
        _wait_row_gather(tm, yb_ref, buf.at[slot, k], sem.at[slot])
    wt = wt_ref[...]
    h = None
    for k in range(TOP_K):
        term = buf[slot, k].reshape(tm, D_MODEL).astype(F32) * wt[:, k:k + 1]
        h = term if h is None else h + term
    y = _layer_norm_rows(alpha * x_ref[...] + h, g_ref[...], b_ref[...])
    o_ref[...] = y
    ob_ref[...] = y.astype(BF16)


def _combine_ln(pos, yb, x1, wts, g, b, alpha, tm):
    m = x1.shape[0]
    tm = min(tm, m)
    row = lambda i, pos: (i, 0)
    grid_spec = pltpu.PrefetchScalarGridSpec(
        num_scalar_prefetch=1,
        grid=(m // tm,),
        in_specs=[pl.BlockSpec(memory_space=pl.ANY),
                  pl.BlockSpec((tm, D_MODEL), row),
                  pl.BlockSpec((tm, LANES), row),
                  pl.BlockSpec((1, D_MODEL), lambda i, pos: (0, 0)),
                  pl.BlockSpec((1, D_MODEL), lambda i, pos: (0, 0))],
        out_specs=(pl.BlockSpec((tm, D_MODEL), row), pl.BlockSpec((tm, D_MODEL), row)),
        scratch_shapes=[pltpu.VMEM((2, TOP_K, tm, ROW_TILES, LANES), BF16), pltpu.SemaphoreType.DMA((2,))],
    )
    return pl.pallas_call(
        functools.partial(_combine_kernel, tm=tm, alpha=alpha),
        out_shape=(jax.ShapeDtypeStruct((m, D_MODEL), F32), jax.ShapeDtypeStruct((m, D_MODEL), BF16)),
        grid_spec=grid_spec,
        compiler_params=_cparams(("arbitrary",)),
        name="combine_ln",
    )(pos, yb, x1, wts, g, b)


def _dispatch_plan(ids, tm):
    t = ids.shape[0]
    flat_e = ids.reshape(-1)
    n_assign = flat_e.shape[0]
    onehot = (flat_e[:, None] == jnp.arange(N_EXPERTS, dtype=jnp.int32)[None, :]).astype(jnp.int32)
    csum = jnp.cumsum(onehot, axis=0)
    rank = jnp.sum((csum - onehot) * onehot, axis=1)
    counts = csum[-1]
    padded = (counts + tm - 1) // tm * tm
    pend = jnp.cumsum(padded)
    pstart = pend - padded
    dest = jnp.sum(onehot * pstart[None, :], axis=1) + rank
    n_rows = n_assign + N_EXPERTS * tm
    n_blocks = n_rows // tm
    flat_tok = jnp.arange(n_assign, dtype=jnp.int32) // TOP_K
    row_tok = (jnp.arange(n_rows, dtype=jnp.int32) % t).at[dest].set(flat_tok, unique_indices=True)
    blk_start = jnp.arange(n_blocks, dtype=jnp.int32) * tm
    blk_e = jnp.minimum(jnp.sum((pend[None, :] <= blk_start[:, None]).astype(jnp.int32), axis=1),
                        N_EXPERTS - 1).astype(jnp.int32)
    n_valid = (pend[-1] // tm).astype(jnp.int32)
    blk = jnp.arange(n_blocks, dtype=jnp.int32)
    is_start = jnp.logical_and(blk < n_valid, jnp.logical_or(blk == 0, blk_e != jnp.roll(blk_e, 1)))
    run_id = jnp.cumsum(is_start.astype(jnp.int32)) - 1
    start_pos = jnp.where(is_start, blk, n_blocks)
    next_start = lax.cummin(jnp.concatenate([start_pos[1:], jnp.full((1,), n_blocks, jnp.int32)]), reverse=True)
    next_e = jnp.where(next_start < n_blocks, blk_e[jnp.minimum(next_start, n_blocks - 1)], -1)
    return dict(row_tok=row_tok, blk_e=blk_e, n_valid=n_valid.reshape(1), run_id=run_id.astype(jnp.int32),
                next_e=next_e.astype(jnp.int32), dest=dest.astype(jnp.int32))


def _rope_tables(s):
    half = HEAD_DIM // 2
    inv = (np.float32(ROPE_BASE) ** (-np.arange(half, dtype=np.float32) / np.float32(half))).astype(np.float32)
    ang = np.arange(s, dtype=np.float32)[:, None] * inv[None, :]
    cos, sin = np.cos(ang), np.sin(ang)
    return (jnp.asarray(np.concatenate([cos, cos], axis=1), F32),
            jnp.asarray(np.concatenate([-sin, sin], axis=1), F32))


TILES = dict(proj_rows=2048, fox_q=512, fox_k=1024,
             retention_rows=256, sb_rows=256, gla_rows=512, merge_rows=1024, merge_cols=256,
             out_ln_rows=512, router_rows=512, moe_rows=256, combine_rows=256)


def kernel(x, w_in, fox_forget_bias, gla_gate_up, gla_gate_bias, w_branch, w_gate, b_gate, w_out, ln1_g, ln1_b, w_group, b_group, w_expert_router, b_expert_router, w1, w3, w2, ln2_g, ln2_b):
    bsz, s, d = x.shape
    depth = w_in.shape[0]
    alpha = DEEPNORM_ALPHA
    assert bsz == 1 and d == D_MODEL and w_in.shape[2] == IN_WIDTH
    t = bsz * s
    tl = TILES

    w_in_b = w_in.astype(BF16)
    wo_b = w_out.astype(BF16)
    fbias = jnp.pad(fox_forget_bias, ((0, 0), (0, LANES - N_HEADS)))[:, None, :]
    up_pad = jnp.pad(gla_gate_up, ((0, 0), (S_GA, LANES - S_GA - GLA_GATE_RANK), (0, 0)))
    w_r = jnp.pad(jnp.concatenate([w_group, w_expert_router], axis=2),
                  ((0, 0), (0, 0), (0, LANES - N_GROUPS - N_EXPERTS)))
    b_r = jnp.pad(jnp.concatenate([b_group, b_expert_router], axis=1),
                  ((0, 0), (0, LANES - N_GROUPS - N_EXPERTS)))[:, None, :]
    cos_t, sin_t = _rope_tables(s)

    xf = x.reshape(t, d)
    xb = xf.astype(BF16)
    for l in range(depth):
        pm = _proj_main(xb, w_in_b, l, tl["proj_rows"])
        ps, qa, ka, va, st = _fox_prep(xb, w_in_b, l, fbias[l], pm, tl["fox_q"])
        y_fox = _fox_attention(qa, ka, va, st, tl["fox_q"], tl["fox_k"])
        y_ret = _retention(pm, cos_t, sin_t, tl["retention_rows"])
        y_sb = _sb_attention(pm, tl["sb_rows"])
        y_gla = _gla(pm, ps, up_pad[l], gla_gate_bias[l][None, :], tl["gla_rows"])
        merged = _merge(xb, (y_fox, y_ret, y_sb, y_gla), w_gate, b_gate[l][:, None, :], w_branch, l,
                        tl["merge_rows"], tl["merge_cols"])
        x1, x1b3 = _out_ln(merged, wo_b, l, xf, ln1_g[l][None, :], ln1_b[l][None, :], alpha, tl["out_ln_rows"])

        ids, wts = _router(x1, w_r[l], b_r[l], tl["router_rows"])
        plan = _dispatch_plan(ids[:, :TOP_K], tl["moe_rows"])
        yb3 = _experts(plan, x1b3, w1, w3, w2, l, tl["moe_rows"])
        xf, xb = _combine_ln(plan["dest"], yb3, x1, wts, ln2_g[l][None, :], ln2_b[l][None, :], alpha,
                             tl["combine_rows"])
    return xf.reshape(bsz, s, d)
```
